```python
import math
import jax, jax.numpy as jnp
from jax import lax
import numpy as np

D_MODEL = 2048
BATCH = 4
SEQ = 2048
DEPTH = 4

MLA_HEADS = 8
MLA_NOPE = 128
MLA_ROPE = 64
MLA_V = 128
MLA_Q_LORA = 512
MLA_KV_LORA = 256
MLA_WIDTH = MLA_HEADS * MLA_V
ROPE_THETA = 10000.0

SSM_WIDTH = D_MODEL // 4
SSM_GROUP = 16
SSM_GROUPS = SSM_WIDTH // SSM_GROUP
SSM_STATE = 64

DIL_WIDTH = D_MODEL // 4
DIL_HEAD_DIM = 64
DIL_HEADS = DIL_WIDTH // DIL_HEAD_DIM
DIL_PATTERNS = ((128, 1), (512, 4), (2048, 16))

BLOCK = 128
MIX_WIDTH = MLA_WIDTH + SSM_WIDTH + DIL_WIDTH
IN_SPLITS = (MLA_Q_LORA, MLA_KV_LORA, MLA_ROPE, SSM_WIDTH, DIL_WIDTH, DIL_WIDTH, DIL_WIDTH)
IN_WIDTH = sum(IN_SPLITS)
D_FF = ((8 * D_MODEL + 3 * 256 - 1) // (3 * 256)) * 256
NORM_EPS = 1e-6

kernel_name = "hymba_mla_s5_dilated_hybrid"


def rms_norm(x, g):
    xf = x.astype(jnp.float32)
    y = xf * lax.rsqrt(jnp.mean(xf * xf, axis=-1, keepdims=True) + NORM_EPS)
    return (y * g.astype(jnp.float32)).astype(x.dtype)


def apply_rope(x, pos):
    half = x.shape[-1] // 2
    inv_freq = ROPE_THETA ** (-jnp.arange(half, dtype=jnp.float32) / half)
    ang = pos.astype(jnp.float32)[:, None] * inv_freq[None, :]
    cos = jnp.cos(ang)[None, :, None, :]
    sin = jnp.sin(ang)[None, :, None, :]
    xf = x.astype(jnp.float32)
    x1, x2 = xf[..., :half], xf[..., half:]
    return jnp.concatenate([x1 * cos - x2 * sin, x2 * cos + x1 * sin], axis=-1).astype(x.dtype)


def mla_mixer(c_q, c_kv, k_rope, g_q, w_uq, g_kv, w_ukv):
    B, S, _ = c_q.shape
    pos = jnp.arange(S)
    q = (rms_norm(c_q, g_q) @ w_uq).reshape(B, S, MLA_HEADS, MLA_NOPE + MLA_ROPE)
    q_nope = q[..., :MLA_NOPE]
    q_pe = apply_rope(q[..., MLA_NOPE:], pos)
    kv = (rms_norm(c_kv, g_kv) @ w_ukv).reshape(B, S, MLA_HEADS, MLA_NOPE + MLA_V)
    k_nope, v = kv[..., :MLA_NOPE], kv[..., MLA_NOPE:]
    k_pe = apply_rope(k_rope[:, :, None, :], pos)[:, :, 0]
    scale = (MLA_NOPE + MLA_ROPE) ** -0.5
    nb = S // BLOCK
    qn_b = q_nope.reshape(B, nb, BLOCK, MLA_HEADS, MLA_NOPE).transpose(1, 0, 2, 3, 4)
    qp_b = q_pe.reshape(B, nb, BLOCK, MLA_HEADS, MLA_ROPE).transpose(1, 0, 2, 3, 4)
    kpos = jnp.arange(S)

    def one_block(args):
        b, qn, qp = args
        s = (jnp.einsum('bqhd,bkhd->bhqk', qn, k_nope).astype(jnp.float32)
             + jnp.einsum('bqhd,bkd->bhqk', qp, k_pe).astype(jnp.float32)) * scale
        qpos = b * BLOCK + jnp.arange(BLOCK)
        causal = qpos[:, None] >= kpos[None, :]
        s = jnp.where(causal[None, None], s, -jnp.inf)
        p = jax.nn.softmax(s, axis=-1).astype(v.dtype)
        return jnp.einsum('bhqk,bkhd->bqhd', p, v)

    out = lax.map(one_block, (jnp.arange(nb), qn_b, qp_b))
    return out.transpose(1, 0, 2, 3, 4).reshape(B, S, MLA_WIDTH)


def s5_mixer(u, a_re, a_im, b_re, b_im, c_re, c_im, d_skip, log_dt, w_glu, b_glu):
    B, S, _ = u.shape
    f32 = jnp.float32
    uf = u.astype(f32).reshape(B, S, SSM_GROUPS, SSM_GROUP)
    lam = lax.complex(jnp.minimum(a_re.astype(f32), -1e-4), a_im.astype(f32))
    dt = jnp.exp(log_dt.astype(f32))[:, None]
    a_bar = jnp.exp(lam * dt)
    b_cplx = lax.complex(b_re.astype(f32), b_im.astype(f32))
    b_bar = ((a_bar - 1.0) / lam)[..., None] * b_cplx
    bu = jnp.einsum('gnp,bsgp->bsgn', b_bar, uf.astype(jnp.complex64))
    a_seq = jnp.broadcast_to(a_bar, bu.shape)

    def combine(left, right):
        a_l, h_l = left
        a_r, h_r = right
        return a_r * a_l, a_r * h_l + h_r

    _, h = lax.associative_scan(combine, (a_seq, bu), axis=1)
    c_cplx = lax.complex(c_re.astype(f32), c_im.astype(f32))
    y = jnp.einsum('gpn,bsgn->bsgp', c_cplx, h).real + d_skip.astype(f32) * uf
    y = jax.nn.gelu(y.reshape(B, S, SSM_WIDTH))
    z = y @ w_glu.astype(f32) + b_glu.astype(f32)
    out = z[..., :SSM_WIDTH] * jax.nn.sigmoid(z[..., SSM_WIDTH:])
    return out.astype(u.dtype)


def strided_fold(x, dil):
    B, S = x.shape[:2]
    rest = x.shape[2:]
    return x.reshape(B, S // dil, dil, *rest).swapaxes(1, 2).reshape(B * dil, S // dil, *rest)


def strided_unfold(x, batch, dil):
    L = x.shape[1]
    rest = x.shape[2:]
    return x.reshape(batch, dil, L, *rest).swapaxes(1, 2).reshape(batch, L * dil, *rest)


def banded_window_attention(q, k, v, span):
    Z, L, H, D = q.shape
    nb = -(-L // BLOCK)
    Lp = nb * BLOCK
    pad = ((0, 0), (0, Lp - L), (0, 0), (0, 0))
    qb, kb, vb = [jnp.pad(t, pad).reshape(Z, nb, BLOCK, H, D) for t in (q, k, v)]

    def with_prev(t):
        prev = jnp.pad(t, ((0, 0), (1, 0), (0, 0), (0, 0), (0, 0)))[:, :-1]
        return jnp.concatenate([prev, t], axis=2)

    kk, vv = with_prev(kb), with_prev(vb)
    s = jnp.einsum('znqhd,znkhd->znhqk', qb, kk).astype(jnp.float32) * (D ** -0.5)
    qpos = jnp.arange(nb)[:, None] * BLOCK + jnp.arange(BLOCK)[None, :]
    kpos = (jnp.arange(nb)[:, None] - 1) * BLOCK + jnp.arange(2 * BLOCK)[None, :]
    dist = qpos[:, :, None] - kpos[:, None, :]
    mask = (dist >= 0) & (dist <= span) & (kpos[:, None, :] >= 0)
    s = jnp.where(mask[None, :, None], s, -jnp.inf)
    m = jnp.max(s, axis=-1, keepdims=True)
    p = jnp.exp(s - m)
    l = jnp.sum(p, axis=-1, keepdims=True)
    o = jnp.einsum('znhqk,znkhd->znqhd', (p / l).astype(v.dtype), vv)
    lse = (m + jnp.log(l))[..., 0].transpose(0, 1, 3, 2).reshape(Z, Lp, H)
    return o.reshape(Z, Lp, H, D)[:, :L], lse[:, :L]


def dilated_mixer(qd, kd, vd):
    B, S, _ = qd.shape
    q, k, v = [t.reshape(B, S, DIL_HEADS, DIL_HEAD_DIM) for t in (qd, kd, vd)]
    outs, lses = [], []
    for window, dil in DIL_PATTERNS:
        o, lse = banded_window_attention(strided_fold(q, dil), strided_fold(k, dil),
                                         strided_fold(v, dil), window // dil)
        outs.append(strided_unfold(o, B, dil).astype(jnp.float32))
        lses.append(strided_unfold(lse, B, dil))
    wts = jax.nn.softmax(jnp.stack(lses, axis=0), axis=0)
    out = wts[0][..., None] * outs[0] + wts[1][..., None] * outs[1] + wts[2][..., None] * outs[2]
    return out.reshape(B, S, DIL_WIDTH).astype(qd.dtype)


def setup_inputs(seed: int = 0) -> dict:
    key = jax.random.key(seed)
    ks = jax.random.split(key, 32)
    L = DEPTH
    nrm = lambda k, shape, scale: jax.random.normal(k, shape, jnp.float32) * scale
    gain = lambda k, n: 1.0 + 0.02 * jax.random.normal(k, (L, n), jnp.float32)
    out_scale = (2 * DEPTH) ** -0.5
    n_idx = jnp.arange(SSM_STATE, dtype=jnp.float32)
    return {
        "x": jax.random.normal(ks[0], (BATCH, SEQ, D_MODEL), jnp.float32),
        "g_mix": gain(ks[1], D_MODEL),
        "w_in": nrm(ks[2], (L, D_MODEL, IN_WIDTH), D_MODEL ** -0.5),
        "g_q": gain(ks[3], MLA_Q_LORA),
        "w_uq": nrm(ks[4], (L, MLA_Q_LORA, MLA_HEADS * (MLA_NOPE + MLA_ROPE)), MLA_Q_LORA ** -0.5),
        "g_kv": gain(ks[5], MLA_KV_LORA),
        "w_ukv": nrm(ks[6], (L, MLA_KV_LORA, MLA_HEADS * (MLA_NOPE + MLA_V)), MLA_KV_LORA ** -0.5),
        "a_re": -0.5 + nrm(ks[7], (L, SSM_GROUPS, SSM_STATE), 0.01),
        "a_im": math.pi * n_idx + nrm(ks[8], (L, SSM_GROUPS, SSM_STATE), 0.01),
        "b_re": nrm(ks[9], (L, SSM_GROUPS, SSM_STATE, SSM_GROUP), (2 * SSM_GROUP) ** -0.5),
        "b_im": nrm(ks[10], (L, SSM_GROUPS, SSM_STATE, SSM_GROUP), (2 * SSM_GROUP) ** -0.5),
        "c_re": nrm(ks[11], (L, SSM_GROUPS, SSM_GROUP, SSM_STATE), 0.5),
        "c_im": nrm(ks[12], (L, SSM_GROUPS, SSM_GROUP, SSM_STATE), 0.5),
        "d_skip": nrm(ks[13], (L, SSM_GROUPS, SSM_GROUP), 1.0),
        "log_dt": jax.random.uniform(ks[14], (L, SSM_GROUPS), jnp.float32,
                                     math.log(1e-3), math.log(1e-1)),
        "w_glu": nrm(ks[15], (L, SSM_WIDTH, 2 * SSM_WIDTH), SSM_WIDTH ** -0.5),
        "b_glu": nrm(ks[16], (L, 2 * SSM_WIDTH), 0.01),
        "g_out_mla": gain(ks[17], MLA_WIDTH),
        "g_out_ssm": gain(ks[18], SSM_WIDTH),
        "g_out_dil": gain(ks[19], DIL_WIDTH),
        "w_o": nrm(ks[20], (L, MIX_WIDTH, D_MODEL), MIX_WIDTH ** -0.5 * out_scale),
        "g_ffn": gain(ks[21], D_MODEL),
        "w_gate": nrm(ks[22], (L, D_MODEL, D_FF), D_MODEL ** -0.5),
        "w_up": nrm(ks[23], (L, D_MODEL, D_FF), D_MODEL ** -0.5),
        "w_down": nrm(ks[24], (L, D_FF, D_MODEL), D_FF ** -0.5 * out_scale),
        "g_final": 1.0 + 0.02 * jax.random.normal(ks[25], (D_MODEL,), jnp.float32),
    }


def reference(x, g_mix, w_in, g_q, w_uq, g_kv, w_ukv, a_re, a_im, b_re, b_im, c_re, c_im,
              d_skip, log_dt, w_glu, b_glu, g_out_mla, g_out_ssm, g_out_dil, w_o,
              g_ffn, w_gate, w_up, w_down, g_final):
    split_at = np.cumsum(IN_SPLITS)[:-1].tolist()
    for l in range(DEPTH):
        h = rms_norm(x, g_mix[l])
        proj = h @ w_in[l]
        c_q, c_kv, k_rope, u, qd, kd, vd = jnp.split(proj, split_at, axis=-1)
        y_mla = mla_mixer(c_q, c_kv, k_rope, g_q[l], w_uq[l], g_kv[l], w_ukv[l])
        y_ssm = s5_mixer(u, a_re[l], a_im[l], b_re[l], b_im[l], c_re[l], c_im[l],
                         d_skip[l], log_dt[l], w_glu[l], b_glu[l])
        y_dil = dilated_mixer(qd, kd, vd)
        y = jnp.concatenate([rms_norm(y_mla, g_out_mla[l]),
                             rms_norm(y_ssm, g_out_ssm[l]),
                             rms_norm(y_dil, g_out_dil[l])], axis=-1)
        x = x + y @ w_o[l]
        h = rms_norm(x, g_ffn[l])
        x = x + (jax.nn.silu(h @ w_gate[l]) * (h @ w_up[l])) @ w_down[l]
    return rms_norm(x, g_final)
```

```python
import functools
import math

import jax
import jax.numpy as jnp
import numpy as np
from jax import lax
from jax.experimental import pallas as pl
from jax.experimental.pallas import tpu as pltpu

F32 = jnp.float32
BF16 = jnp.bfloat16

D_MODEL = 2048
BATCH = 4
SEQ = 2048
DEPTH = 4
TOKENS = BATCH * SEQ

MLA_HEADS = 8
MLA_NOPE = 128
MLA_ROPE = 64
MLA_V = 128
MLA_Q_LORA = 512
MLA_KV_LORA = 256
MLA_WIDTH = MLA_HEADS * MLA_V
MLA_QK_PAD = 256
ROPE_THETA = 10000.0

SSM_WIDTH = 512
SSM_GROUP = 16
SSM_GROUPS = 32
SSM_STATE = 64
SSM_CHUNK = 16
SSM_ROW = SSM_CHUNK * SSM_GROUP

DIL_WIDTH = 512
DIL_HEAD_DIM = 64
DIL_HEADS = 8
DIL_PATTERNS = ((128, 1), (512, 4), (2048, 16))
BLOCK = 128

IN_PAD = 3072
D_FF = 5632
NORM_EPS = 1e-6

LANES = 128
VMEM_LIMIT_BYTES = 56 * 1024 * 1024

NT_DIMS = (((1,), (1,)), ((), ()))


def _cparams(*semantics):
    return pltpu.CompilerParams(dimension_semantics=semantics,
                                vmem_limit_bytes=VMEM_LIMIT_BYTES)


def _rms_rows(x, g):
    ms = jnp.mean(x * x, axis=-1, keepdims=True)
    return x * lax.rsqrt(ms + NORM_EPS) * g


def _norm_into(h_ref, x_ref, g_ref, col0=0, chunk=256):
    rows, width = x_ref.shape
    g = g_ref[...]
    for r in range(0, rows, chunk):
        x = x_ref[r:r + chunk, :]
        h_ref[r:r + chunk, col0:col0 + width] = _rms_rows(x, g).astype(BF16)


def _norm_matmul_kernel(x_ref, g_ref, w_ref, o_ref, h_ref):
    @pl.when(pl.program_id(1) == 0)
    def _():
        _norm_into(h_ref, x_ref, g_ref)

    o_ref[...] = jnp.dot(h_ref[...], w_ref[...].astype(BF16),
                         preferred_element_type=F32).astype(o_ref.dtype)


def _norm_matmul(x, g, w, *, bm, bn, out_dtype, name):
    m, k = x.shape
    n = w.shape[1]
    return pl.pallas_call(
        _norm_matmul_kernel,
        grid=(m // bm, n // bn),
        in_specs=[pl.BlockSpec((bm, k), lambda i, j: (i, 0)),
                  pl.BlockSpec((1, k), lambda i, j: (0, 0)),
                  pl.BlockSpec((k, bn), lambda i, j: (0, j))],
        out_specs=pl.BlockSpec((bm, bn), lambda i, j: (i, j)),
        out_shape=jax.ShapeDtypeStruct((m, n), out_dtype),
        scratch_shapes=[pltpu.VMEM((bm, k), BF16)],
        compiler_params=_cparams("parallel", "arbitrary"),
        name=name,
    )(x, g.reshape(1, k), w)


def _rope_pad(x, cos_t, sin_a, sin_b):
    up = pltpu.roll(x, MLA_QK_PAD - MLA_ROPE // 2, axis=1)
    dn = pltpu.roll(x, MLA_ROPE // 2, axis=1)
    return x * cos_t + up * sin_a + dn * sin_b


def _q_up_kernel(c_ref, g_ref, w_ref, cos_ref, sina_ref, sinb_ref, o_ref):
    h = _rms_rows(c_ref[...], g_ref[...]).astype(BF16)
    q = jnp.dot(h, w_ref[...].astype(BF16), preferred_element_type=F32)
    cos_t, sin_a, sin_b = cos_ref[...], sina_ref[...], sinb_ref[...]
    for hd in range(MLA_HEADS):
        sl = slice(hd * MLA_QK_PAD, (hd + 1) * MLA_QK_PAD)
        o_ref[:, sl] = _rope_pad(q[:, sl], cos_t, sin_a, sin_b).astype(o_ref.dtype)


def _q_up(proj, g_q, w_q, tabs, *, bm):
    pos_blocks = SEQ // bm
    tab_spec = pl.BlockSpec((bm, MLA_QK_PAD), lambda i: (i % pos_blocks, 0))
    n = MLA_HEADS * MLA_QK_PAD
    return pl.pallas_call(
        _q_up_kernel,
        grid=(TOKENS // bm,),
        in_specs=[pl.BlockSpec((bm, MLA_Q_LORA), lambda i: (i, 0)),
                  pl.BlockSpec((1, MLA_Q_LORA), lambda i: (0, 0)),
                  pl.BlockSpec((MLA_Q_LORA, n), lambda i: (0, 0)),
                  tab_spec, tab_spec, tab_spec],
        out_specs=pl.BlockSpec((bm, n), lambda i: (i, 0)),
        out_shape=jax.ShapeDtypeStruct((TOKENS, n), BF16),
        compiler_params=_cparams("parallel"),
        name="mla_q_up",
    )(proj, g_q.reshape(1, -1), w_q, *tabs)


def _kv_up_kernel(c_ref, kr_ref, g_ref, w_ref, cos_ref, sina_ref, sinb_ref, k_ref, v_ref):
    h = _rms_rows(c_ref[...], g_ref[...]).astype(BF16)
    kv = jnp.dot(h, w_ref[...].astype(BF16), preferred_element_type=F32)
    k_pe = _rope_pad(kr_ref[...], cos_ref[...], sina_ref[...], sinb_ref[...])
    nk = MLA_HEADS * MLA_QK_PAD
    for hd in range(MLA_HEADS):
        sl = slice(hd * MLA_QK_PAD, (hd + 1) * MLA_QK_PAD)
        k_ref[:, sl] = (kv[:, sl] + k_pe).astype(k_ref.dtype)
    v_ref[...] = kv[:, nk:].astype(v_ref.dtype)


def _kv_up(proj, g_kv, w_kv, tabs, *, bm):
    pos_blocks = SEQ // bm
    tab_spec = pl.BlockSpec((bm, MLA_QK_PAD), lambda i: (i % pos_blocks, 0))
    nk = MLA_HEADS * MLA_QK_PAD
    return pl.pallas_call(
        _kv_up_kernel,
        grid=(TOKENS // bm,),
        in_specs=[pl.BlockSpec((bm, MLA_KV_LORA), lambda i: (i, 2)),
                  pl.BlockSpec((bm, MLA_QK_PAD), lambda i: (i, 3)),
                  pl.BlockSpec((1, MLA_KV_LORA), lambda i: (0, 0)),
                  pl.BlockSpec((MLA_KV_LORA, nk + MLA_WIDTH), lambda i: (0, 0)),
                  tab_spec, tab_spec, tab_spec],
        out_specs=[pl.BlockSpec((bm, nk), lambda i: (i, 0)),
                   pl.BlockSpec((bm, MLA_WIDTH), lambda i: (i, 0))],
        out_shape=[jax.ShapeDtypeStruct((TOKENS, nk), BF16),
                   jax.ShapeDtypeStruct((TOKENS, MLA_WIDTH), BF16)],
        compiler_params=_cparams("parallel"),
        name="mla_kv_up",
    )(proj, proj, g_kv.reshape(1, -1), w_kv, *tabs)


ATT_BQ = 256
ATT_BK = 256


def _mla_attn_kernel(q_ref, k_ref, v_ref, o_ref):
    row = lax.broadcasted_iota(jnp.int32, (ATT_BQ, ATT_BK), 0)
    col = lax.broadcasted_iota(jnp.int32, (ATT_BQ, ATT_BK), 1)
    causal = row >= col

    def step(q, kj, vj, carry, mask):
        m, l, acc = carry
        s = lax.dot_general(q, kj, NT_DIMS, preferred_element_type=F32)
        if mask is not None:
            s = jnp.where(mask, s, -jnp.inf)
        m_new = jnp.maximum(m, jnp.max(s, axis=-1, keepdims=True))
        p = jnp.exp(s - m_new)
        alpha = jnp.exp(m - m_new)
        l = alpha * l + jnp.sum(p, axis=-1, keepdims=True)
        acc = alpha * acc + jnp.dot(p.astype(BF16), vj, preferred_element_type=F32)
        return m_new, l, acc

    for qi in range(SEQ // ATT_BQ):
        q = q_ref[qi * ATT_BQ:(qi + 1) * ATT_BQ, :]

        def body(j, carry, q=q):
            off = pl.multiple_of(j * ATT_BK, ATT_BK)
            return step(q, k_ref[pl.ds(off, ATT_BK), :], v_ref[pl.ds(off, ATT_BK), :],
                        carry, None)

        init = (jnp.full((ATT_BQ, 1), -jnp.inf, F32),
                jnp.zeros((ATT_BQ, 1), F32),
                jnp.zeros((ATT_BQ, MLA_V), F32))
        carry = lax.fori_loop(0, qi, body, init) if qi else init
        d0 = qi * ATT_BQ
        _, l, acc = step(q, k_ref[d0:d0 + ATT_BK, :], v_ref[d0:d0 + ATT_BK, :], carry, causal)
        o_ref[d0:d0 + ATT_BQ, :] = (acc / l).astype(o_ref.dtype)


def _mla_attention(q, k, v):
    return pl.pallas_call(
        _mla_attn_kernel,
        grid=(BATCH, MLA_HEADS),
        in_specs=[pl.BlockSpec((SEQ, MLA_QK_PAD), lambda b, h: (b, h)),
                  pl.BlockSpec((SEQ, MLA_QK_PAD), lambda b, h: (b, h)),
                  pl.BlockSpec((SEQ, MLA_V), lambda b, h: (b, h))],
        out_specs=pl.BlockSpec((SEQ, MLA_V), lambda b, h: (b, h)),
        out_shape=jax.ShapeDtypeStruct((TOKENS, MLA_WIDTH), F32),
        compiler_params=_cparams("parallel", "parallel"),
        name="mla_attention",
    )(q, k, v)


def _dil_attn_kernel(q_ref, k_ref, v_ref, o_ref, lse_ref, *, nb):
    row = lax.broadcasted_iota(jnp.int32, (BLOCK, 2 * BLOCK), 0)
    col = lax.broadcasted_iota(jnp.int32, (BLOCK, 2 * BLOCK), 1)
    dist = row + BLOCK - col
    band = (dist >= 0) & (dist <= BLOCK)
    scale = DIL_HEAD_DIM ** -0.5

    def block(n, _):
        r0 = pl.multiple_of(n * BLOCK, BLOCK)
        p0 = pl.multiple_of(jnp.maximum(n - 1, 0) * BLOCK, BLOCK)
        mask = band & (col >= jnp.where(n > 0, 0, BLOCK))
        q = q_ref[pl.ds(r0, BLOCK), :].astype(BF16)
        kk = jnp.concatenate([k_ref[pl.ds(p0, BLOCK), :], k_ref[pl.ds(r0, BLOCK), :]],
                             axis=0).astype(BF16)
        vv = jnp.concatenate([v_ref[pl.ds(p0, BLOCK), :], v_ref[pl.ds(r0, BLOCK), :]],
                             axis=0).astype(BF16)
        for pair in range(DIL_HEADS // 2):
            outs, lses = [], []
            for hd in (2 * pair, 2 * pair + 1):
                sl = slice(hd * DIL_HEAD_DIM, (hd + 1) * DIL_HEAD_DIM)
                s = lax.dot_general(q[:, sl], kk[:, sl], NT_DIMS,
                                    preferred_element_type=F32) * scale
                s = jnp.where(mask, s, -jnp.inf)
                m = jnp.max(s, axis=-1, keepdims=True)
                p = jnp.exp(s - m)
                l = jnp.sum(p, axis=-1, keepdims=True)
                o = jnp.dot(p.astype(BF16), vv[:, sl], preferred_element_type=F32) / l
                outs.append(o)
                lses.append(jnp.broadcast_to(m + jnp.log(l), (BLOCK, DIL_HEAD_DIM)))
            lanes = slice(pair * LANES, (pair + 1) * LANES)
            o_ref[pl.ds(r0, BLOCK), lanes] = jnp.concatenate(outs, axis=-1)
            lse_ref[pl.ds(r0, BLOCK), lanes] = jnp.concatenate(lses, axis=-1)
        return 0

    lax.fori_loop(0, nb, block, 0)


def _dil_attention(proj, dil):
    length = SEQ // dil
    nb = length // BLOCK
    blocks_per_row = IN_PAD // DIL_WIDTH
    folded = proj.reshape(TOKENS // dil, dil * IN_PAD)

    def in_spec(which):
        return pl.BlockSpec((length, DIL_WIDTH),
                            lambda b, r: (b, r * blocks_per_row + which))

    out_spec = pl.BlockSpec((length, DIL_WIDTH), lambda b, r: (b, r))
    out_shape = jax.ShapeDtypeStruct((TOKENS // dil, dil * DIL_WIDTH), F32)
    o, lse = pl.pallas_call(
        functools.partial(_dil_attn_kernel, nb=nb),
        grid=(BATCH, dil),
        in_specs=[in_spec(3), in_spec(4), in_spec(5)],
        out_specs=[out_spec, out_spec],
        out_shape=[out_shape, out_shape],
        compiler_params=_cparams("parallel", "parallel"),
        name=f"dilated_attention_d{dil}",
    )(folded, folded, folded)
    return o.reshape(TOKENS, DIL_WIDTH), lse.reshape(TOKENS, DIL_WIDTH)


def _dil_combine_kernel(o0, l0, o1, l1, o2, l2, out_ref):
    a, b, c = l0[...], l1[...], l2[...]
    top = jnp.maximum(jnp.maximum(a, b), c)
    ea, eb, ec = jnp.exp(a - top), jnp.exp(b - top), jnp.exp(c - top)
    num = ea * o0[...] + eb * o1[...] + ec * o2[...]
    out_ref[...] = num / (ea + eb + ec)


def _dil_combine(parts, *, bm):
    spec = pl.BlockSpec((bm, DIL_WIDTH), lambda i: (i, 0))
    flat = [t for pair in parts for t in pair]
    return pl.pallas_call(
        _dil_combine_kernel,
        grid=(TOKENS // bm,),
        in_specs=[spec] * 6,
        out_specs=spec,
        out_shape=jax.ShapeDtypeStruct((TOKENS, DIL_WIDTH), F32),
        compiler_params=_cparams("parallel"),
        name="dilated_combine",
    )(*flat)


SSM_ROWS = TOKENS // SSM_CHUNK
SSM_NCHUNK = SEQ // SSM_CHUNK
HI = lax.Precision.HIGHEST


def _ssm_kernel(u_ref, t0_ref, m1re_ref, m1im_ref, m2re_ref, m2im_ref,
                are_ref, aim_ref, d_ref, y_ref, hre_ref, him_ref):
    u = u_ref[0]
    hre_ref[...] = jnp.dot(u, m1re_ref[0], precision=HI, preferred_element_type=F32)
    him_ref[...] = jnp.dot(u, m1im_ref[0], precision=HI, preferred_element_type=F32)
    a_re = jnp.broadcast_to(are_ref[0], (BATCH, SSM_STATE))
    a_im = jnp.broadcast_to(aim_ref[0], (BATCH, SSM_STATE))
    s_re = jnp.zeros((BATCH, SSM_STATE), F32)
    s_im = jnp.zeros((BATCH, SSM_STATE), F32)
    for c in range(SSM_NCHUNK):
        rows = slice(c * BATCH, (c + 1) * BATCH)
        loc_re, loc_im = hre_ref[rows, :], him_ref[rows, :]
        hre_ref[rows, :] = s_re
        him_ref[rows, :] = s_im
        s_re, s_im = (a_re * s_re - a_im * s_im + loc_re,
                      a_re * s_im + a_im * s_re + loc_im)
    y = (jnp.dot(u, t0_ref[0], precision=HI, preferred_element_type=F32)
         + jnp.dot(hre_ref[...], m2re_ref[0], precision=HI, preferred_element_type=F32)
         + jnp.dot(him_ref[...], m2im_ref[0], precision=HI, preferred_element_type=F32)
         + u * d_ref[0])
    y_ref[0] = jax.nn.gelu(y, approximate=True)


def _ssm_scan(u_rows, mats):
    t0, m1re, m1im, m2re, m2im, a_re, a_im, dvec = mats

    def spec(r, c):
        return pl.BlockSpec((1, r, c), lambda g: (g, 0, 0))

    return pl.pallas_call(
        _ssm_kernel,
        grid=(SSM_GROUPS,),
        in_specs=[spec(SSM_ROWS, SSM_ROW), spec(SSM_ROW, SSM_ROW),
                  spec(SSM_ROW, SSM_STATE), spec(SSM_ROW, SSM_STATE),
                  spec(SSM_STATE, SSM_ROW), spec(SSM_STATE, SSM_ROW),
                  spec(1, SSM_STATE), spec(1, SSM_STATE), spec(1, SSM_ROW)],
        out_specs=spec(SSM_ROWS, SSM_ROW),
        out_shape=jax.ShapeDtypeStruct((SSM_GROUPS, SSM_ROWS, SSM_ROW), F32),
        scratch_shapes=[pltpu.VMEM((SSM_ROWS, SSM_STATE), F32),
                        pltpu.VMEM((SSM_ROWS, SSM_STATE), F32)],
        compiler_params=_cparams("parallel"),
        name="ssm_chunk_scan",
    )(u_rows, t0, m1re, m1im, m2re, m2im, a_re, a_im, dvec)


def _ssm_matrices(a_re, a_im, b_re, b_im, c_re, c_im, d_skip, log_dt):
    lam_re = jnp.minimum(a_re, -1e-4)
    lam_im = a_im
    dt = jnp.exp(log_dt)[:, None]
    mag = jnp.exp(lam_re * dt)
    ab_re, ab_im = mag * jnp.cos(lam_im * dt), mag * jnp.sin(lam_im * dt)
    n_re, n_im = ab_re - 1.0, ab_im
    den = lam_re * lam_re + lam_im * lam_im
    f_re = (n_re * lam_re + n_im * lam_im) / den
    f_im = (n_im * lam_re - n_re * lam_im) / den
    bb_re = f_re[..., None] * b_re - f_im[..., None] * b_im
    bb_im = f_re[..., None] * b_im + f_im[..., None] * b_re
    p_re, p_im = [jnp.ones_like(ab_re)], [jnp.zeros_like(ab_im)]
    for _ in range(SSM_CHUNK):
        p_re.append(p_re[-1] * ab_re - p_im[-1] * ab_im)
        p_im.append(p_re[-2] * ab_im + p_im[-1] * ab_re)
    pw_re, pw_im = jnp.stack(p_re), jnp.stack(p_im)

    cb_re = (c_re[:, None, :, :] * bb_re.transpose(0, 2, 1)[:, :, None, :]
             - c_im[:, None, :, :] * bb_im.transpose(0, 2, 1)[:, :, None, :])
    cb_im = (c_re[:, None, :, :] * bb_im.transpose(0, 2, 1)[:, :, None, :]
             + c_im[:, None, :, :] * bb_re.transpose(0, 2, 1)[:, :, None, :])
    lag_re = pw_re[:SSM_CHUNK].transpose(1, 0, 2)[:, :, None, None, :]
    lag_im = pw_im[:SSM_CHUNK].transpose(1, 0, 2)[:, :, None, None, :]
    kern = jnp.sum(lag_re * cb_re[:, None] - lag_im * cb_im[:, None], axis=-1)
    kern = jnp.concatenate([kern, jnp.zeros_like(kern[:, :1])], axis=1)
    t_idx = np.arange(SSM_CHUNK)
    lag = t_idx[None, :] - t_idx[:, None]
    lag = np.where(lag >= 0, lag, SSM_CHUNK)
    t0 = kern[:, lag]
    t0 = t0.transpose(0, 1, 3, 2, 4).reshape(SSM_GROUPS, SSM_ROW, SSM_ROW)

    back = pw_re[SSM_CHUNK - 1 - t_idx], pw_im[SSM_CHUNK - 1 - t_idx]
    bre_t, bim_t = bb_re.transpose(0, 2, 1), bb_im.transpose(0, 2, 1)
    m1re = (back[0].transpose(1, 0, 2)[:, :, None, :] * bre_t[:, None]
            - back[1].transpose(1, 0, 2)[:, :, None, :] * bim_t[:, None])
    m1im = (back[0].transpose(1, 0, 2)[:, :, None, :] * bim_t[:, None]
            + back[1].transpose(1, 0, 2)[:, :, None, :] * bre_t[:, None])
    m1re = m1re.reshape(SSM_GROUPS, SSM_ROW, SSM_STATE)
    m1im = m1im.reshape(SSM_GROUPS, SSM_ROW, SSM_STATE)

    fwd_re = pw_re[1:].transpose(1, 0, 2)[:, :, None, :]
    fwd_im = pw_im[1:].transpose(1, 0, 2)[:, :, None, :]
    ca_re = c_re[:, None] * fwd_re - c_im[:, None] * fwd_im
    ca_im = c_re[:, None] * fwd_im + c_im[:, None] * fwd_re
    m2re = ca_re.transpose(0, 3, 1, 2).reshape(SSM_GROUPS, SSM_STATE, SSM_ROW)
    m2im = (-ca_im).transpose(0, 3, 1, 2).reshape(SSM_GROUPS, SSM_STATE, SSM_ROW)

    a16_re = pw_re[SSM_CHUNK].reshape(SSM_GROUPS, 1, SSM_STATE)
    a16_im = pw_im[SSM_CHUNK].reshape(SSM_GROUPS, 1, SSM_STATE)
    dvec = jnp.tile(d_skip[:, None, :], (1, SSM_CHUNK, 1)).reshape(SSM_GROUPS, 1, SSM_ROW)
    return t0, m1re, m1im, m2re, m2im, a16_re, a16_im, dvec


def _glu_kernel(y_ref, w_ref, b_ref, o_ref):
    z = jnp.dot(y_ref[...].astype(BF16), w_ref[...].astype(BF16),
                preferred_element_type=F32) + b_ref[...]
    o_ref[...] = z[:, :SSM_WIDTH] * jax.nn.sigmoid(z[:, SSM_WIDTH:])


def _glu(y, w_glu, b_glu, *, bm):
    return pl.pallas_call(
        _glu_kernel,
        grid=(TOKENS // bm,),
        in_specs=[pl.BlockSpec((bm, SSM_WIDTH), lambda i: (i, 0)),
                  pl.BlockSpec((SSM_WIDTH, 2 * SSM_WIDTH), lambda i: (0, 0)),
                  pl.BlockSpec((1, 2 * SSM_WIDTH), lambda i: (0, 0))],
        out_specs=pl.BlockSpec((bm, SSM_WIDTH), lambda i: (i, 0)),
        out_shape=jax.ShapeDtypeStruct((TOKENS, SSM_WIDTH), F32),
        compiler_params=_cparams("parallel"),
        name="ssm_glu",
    )(y, w_glu, b_glu.reshape(1, -1))


def _out_proj_kernel(ya_ref, yb_ref, yc_ref, ga_ref, gb_ref, gc_ref, w_ref, x_ref,
                     o_ref, h_ref):
    @pl.when(pl.program_id(1) == 0)
    def _():
        _norm_into(h_ref, ya_ref, ga_ref, 0)
        _norm_into(h_ref, yb_ref, gb_ref, MLA_WIDTH)
        _norm_into(h_ref, yc_ref, gc_ref, MLA_WIDTH + SSM_WIDTH)

    o_ref[...] = x_ref[...] + jnp.dot(h_ref[...], w_ref[...].astype(BF16),
                                      preferred_element_type=F32)


def _out_proj(y_mla, y_ssm, y_dil, g_mla, g_ssm, g_dil, w_o, x, *, bm, bn):
    def rows(width):
        return pl.BlockSpec((bm, width), lambda i, j: (i, 0))

    def gain(width):
        return pl.BlockSpec((1, width), lambda i, j: (0, 0))

    return pl.pallas_call(
        _out_proj_kernel,
        grid=(TOKENS // bm, D_MODEL // bn),
        in_specs=[rows(MLA_WIDTH), rows(SSM_WIDTH), rows(DIL_WIDTH),
                  gain(MLA_WIDTH), gain(SSM_WIDTH), gain(DIL_WIDTH),
                  pl.BlockSpec((D_MODEL, bn), lambda i, j: (0, j)),
                  pl.BlockSpec((bm, bn), lambda i, j: (i, j))],
        out_specs=pl.BlockSpec((bm, bn), lambda i, j: (i, j)),
        out_shape=jax.ShapeDtypeStruct((TOKENS, D_MODEL), F32),
        scratch_shapes=[pltpu.VMEM((bm, D_MODEL), BF16)],
        compiler_params=_cparams("parallel", "arbitrary"),
        name="out_proj",
    )(y_mla, y_ssm, y_dil, g_mla.reshape(1, -1), g_ssm.reshape(1, -1),
      g_dil.reshape(1, -1), w_o, x)


def _ffn_up_kernel(x_ref, g_ref, wg_ref, wu_ref, o_ref, h_ref):
    @pl.when(pl.program_id(1) == 0)
    def _():
        _norm_into(h_ref, x_ref, g_ref)

    h = h_ref[...]
    gate = jnp.dot(h, wg_ref[...].astype(BF16), preferred_element_type=F32)
    up = jnp.dot(h, wu_ref[...].astype(BF16), preferred_element_type=F32)
    o_ref[...] = (jax.nn.silu(gate) * up).astype(o_ref.dtype)


def _ffn_up(x, g, w_gate, w_up, *, bm, bn):
    return pl.pallas_call(
        _ffn_up_kernel,
        grid=(TOKENS // bm, D_FF // bn),
        in_specs=[pl.BlockSpec((bm, D_MODEL), lambda i, j: (i, 0)),
                  pl.BlockSpec((1, D_MODEL), lambda i, j: (0, 0)),
                  pl.BlockSpec((D_MODEL, bn), lambda i, j: (0, j)),
                  pl.BlockSpec((D_MODEL, bn), lambda i, j: (0, j))],
        out_specs=pl.BlockSpec((bm, bn), lambda i, j: (i, j)),
        out_shape=jax.ShapeDtypeStruct((TOKENS, D_FF), BF16),
        scratch_shapes=[pltpu.VMEM((bm, D_MODEL), BF16)],
        compiler_params=_cparams("parallel", "arbitrary"),
        name="ffn_gate_up",
    )(x, g.reshape(1, -1), w_gate, w_up)


def _ffn_down_kernel(a_ref, w_ref, x_ref, o_ref):
    o_ref[...] = x_ref[...] + jnp.dot(a_ref[...], w_ref[...].astype(BF16),
                                      preferred_element_type=F32)


def _ffn_down(a, w_down, x, *, bm, bn):
    return pl.pallas_call(
        _ffn_down_kernel,
        grid=(TOKENS // bm, D_MODEL // bn),
        in_specs=[pl.BlockSpec((bm, D_FF), lambda i, j: (i, 0)),
                  pl.BlockSpec((D_FF, bn), lambda i, j: (0, j)),
                  pl.BlockSpec((bm, bn), lambda i, j: (i, j))],
        out_specs=pl.BlockSpec((bm, bn), lambda i, j: (i, j)),
        out_shape=jax.ShapeDtypeStruct((TOKENS, D_MODEL), F32),
        compiler_params=_cparams("parallel", "arbitrary"),
        name="ffn_down",
    )(a, w_down, x)


def _final_norm_kernel(x_ref, g_ref, o_ref):
    o_ref[...] = _rms_rows(x_ref[...], g_ref[...])


def _final_norm(x, g, *, bm):
    return pl.pallas_call(
        _final_norm_kernel,
        grid=(TOKENS // bm,),
        in_specs=[pl.BlockSpec((bm, D_MODEL), lambda i: (i, 0)),
                  pl.BlockSpec((1, D_MODEL), lambda i: (0, 0))],
        out_specs=pl.BlockSpec((bm, D_MODEL), lambda i: (i, 0)),
        out_shape=jax.ShapeDtypeStruct((TOKENS, D_MODEL), F32),
        compiler_params=_cparams("parallel"),
        name="final_norm",
    )(x, g.reshape(1, -1))


def _pad_w_in(w_in):
    lyr, d, _ = w_in.shape
    z = lambda n: jnp.zeros((lyr, d, n), w_in.dtype)
    return jnp.concatenate([w_in[..., :768], z(128), w_in[..., 768:832], z(64),
                            w_in[..., 832:]], axis=-1)


def _pad_w_uq(w_uq):
    lyr = w_uq.shape[0]
    w = w_uq.reshape(lyr, MLA_Q_LORA, MLA_HEADS, MLA_NOPE + MLA_ROPE)
    w = jnp.pad(w, ((0, 0), (0, 0), (0, 0), (0, MLA_QK_PAD - MLA_NOPE - MLA_ROPE)))
    return w.reshape(lyr, MLA_Q_LORA, MLA_HEADS * MLA_QK_PAD)


def _pad_w_ukv(w_ukv):
    lyr = w_ukv.shape[0]
    w = w_ukv.reshape(lyr, MLA_KV_LORA, MLA_HEADS, MLA_NOPE + MLA_V)
    wk = jnp.pad(w[..., :MLA_NOPE], ((0, 0), (0, 0), (0, 0), (0, MLA_QK_PAD - MLA_NOPE)))
    wv = w[..., MLA_NOPE:]
    return jnp.concatenate([wk.reshape(lyr, MLA_KV_LORA, -1),
                            wv.reshape(lyr, MLA_KV_LORA, -1)], axis=-1)


def _rope_tables(scale):
    half = MLA_ROPE // 2
    inv_freq = ROPE_THETA ** (-jnp.arange(half, dtype=F32) / half)
    ang = jnp.arange(SEQ, dtype=F32)[:, None] * inv_freq[None, :]
    cos, sin = jnp.cos(ang), jnp.sin(ang)
    one = jnp.ones((SEQ, MLA_NOPE), F32)
    z = lambda n: jnp.zeros((SEQ, n), F32)
    tail = MLA_QK_PAD - MLA_NOPE - MLA_ROPE
    cos_t = jnp.concatenate([one, cos, cos, z(tail)], axis=1) * scale
    sin_a = jnp.concatenate([z(MLA_NOPE), -sin, z(half), z(tail)], axis=1) * scale
    sin_b = jnp.concatenate([z(MLA_NOPE), z(half), sin, z(tail)], axis=1) * scale
    return cos_t, sin_a, sin_b


def _to_ssm_rows(u):
    u = u.reshape(BATCH, SSM_NCHUNK, SSM_CHUNK, SSM_GROUPS, SSM_GROUP)
    return u.transpose(3, 1, 0, 2, 4).reshape(SSM_GROUPS, SSM_ROWS, SSM_ROW)


def _from_ssm_rows(y):
    y = y.reshape(SSM_GROUPS, SSM_NCHUNK, BATCH, SSM_CHUNK, SSM_GROUP)
    return y.transpose(2, 1, 3, 0, 4).reshape(TOKENS, SSM_WIDTH)


def kernel(x, g_mix, w_in, g_q, w_uq, g_kv, w_ukv, a_re, a_im, b_re, b_im, c_re, c_im,
           d_skip, log_dt, w_glu, b_glu, g_out_mla, g_out_ssm, g_out_dil, w_o,
           g_ffn, w_gate, w_up, w_down, g_final):
    x = x.reshape(TOKENS, D_MODEL)
    w_in_p = _pad_w_in(w_in)
    w_q_p = _pad_w_uq(w_uq)
    w_kv_p = _pad_w_ukv(w_ukv)
    q_tabs = _rope_tables((MLA_NOPE + MLA_ROPE) ** -0.5)
    k_tabs = _rope_tables(1.0)

    for l in range(DEPTH):
        proj = _norm_matmul(x, g_mix[l], w_in_p[l], bm=1024, bn=512, out_dtype=F32,
                            name="in_proj")
        q = _q_up(proj, g_q[l], w_q_p[l], q_tabs, bm=512)
        k, v = _kv_up(proj, g_kv[l], w_kv_p[l], k_tabs, bm=512)
        y_mla = _mla_attention(q, k, v)
        mats = _ssm_matrices(a_re[l], a_im[l], b_re[l], b_im[l], c_re[l], c_im[l],
                             d_skip[l], log_dt[l])
        y = _ssm_scan(_to_ssm_rows(proj[:, 1024:1536]), mats)
        y_ssm = _glu(_from_ssm_rows(y), w_glu[l], b_glu[l], bm=1024)
        parts = [_dil_attention(proj, dil) for _, dil in DIL_PATTERNS]
        y_dil = _dil_combine(parts, bm=1024)
        x = _out_proj(y_mla, y_ssm, y_dil, g_out_mla[l], g_out_ssm[l], g_out_dil[l],
                      w_o[l], x, bm=1024, bn=512)
        act = _ffn_up(x, g_ffn[l], w_gate[l], w_up[l], bm=1024, bn=512)
        x = _ffn_down(act, w_down[l], x, bm=1024, bn=256)
    out = _final_norm(x, g_final, bm=512)
    return out.reshape(BATCH, SEQ, D_MODEL)
```

```python
import functools
import math

import jax
import jax.numpy as jnp
import numpy as np
from jax import lax
from jax.experimental import pallas as pl
from jax.experimental.pallas import tpu as pltpu

F32 = jnp.float32
BF16 = jnp.bfloat16

D_MODEL = 2048
BATCH = 4
SEQ = 2048
DEPTH = 4
TOKENS = BATCH * SEQ

MLA_HEADS = 8
MLA_NOPE = 128
MLA_ROPE = 64
MLA_V = 128
MLA_Q_LORA = 512
MLA_KV_LORA = 256
MLA_WIDTH = MLA_HEADS * MLA_V
MLA_QK_PAD = 256
ROPE_THETA = 10000.0
ROPE_HALF = MLA_ROPE // 2

SSM_WIDTH = 512
SSM_GROUP = 16
SSM_GROUPS = 32
SSM_STATE = 64
SSM_CHUNK = 16
SSM_ROW = SSM_CHUNK * SSM_GROUP

DIL_WIDTH = 512
DIL_HEAD_DIM = 64
DIL_HEADS = 8
DIL_PATTERNS = ((128, 1), (512, 4), (2048, 16))
BLOCK = 128

IN_PAD = 3072
DIL_Q_COL = 1536
DIL_K_COL = 2048
DIL_V_COL = 2560
SSM_U_COL = 1024
D_FF = 5632
NORM_EPS = 1e-6

LANES = 128
VMEM_LIMIT_BYTES = 56 * 1024 * 1024

NT_DIMS = (((1,), (1,)), ((), ()))


def _cparams(*semantics):
    return pltpu.CompilerParams(dimension_semantics=semantics,
                                vmem_limit_bytes=VMEM_LIMIT_BYTES)


def _rms_rows(x, g):
    ms = jnp.mean(x * x, axis=-1, keepdims=True)
    return x * lax.rsqrt(ms + NORM_EPS) * g


def _norm_into(h_ref, x_ref, g_ref, col0=0, chunk=256):
    rows, width = x_ref.shape
    g = g_ref[...]
    for r in range(0, rows, chunk):
        x = x_ref[r:r + chunk, :]
        h_ref[r:r + chunk, col0:col0 + width] = _rms_rows(x, g).astype(BF16)


def _bf16(w):
    return w if w.dtype == BF16 else w.astype(BF16)


def _layer_spec(shape, index_map):
    return pl.BlockSpec((None,) + tuple(shape), index_map)


def _norm_matmul_kernel(x_ref, g_ref, w_ref, o_ref, h_ref):
    @pl.when(pl.program_id(1) == 0)
    def _():
        _norm_into(h_ref, x_ref, g_ref)

    o_ref[...] = jnp.dot(h_ref[...], _bf16(w_ref[...]),
                         preferred_element_type=F32).astype(o_ref.dtype)


def _norm_matmul(x, g, w, layer, *, bm, bn, out_dtype, name):
    m, k = x.shape
    n = w.shape[2]
    return pl.pallas_call(
        _norm_matmul_kernel,
        grid=(m // bm, n // bn),
        in_specs=[pl.BlockSpec((bm, k), lambda i, j: (i, 0)),
                  _layer_spec((1, k), lambda i, j: (layer, 0, 0)),
                  _layer_spec((k, bn), lambda i, j: (layer, 0, j))],
        out_specs=pl.BlockSpec((bm, bn), lambda i, j: (i, j)),
        out_shape=jax.ShapeDtypeStruct((m, n), out_dtype),
        scratch_shapes=[pltpu.VMEM((bm, k), BF16)],
        compiler_params=_cparams("parallel", "arbitrary"),
        name=name,
    )(x, g.reshape(DEPTH, 1, k), w)


def _q_up_kernel(c_ref, g_ref, wt_ref, cos_ref, sin_ref, o_ref, *, scale):
    h = _rms_rows(c_ref[...], g_ref[...]).astype(BF16)
    qt = lax.dot_general(wt_ref[...], h, NT_DIMS, preferred_element_type=F32)
    cos, sin = cos_ref[...], sin_ref[...]
    for hd in range(MLA_HEADS):
        r0 = hd * MLA_QK_PAD
        r1, r2, r3 = r0 + MLA_NOPE, r0 + MLA_NOPE + ROPE_HALF, r0 + MLA_NOPE + MLA_ROPE
        x1, x2 = qt[r1:r2], qt[r2:r3]
        o_ref[r0:r1, :] = (qt[r0:r1] * scale).astype(o_ref.dtype)
        o_ref[r1:r2, :] = (x1 * cos - x2 * sin).astype(o_ref.dtype)
        o_ref[r2:r3, :] = (x2 * cos + x1 * sin).astype(o_ref.dtype)
        o_ref[r3:r0 + MLA_QK_PAD, :] = qt[r3:r0 + MLA_QK_PAD].astype(o_ref.dtype)


def _q_up(proj, g_q, wt_q, cos_t, sin_t, layer, scale, *, bm):
    pos_blocks = SEQ // bm
    tab_spec = pl.BlockSpec((ROPE_HALF, bm), lambda i: (0, i % pos_blocks))
    n = MLA_HEADS * MLA_QK_PAD
    return pl.pallas_call(
        functools.partial(_q_up_kernel, scale=scale),
        grid=(TOKENS // bm,),
        in_specs=[pl.BlockSpec((bm, MLA_Q_LORA), lambda i: (i, 0)),
                  _layer_spec((1, MLA_Q_LORA), lambda i: (layer, 0, 0)),
                  _layer_spec((n, MLA_Q_LORA), lambda i: (layer, 0, 0)),
                  tab_spec, tab_spec],
        out_specs=pl.BlockSpec((n, bm), lambda i: (0, i)),
        out_shape=jax.ShapeDtypeStruct((n, TOKENS), BF16),
        compiler_params=_cparams("parallel"),
        name="mla_q_up",
    )(proj, g_q.reshape(DEPTH, 1, -1), wt_q, cos_t, sin_t)


def _rope_pad(x, cos_t, sin_a, sin_b):
    up = pltpu.roll(x, MLA_QK_PAD - ROPE_HALF, axis=1)
    dn = pltpu.roll(x, ROPE_HALF, axis=1)
    return x * cos_t + up * sin_a + dn * sin_b


def _kv_up_kernel(c_ref, kr_ref, g_ref, wk_ref, wvt_ref, cos_ref, sina_ref, sinb_ref,
                  k_ref, vt_ref):
    h = _rms_rows(c_ref[...], g_ref[...]).astype(BF16)
    kn = jnp.dot(h, wk_ref[...], preferred_element_type=F32)
    k_pe = _rope_pad(kr_ref[...], cos_ref[...], sina_ref[...], sinb_ref[...])
    for hd in range(MLA_HEADS):
        sl = slice(hd * MLA_QK_PAD, (hd + 1) * MLA_QK_PAD)
        k_ref[:, sl] = (kn[:, sl] + k_pe).astype(k_ref.dtype)
    vt_ref[...] = lax.dot_general(wvt_ref[...], h, NT_DIMS,
                                  preferred_element_type=F32).astype(vt_ref.dtype)


def _kv_up(proj, g_kv, w_k, wt_v, tabs, layer, *, bm):
    pos_blocks = SEQ // bm
    tab_spec = pl.BlockSpec((bm, MLA_QK_PAD), lambda i: (i % pos_blocks, 0))
    nk = MLA_HEADS * MLA_QK_PAD
    return pl.pallas_call(
        _kv_up_kernel,
        grid=(TOKENS // bm,),
        in_specs=[pl.BlockSpec((bm, MLA_KV_LORA), lambda i: (i, 2)),
                  pl.BlockSpec((bm, MLA_QK_PAD), lambda i: (i, 3)),
                  _layer_spec((1, MLA_KV_LORA), lambda i: (layer, 0, 0)),
                  _layer_spec((MLA_KV_LORA, nk), lambda i: (layer, 0, 0)),
                  _layer_spec((MLA_WIDTH, MLA_KV_LORA), lambda i: (layer, 0, 0)),
                  tab_spec, tab_spec, tab_spec],
        out_specs=[pl.BlockSpec((bm, nk), lambda i: (i, 0)),
                   pl.BlockSpec((MLA_WIDTH, bm), lambda i: (0, i))],
        out_shape=[jax.ShapeDtypeStruct((TOKENS, nk), BF16),
                   jax.ShapeDtypeStruct((MLA_WIDTH, TOKENS), BF16)],
        compiler_params=_cparams("parallel"),
        name="mla_kv_up",
    )(proj, proj, g_kv.reshape(DEPTH, 1, -1), w_k, wt_v, *tabs)


ATT_BQ = 256
ATT_BK = 256
ATT_NQ = SEQ // ATT_BQ


def _mla_attn_kernel(qt_ref, k_ref, vt_ref, o_ref, m_sc, l_sc, acc_sc):
    key = lax.broadcasted_iota(jnp.int32, (ATT_BK, ATT_BQ), 0)
    qry = lax.broadcasted_iota(jnp.int32, (ATT_BK, ATT_BQ), 1)
    causal = key <= qry
    tiles = [(i, j) for j in range(ATT_NQ) for i in range(j, ATT_NQ)]

    def scores(i, j):
        kj = k_ref[j * ATT_BK:(j + 1) * ATT_BK, :]
        qi = qt_ref[:, i * ATT_BQ:(i + 1) * ATT_BQ]
        return jnp.dot(kj, qi, preferred_element_type=F32)

    def absorb(i, j, s):
        vj = vt_ref[:, j * ATT_BK:(j + 1) * ATT_BK]
        if i == j:
            s = jnp.where(causal, s, -jnp.inf)
        m_blk = jnp.max(s, axis=0, keepdims=True)
        if j == 0:
            m_new = m_blk
            p = jnp.exp2(s - m_new)
            l = jnp.sum(p, axis=0, keepdims=True)
            acc = jnp.dot(vj, p.astype(BF16), preferred_element_type=F32)
        else:
            m_old = m_sc[i]
            m_new = jnp.maximum(m_old, m_blk)
            alpha = jnp.exp2(m_old - m_new)
            p = jnp.exp2(s - m_new)
            l = alpha * l_sc[i] + jnp.sum(p, axis=0, keepdims=True)
            acc = alpha * acc_sc[i] + jnp.dot(vj, p.astype(BF16),
                                              preferred_element_type=F32)
        if i == j:
            o_ref[i * ATT_BQ:(i + 1) * ATT_BQ, :] = (acc / l).T.astype(o_ref.dtype)
        else:
            m_sc[i] = m_new
            l_sc[i] = l
            acc_sc[i] = acc

    s_next = scores(*tiles[0])
    for t, (i, j) in enumerate(tiles):
        s_cur = s_next
        if t + 1 < len(tiles):
            s_next = scores(*tiles[t + 1])
        absorb(i, j, s_cur)


def _mla_attention(qt, k, vt):
    return pl.pallas_call(
        _mla_attn_kernel,
        grid=(BATCH, MLA_HEADS),
        in_specs=[pl.BlockSpec((MLA_QK_PAD, SEQ), lambda b, h: (h, b)),
                  pl.BlockSpec((SEQ, MLA_QK_PAD), lambda b, h: (b, h)),
                  pl.BlockSpec((MLA_V, SEQ), lambda b, h: (h, b))],
        out_specs=pl.BlockSpec((SEQ, MLA_V), lambda b, h: (b, h)),
        out_shape=jax.ShapeDtypeStruct((TOKENS, MLA_WIDTH), F32),
        scratch_shapes=[pltpu.VMEM((ATT_NQ, 1, ATT_BQ), F32),
                        pltpu.VMEM((ATT_NQ, 1, ATT_BQ), F32),
                        pltpu.VMEM((ATT_NQ, MLA_V, ATT_BQ), F32)],
        compiler_params=_cparams("parallel", "parallel"),
        name="mla_attention",
    )(qt, k, vt)


def _dil_attn_kernel(q_ref, k_ref, v_ref, o_ref, m_sc, l_sc, n_sc):
    row2 = lax.broadcasted_iota(jnp.int32, (BLOCK, 2 * BLOCK), 0)
    col2 = lax.broadcasted_iota(jnp.int32, (BLOCK, 2 * BLOCK), 1)
    dist = row2 + BLOCK - col2
    band = (dist >= 0) & (dist <= BLOCK)
    row1 = lax.broadcasted_iota(jnp.int32, (BLOCK, BLOCK), 0)
    col1 = lax.broadcasted_iota(jnp.int32, (BLOCK, BLOCK), 1)
    tri = row1 >= col1
    q_scale = DIL_HEAD_DIM ** -0.5 * math.log2(math.e)
    heads = LANES // DIL_HEAD_DIM
    n_patterns = len(DIL_PATTERNS)

    def rows_at(start, dil):
        return pl.ds(start, BLOCK) if dil == 1 else pl.ds(start, BLOCK, stride=dil)

    blocks = [(pi, dil, r, n) for pi, (_, dil) in enumerate(DIL_PATTERNS)
              for r in range(dil) for n in range(SEQ // dil // BLOCK)]

    def load(pi, dil, r, n):
        rows = rows_at(r + dil * BLOCK * n, dil)
        q = (q_ref[rows, :] * q_scale).astype(BF16)
        if n == 0:
            return rows, q, k_ref[rows, :].astype(BF16), v_ref[rows, :].astype(BF16), tri
        prev = rows_at(r + dil * BLOCK * (n - 1), dil)
        kk = jnp.concatenate([k_ref[prev, :], k_ref[rows, :]], axis=0).astype(BF16)
        vv = jnp.concatenate([v_ref[prev, :], v_ref[rows, :]], axis=0).astype(BF16)
        return rows, q, kk, vv, band

    def scores(blk, hd):
        _, q, kk, _, _ = blk
        sl = slice(hd * DIL_HEAD_DIM, (hd + 1) * DIL_HEAD_DIM)
        return lax.dot_general(q[:, sl], kk[:, sl], NT_DIMS, preferred_element_type=F32)

    def softmax_pv(blk, hd, s):
        _, _, _, vv, mask = blk
        sl = slice(hd * DIL_HEAD_DIM, (hd + 1) * DIL_HEAD_DIM)
        s = jnp.where(mask, s, -jnp.inf)
        m = jnp.max(s, axis=-1, keepdims=True)
        p = jnp.exp2(s - m)
        l = jnp.sum(p, axis=-1, keepdims=True)
        acc = jnp.dot(p.astype(BF16), vv[:, sl], preferred_element_type=F32)
        return (jnp.broadcast_to(m, (BLOCK, DIL_HEAD_DIM)),
                jnp.broadcast_to(l, (BLOCK, DIL_HEAD_DIM)), acc)

    def merge(pi, rows, parts):
        m2, l2, a2 = (jnp.concatenate([p[c] for p in parts], axis=-1) for c in range(3))
        if pi > 0:
            m_old = m_sc[rows, :]
            m_new = jnp.maximum(m_old, m2)
            w_old, w_new = jnp.exp2(m_old - m_new), jnp.exp2(m2 - m_new)
            l2 = w_old * l_sc[rows, :] + w_new * l2
            a2 = w_old * n_sc[rows, :] + w_new * a2
            m2 = m_new
        if pi == n_patterns - 1:
            o_ref[rows, :] = a2 / l2
        else:
            m_sc[rows, :] = m2
            l_sc[rows, :] = l2
            n_sc[rows, :] = a2

    items = [(b, hd) for b in range(len(blocks)) for hd in range(heads)]
    blk_next = load(*blocks[0])
    s_next = scores(blk_next, 0)
    parts = []
    for t, (b, hd) in enumerate(items):
        blk, s_cur = blk_next, s_next
        if t + 1 < len(items):
            nb_, nhd = items[t + 1]
            if nb_ != b:
                blk_next = load(*blocks[nb_])
            s_next = scores(blk_next, nhd)
        parts.append(softmax_pv(blk, hd, s_cur))
        if hd == heads - 1:
            merge(blocks[b][0], blk[0], parts)
            parts = []


def _dil_attention(proj):
    def spec(col0):
        return pl.BlockSpec((SEQ, LANES), lambda b, hp: (b, col0 // LANES + hp))

    return pl.pallas_call(
        _dil_attn_kernel,
        grid=(BATCH, DIL_WIDTH // LANES),
        in_specs=[spec(DIL_Q_COL), spec(DIL_K_COL), spec(DIL_V_COL)],
        out_specs=pl.BlockSpec((SEQ, LANES), lambda b, hp: (b, hp)),
        out_shape=jax.ShapeDtypeStruct((TOKENS, DIL_WIDTH), F32),
        scratch_shapes=[pltpu.VMEM((SEQ, LANES), F32)] * 3,
        compiler_params=_cparams("parallel", "parallel"),
        name="dilated_attention",
    )(proj, proj, proj)


SSM_ROWS = TOKENS // SSM_CHUNK
SSM_NCHUNK = SEQ // SSM_CHUNK
HI = lax.Precision.HIGHEST


def _ssm_kernel(u_ref, t0_ref, m1re_ref, m1im_ref, m2re_ref, m2im_ref,
                are_ref, aim_ref, d_ref, y_ref, hre_ref, him_ref):
    u = u_ref[0]
    hre_ref[...] = jnp.dot(u, m1re_ref[0], precision=HI, preferred_element_type=F32)
    him_ref[...] = jnp.dot(u, m1im_ref[0], precision=HI, preferred_element_type=F32)
    a_re = jnp.broadcast_to(are_ref[0], (BATCH, SSM_STATE))
    a_im = jnp.broadcast_to(aim_ref[0], (BATCH, SSM_STATE))
    s_re = jnp.zeros((BATCH, SSM_STATE), F32)
    s_im = jnp.zeros((BATCH, SSM_STATE), F32)
    for c in range(SSM_NCHUNK):
        rows = slice(c * BATCH, (c + 1) * BATCH)
        loc_re, loc_im = hre_ref[rows, :], him_ref[rows, :]
        hre_ref[rows, :] = s_re
        him_ref[rows, :] = s_im
        s_re, s_im = (a_re * s_re - a_im * s_im + loc_re,
                      a_re * s_im + a_im * s_re + loc_im)
    y = (jnp.dot(u, t0_ref[0], precision=HI, preferred_element_type=F32)
         + jnp.dot(hre_ref[...], m2re_ref[0], precision=HI, preferred_element_type=F32)
         + jnp.dot(him_ref[...], m2im_ref[0], precision=HI, preferred_element_type=F32)
         + u * d_ref[0])
    y_ref[0] = jax.nn.gelu(y, approximate=True)


def _ssm_scan(u_rows, mats):
    t0, m1re, m1im, m2re, m2im, a_re, a_im, dvec = mats

    def spec(r, c):
        return pl.BlockSpec((1, r, c), lambda g: (g, 0, 0))

    return pl.pallas_call(
        _ssm_kernel,
        grid=(SSM_GROUPS,),
        in_specs=[spec(SSM_ROWS, SSM_ROW), spec(SSM_ROW, SSM_ROW),
                  spec(SSM_ROW, SSM_STATE), spec(SSM_ROW, SSM_STATE),
                  spec(SSM_STATE, SSM_ROW), spec(SSM_STATE, SSM_ROW),
                  spec(1, SSM_STATE), spec(1, SSM_STATE), spec(1, SSM_ROW)],
        out_specs=spec(SSM_ROWS, SSM_ROW),
        out_shape=jax.ShapeDtypeStruct((SSM_GROUPS, SSM_ROWS, SSM_ROW), F32),
        scratch_shapes=[pltpu.VMEM((SSM_ROWS, SSM_STATE), F32),
                        pltpu.VMEM((SSM_ROWS, SSM_STATE), F32)],
        compiler_params=_cparams("parallel"),
        name="ssm_chunk_scan",
    )(u_rows, t0, m1re, m1im, m2re, m2im, a_re, a_im, dvec)


def _ssm_matrices(a_re, a_im, b_re, b_im, c_re, c_im, d_skip, log_dt):
    lam_re = jnp.minimum(a_re, -1e-4)
    lam_im = a_im
    dt = jnp.exp(log_dt)[:, None]
    mag = jnp.exp(lam_re * dt)
    ab_re, ab_im = mag * jnp.cos(lam_im * dt), mag * jnp.sin(lam_im * dt)
    n_re, n_im = ab_re - 1.0, ab_im
    den = lam_re * lam_re + lam_im * lam_im
    f_re = (n_re * lam_re + n_im * lam_im) / den
    f_im = (n_im * lam_re - n_re * lam_im) / den
    bb_re = f_re[..., None] * b_re - f_im[..., None] * b_im
    bb_im = f_re[..., None] * b_im + f_im[..., None] * b_re
    p_re, p_im = [jnp.ones_like(ab_re)], [jnp.zeros_like(ab_im)]
    for _ in range(SSM_CHUNK):
        p_re.append(p_re[-1] * ab_re - p_im[-1] * ab_im)
        p_im.append(p_re[-2] * ab_im + p_im[-1] * ab_re)
    pw_re, pw_im = jnp.stack(p_re), jnp.stack(p_im)

    cb_re = (c_re[:, None, :, :] * bb_re.transpose(0, 2, 1)[:, :, None, :]
             - c_im[:, None, :, :] * bb_im.transpose(0, 2, 1)[:, :, None, :])
    cb_im = (c_re[:, None, :, :] * bb_im.transpose(0, 2, 1)[:, :, None, :]
             + c_im[:, None, :, :] * bb_re.transpose(0, 2, 1)[:, :, None, :])
    lag_re = pw_re[:SSM_CHUNK].transpose(1, 0, 2)[:, :, None, None, :]
    lag_im = pw_im[:SSM_CHUNK].transpose(1, 0, 2)[:, :, None, None, :]
    kern = jnp.sum(lag_re * cb_re[:, None] - lag_im * cb_im[:, None], axis=-1)
    kern = jnp.concatenate([kern, jnp.zeros_like(kern[:, :1])], axis=1)
    t_idx = np.arange(SSM_CHUNK)
    lag = t_idx[None, :] - t_idx[:, None]
    lag = np.where(lag >= 0, lag, SSM_CHUNK)
    t0 = kern[:, lag]
    t0 = t0.transpose(0, 1, 3, 2, 4).reshape(SSM_GROUPS, SSM_ROW, SSM_ROW)

    back = pw_re[SSM_CHUNK - 1 - t_idx], pw_im[SSM_CHUNK - 1 - t_idx]
    bre_t, bim_t = bb_re.transpose(0, 2, 1), bb_im.transpose(0, 2, 1)
    m1re = (back[0].transpose(1, 0, 2)[:, :, None, :] * bre_t[:, None]
            - back[1].transpose(1, 0, 2)[:, :, None, :] * bim_t[:, None])
    m1im = (back[0].transpose(1, 0, 2)[:, :, None, :] * bim_t[:, None]
            + back[1].transpose(1, 0, 2)[:, :, None, :] * bre_t[:, None])
    m1re = m1re.reshape(SSM_GROUPS, SSM_ROW, SSM_STATE)
    m1im = m1im.reshape(SSM_GROUPS, SSM_ROW, SSM_STATE)

    fwd_re = pw_re[1:].transpose(1, 0, 2)[:, :, None, :]
    fwd_im = pw_im[1:].transpose(1, 0, 2)[:, :, None, :]
    ca_re = c_re[:, None] * fwd_re - c_im[:, None] * fwd_im
    ca_im = c_re[:, None] * fwd_im + c_im[:, None] * fwd_re
    m2re = ca_re.transpose(0, 3, 1, 2).reshape(SSM_GROUPS, SSM_STATE, SSM_ROW)
    m2im = (-ca_im).transpose(0, 3, 1, 2).reshape(SSM_GROUPS, SSM_STATE, SSM_ROW)

    a16_re = pw_re[SSM_CHUNK].reshape(SSM_GROUPS, 1, SSM_STATE)
    a16_im = pw_im[SSM_CHUNK].reshape(SSM_GROUPS, 1, SSM_STATE)
    dvec = jnp.tile(d_skip[:, None, :], (1, SSM_CHUNK, 1)).reshape(SSM_GROUPS, 1, SSM_ROW)
    return t0, m1re, m1im, m2re, m2im, a16_re, a16_im, dvec


def _glu_kernel(y_ref, w_ref, b_ref, o_ref):
    z = jnp.dot(y_ref[...].astype(BF16), _bf16(w_ref[...]),
                preferred_element_type=F32) + b_ref[...]
    o_ref[...] = z[:, :SSM_WIDTH] * jax.nn.sigmoid(z[:, SSM_WIDTH:])


def _glu(y, w_glu, b_glu, layer, *, bm):
    return pl.pallas_call(
        _glu_kernel,
        grid=(TOKENS // bm,),
        in_specs=[pl.BlockSpec((bm, SSM_WIDTH), lambda i: (i, 0)),
                  _layer_spec((SSM_WIDTH, 2 * SSM_WIDTH), lambda i: (layer, 0, 0)),
                  _layer_spec((1, 2 * SSM_WIDTH), lambda i: (layer, 0, 0))],
        out_specs=pl.BlockSpec((bm, SSM_WIDTH), lambda i: (i, 0)),
        out_shape=jax.ShapeDtypeStruct((TOKENS, SSM_WIDTH), F32),
        compiler_params=_cparams("parallel"),
        name="ssm_glu",
    )(y, w_glu, b_glu.reshape(DEPTH, 1, -1))


def _out_proj_kernel(ya_ref, yb_ref, yc_ref, ga_ref, gb_ref, gc_ref, w_ref, x_ref,
                     o_ref, h_ref):
    @pl.when(pl.program_id(1) == 0)
    def _():
        _norm_into(h_ref, ya_ref, ga_ref, 0)
        _norm_into(h_ref, yb_ref, gb_ref, MLA_WIDTH)
        _norm_into(h_ref, yc_ref, gc_ref, MLA_WIDTH + SSM_WIDTH)

    o_ref[...] = x_ref[...] + jnp.dot(h_ref[...], _bf16(w_ref[...]),
                                      preferred_element_type=F32)


def _out_proj(y_mla, y_ssm, y_dil, g_mla, g_ssm, g_dil, w_o, x, layer, *, bm, bn):
    def rows(width):
        return pl.BlockSpec((bm, width), lambda i, j: (i, 0))

    def gain(width):
        return _layer_spec((1, width), lambda i, j: (layer, 0, 0))

    return pl.pallas_call(
        _out_proj_kernel,
        grid=(TOKENS // bm, D_MODEL // bn),
        in_specs=[rows(MLA_WIDTH), rows(SSM_WIDTH), rows(DIL_WIDTH),
                  gain(MLA_WIDTH), gain(SSM_WIDTH), gain(DIL_WIDTH),
                  _layer_spec((D_MODEL, bn), lambda i, j: (layer, 0, j)),
                  pl.BlockSpec((bm, bn), lambda i, j: (i, j))],
        out_specs=pl.BlockSpec((bm, bn), lambda i, j: (i, j)),
        out_shape=jax.ShapeDtypeStruct((TOKENS, D_MODEL), F32),
        scratch_shapes=[pltpu.VMEM((bm, D_MODEL), BF16)],
        compiler_params=_cparams("parallel", "arbitrary"),
        name="out_proj",
    )(y_mla, y_ssm, y_dil, g_mla.reshape(DEPTH, 1, -1), g_ssm.reshape(DEPTH, 1, -1),
      g_dil.reshape(DEPTH, 1, -1), w_o, x)


def _ffn_up_kernel(x_ref, g_ref, wg_ref, wu_ref, o_ref, h_ref):
    @pl.when(pl.program_id(1) == 0)
    def _():
        _norm_into(h_ref, x_ref, g_ref)

    h = h_ref[...]
    gate = jnp.dot(h, _bf16(wg_ref[...]), preferred_element_type=F32)
    up = jnp.dot(h, _bf16(wu_ref[...]), preferred_element_type=F32)
    o_ref[...] = (jax.nn.silu(gate) * up).astype(o_ref.dtype)


def _ffn_up(x, g, w_gate, w_up, layer, *, bm, bn):
    return pl.pallas_call(
        _ffn_up_kernel,
        grid=(TOKENS // bm, D_FF // bn),
        in_specs=[pl.BlockSpec((bm, D_MODEL), lambda i, j: (i, 0)),
                  _layer_spec((1, D_MODEL), lambda i, j: (layer, 0, 0)),
                  _layer_spec((D_MODEL, bn), lambda i, j: (layer, 0, j)),
                  _layer_spec((D_MODEL, bn), lambda i, j: (layer, 0, j))],
        out_specs=pl.BlockSpec((bm, bn), lambda i, j: (i, j)),
        out_shape=jax.ShapeDtypeStruct((TOKENS, D_FF), BF16),
        scratch_shapes=[pltpu.VMEM((bm, D_MODEL), BF16)],
        compiler_params=_cparams("parallel", "arbitrary"),
        name="ffn_gate_up",
    )(x, g.reshape(DEPTH, 1, -1), w_gate, w_up)


def _ffn_down_kernel(a_ref, w_ref, x_ref, o_ref):
    o_ref[...] = x_ref[...] + jnp.dot(a_ref[...], _bf16(w_ref[...]),
                                      preferred_element_type=F32)


def _ffn_down(a, w_down, x, layer, *, bm, bn):
    return pl.pallas_call(
        _ffn_down_kernel,
        grid=(TOKENS // bm, D_MODEL // bn),
        in_specs=[pl.BlockSpec((bm, D_FF), lambda i, j: (i, 0)),
                  _layer_spec((D_FF, bn), lambda i, j: (layer, 0, j)),
                  pl.BlockSpec((bm, bn), lambda i, j: (i, j))],
        out_specs=pl.BlockSpec((bm, bn), lambda i, j: (i, j)),
        out_shape=jax.ShapeDtypeStruct((TOKENS, D_MODEL), F32),
        compiler_params=_cparams("parallel", "arbitrary"),
        name="ffn_down",
    )(a, w_down, x)


def _final_norm_kernel(x_ref, g_ref, o_ref):
    o_ref[...] = _rms_rows(x_ref[...], g_ref[...])


def _final_norm(x, g, *, bm):
    return pl.pallas_call(
        _final_norm_kernel,
        grid=(TOKENS // bm,),
        in_specs=[pl.BlockSpec((bm, D_MODEL), lambda i: (i, 0)),
                  pl.BlockSpec((1, D_MODEL), lambda i: (0, 0))],
        out_specs=pl.BlockSpec((bm, D_MODEL), lambda i: (i, 0)),
        out_shape=jax.ShapeDtypeStruct((TOKENS, D_MODEL), F32),
        compiler_params=_cparams("parallel"),
        name="final_norm",
    )(x, g.reshape(1, -1))


def _pad_w_in(w_in):
    lyr, d, _ = w_in.shape
    z = lambda n: jnp.zeros((lyr, d, n), BF16)
    w = w_in.astype(BF16)
    return jnp.concatenate([w[..., :768], z(128), w[..., 768:832], z(64), w[..., 832:]],
                           axis=-1)


def _pad_wt_uq(w_uq):
    lyr = w_uq.shape[0]
    w = w_uq.reshape(lyr, MLA_Q_LORA, MLA_HEADS, MLA_NOPE + MLA_ROPE)
    w = jnp.pad(w, ((0, 0), (0, 0), (0, 0), (0, MLA_QK_PAD - MLA_NOPE - MLA_ROPE)))
    w = w.reshape(lyr, MLA_Q_LORA, MLA_HEADS * MLA_QK_PAD)
    return w.transpose(0, 2, 1).astype(BF16)


def _split_w_ukv(w_ukv):
    lyr = w_ukv.shape[0]
    w = w_ukv.reshape(lyr, MLA_KV_LORA, MLA_HEADS, MLA_NOPE + MLA_V)
    wk = jnp.pad(w[..., :MLA_NOPE], ((0, 0), (0, 0), (0, 0), (0, MLA_QK_PAD - MLA_NOPE)))
    wk = wk.reshape(lyr, MLA_KV_LORA, -1).astype(BF16)
    wv = w[..., MLA_NOPE:].reshape(lyr, MLA_KV_LORA, -1)
    return wk, wv.transpose(0, 2, 1).astype(BF16)


def _rope_angles():
    inv_freq = ROPE_THETA ** (-jnp.arange(ROPE_HALF, dtype=F32) / ROPE_HALF)
    ang = jnp.arange(SEQ, dtype=F32)[:, None] * inv_freq[None, :]
    return jnp.cos(ang), jnp.sin(ang)


def _rope_tables_k():
    cos, sin = _rope_angles()
    one = jnp.ones((SEQ, MLA_NOPE), F32)
    z = lambda n: jnp.zeros((SEQ, n), F32)
    tail = MLA_QK_PAD - MLA_NOPE - MLA_ROPE
    cos_t = jnp.concatenate([one, cos, cos, z(tail)], axis=1)
    sin_a = jnp.concatenate([z(MLA_NOPE), -sin, z(ROPE_HALF), z(tail)], axis=1)
    sin_b = jnp.concatenate([z(MLA_NOPE), z(ROPE_HALF), sin, z(tail)], axis=1)
    return cos_t, sin_a, sin_b


def _to_ssm_rows(u):
    u = u.reshape(BATCH, SSM_NCHUNK, SSM_CHUNK, SSM_GROUPS, SSM_GROUP)
    return u.transpose(3, 1, 0, 2, 4).reshape(SSM_GROUPS, SSM_ROWS, SSM_ROW)


def _from_ssm_rows(y):
    y = y.reshape(SSM_GROUPS, SSM_NCHUNK, BATCH, SSM_CHUNK, SSM_GROUP)
    return y.transpose(2, 1, 3, 0, 4).reshape(TOKENS, SSM_WIDTH)


def kernel(x, g_mix, w_in, g_q, w_uq, g_kv, w_ukv, a_re, a_im, b_re, b_im, c_re, c_im,
           d_skip, log_dt, w_glu, b_glu, g_out_mla, g_out_ssm, g_out_dil, w_o,
           g_ffn, w_gate, w_up, w_down, g_final):
    x = x.reshape(TOKENS, D_MODEL)
    w_in_p = _pad_w_in(w_in)
    wt_q = _pad_wt_uq(w_uq)
    w_k, wt_v = _split_w_ukv(w_ukv)
    q_scale = (MLA_NOPE + MLA_ROPE) ** -0.5 * math.log2(math.e)
    cos, sin = _rope_angles()
    q_cos_t, q_sin_t = (cos * q_scale).T, (sin * q_scale).T
    k_tabs = _rope_tables_k()

    for l in range(DEPTH):
        proj = _norm_matmul(x, g_mix, w_in_p, l, bm=1024, bn=512, out_dtype=F32,
                            name="in_proj")
        qt = _q_up(proj, g_q, wt_q, q_cos_t, q_sin_t, l, q_scale, bm=512)
        k, vt = _kv_up(proj, g_kv, w_k, wt_v, k_tabs, l, bm=512)
        y_mla = _mla_attention(qt, k, vt)
        mats = _ssm_matrices(a_re[l], a_im[l], b_re[l], b_im[l], c_re[l], c_im[l],
                             d_skip[l], log_dt[l])
        y = _ssm_scan(_to_ssm_rows(proj[:, SSM_U_COL:SSM_U_COL + SSM_WIDTH]), mats)
        y_ssm = _glu(_from_ssm_rows(y), w_glu, b_glu, l, bm=1024)
        y_dil = _dil_attention(proj)
        x = _out_proj(y_mla, y_ssm, y_dil, g_out_mla, g_out_ssm, g_out_dil,
                      w_o, x, l, bm=1024, bn=512)
        act = _ffn_up(x, g_ffn, w_gate, w_up, l, bm=1024, bn=512)
        x = _ffn_down(act, w_down, x, l, bm=1024, bn=256)
    out = _final_norm(x, g_final, bm=512)
    return out.reshape(BATCH, SEQ, D_MODEL)
```

```python
import functools
import math

import jax
import jax.numpy as jnp
import numpy as np
from jax import lax
from jax.experimental import pallas as pl
from jax.experimental.pallas import tpu as pltpu

F32 = jnp.float32
BF16 = jnp.bfloat16

D_MODEL = 2048
BATCH = 4
SEQ = 2048
DEPTH = 4
TOKENS = BATCH * SEQ

MLA_HEADS = 8
MLA_NOPE = 128
MLA_ROPE = 64
MLA_V = 128
MLA_Q_LORA = 512
MLA_KV_LORA = 256
MLA_WIDTH = MLA_HEADS * MLA_V
MLA_QK_PAD = 256
ROPE_THETA = 10000.0
ROPE_HALF = MLA_ROPE // 2

SSM_WIDTH = 512
SSM_GROUP = 16
SSM_GROUPS = 32
SSM_STATE = 64
SSM_CHUNK = 8

DIL_WIDTH = 512
DIL_HEAD_DIM = 64
DIL_HEADS = 8
DIL_PATTERNS = ((128, 1), (512, 4), (2048, 16))
BLOCK = 128

IN_PAD = 3072
DIL_Q_COL = 1536
DIL_K_COL = 2048
DIL_V_COL = 2560
SSM_U_COL = 1024
D_FF = 5632
NORM_EPS = 1e-6

LANES = 128
VMEM_LIMIT_BYTES = 56 * 1024 * 1024

NT_DIMS = (((1,), (1,)), ((), ()))


def _cparams(*semantics):
    return pltpu.CompilerParams(dimension_semantics=semantics,
                                vmem_limit_bytes=VMEM_LIMIT_BYTES)


def _rms_rows(x, g):
    ms = jnp.mean(x * x, axis=-1, keepdims=True)
    return x * lax.rsqrt(ms + NORM_EPS) * g


def _norm_into(h_ref, x_ref, g_ref, col0=0, chunk=256):
    rows, width = x_ref.shape
    g = g_ref[...]
    for r in range(0, rows, chunk):
        x = x_ref[r:r + chunk, :]
        h_ref[r:r + chunk, col0:col0 + width] = _rms_rows(x, g).astype(BF16)


def _bf16(w):
    return w if w.dtype == BF16 else w.astype(BF16)


def _layer_spec(shape, index_map):
    return pl.BlockSpec((None,) + tuple(shape), index_map)


def _norm_matmul_kernel(x_ref, g_ref, w_ref, o_ref, h_ref):
    @pl.when(pl.program_id(1) == 0)
    def _():
        _norm_into(h_ref, x_ref, g_ref)

    o_ref[...] = jnp.dot(h_ref[...], _bf16(w_ref[...]),
                         preferred_element_type=F32).astype(o_ref.dtype)


def _norm_matmul(x, g, w, layer, *, bm, bn, out_dtype, name):
    m, k = x.shape
    n = w.shape[2]
    return pl.pallas_call(
        _norm_matmul_kernel,
        grid=(m // bm, n // bn),
        in_specs=[pl.BlockSpec((bm, k), lambda i, j: (i, 0)),
                  _layer_spec((1, k), lambda i, j: (layer, 0, 0)),
                  _layer_spec((k, bn), lambda i, j: (layer, 0, j))],
        out_specs=pl.BlockSpec((bm, bn), lambda i, j: (i, j)),
        out_shape=jax.ShapeDtypeStruct((m, n), out_dtype),
        scratch_shapes=[pltpu.VMEM((bm, k), BF16)],
        compiler_params=_cparams("parallel", "arbitrary"),
        name=name,
    )(x, g.reshape(DEPTH, 1, k), w)


def _q_up_kernel(c_ref, g_ref, wt_ref, cos_ref, sin_ref, o_ref, *, scale):
    h = _rms_rows(c_ref[...], g_ref[...]).astype(BF16)
    qt = lax.dot_general(wt_ref[...], h, NT_DIMS, preferred_element_type=F32)
    cos, sin = cos_ref[...], sin_ref[...]
    for hd in range(MLA_HEADS):
        r0 = hd * MLA_QK_PAD
        r1, r2, r3 = r0 + MLA_NOPE, r0 + MLA_NOPE + ROPE_HALF, r0 + MLA_NOPE + MLA_ROPE
        x1, x2 = qt[r1:r2], qt[r2:r3]
        o_ref[r0:r1, :] = (qt[r0:r1] * scale).astype(o_ref.dtype)
        o_ref[r1:r2, :] = (x1 * cos - x2 * sin).astype(o_ref.dtype)
        o_ref[r2:r3, :] = (x2 * cos + x1 * sin).astype(o_ref.dtype)
        o_ref[r3:r0 + MLA_QK_PAD, :] = qt[r3:r0 + MLA_QK_PAD].astype(o_ref.dtype)


def _q_up(proj, g_q, wt_q, cos_t, sin_t, layer, scale, *, bm):
    pos_blocks = SEQ // bm
    tab_spec = pl.BlockSpec((ROPE_HALF, bm), lambda i: (0, i % pos_blocks))
    n = MLA_HEADS * MLA_QK_PAD
    return pl.pallas_call(
        functools.partial(_q_up_kernel, scale=scale),
        grid=(TOKENS // bm,),
        in_specs=[pl.BlockSpec((bm, MLA_Q_LORA), lambda i: (i, 0)),
                  _layer_spec((1, MLA_Q_LORA), lambda i: (layer, 0, 0)),
                  _layer_spec((n, MLA_Q_LORA), lambda i: (layer, 0, 0)),
                  tab_spec, tab_spec],
        out_specs=pl.BlockSpec((n, bm), lambda i: (0, i)),
        out_shape=jax.ShapeDtypeStruct((n, TOKENS), BF16),
        compiler_params=_cparams("parallel"),
        name="mla_q_up",
    )(proj, g_q.reshape(DEPTH, 1, -1), wt_q, cos_t, sin_t)


def _rope_pad(x, cos_t, sin_a, sin_b):
    up = pltpu.roll(x, MLA_QK_PAD - ROPE_HALF, axis=1)
    dn = pltpu.roll(x, ROPE_HALF, axis=1)
    return x * cos_t + up * sin_a + dn * sin_b


def _kv_up_kernel(c_ref, kr_ref, g_ref, wk_ref, wvt_ref, cos_ref, sina_ref, sinb_ref,
                  k_ref, vt_ref):
    h = _rms_rows(c_ref[...], g_ref[...]).astype(BF16)
    kn = jnp.dot(h, wk_ref[...], preferred_element_type=F32)
    k_pe = _rope_pad(kr_ref[...], cos_ref[...], sina_ref[...], sinb_ref[...])
    for hd in range(MLA_HEADS):
        sl = slice(hd * MLA_QK_PAD, (hd + 1) * MLA_QK_PAD)
        k_ref[:, sl] = (kn[:, sl] + k_pe).astype(k_ref.dtype)
    vt_ref[...] = lax.dot_general(wvt_ref[...], h, NT_DIMS,
                                  preferred_element_type=F32).astype(vt_ref.dtype)


def _kv_up(proj, g_kv, w_k, wt_v, tabs, layer, *, bm):
    pos_blocks = SEQ // bm
    tab_spec = pl.BlockSpec((bm, MLA_QK_PAD), lambda i: (i % pos_blocks, 0))
    nk = MLA_HEADS * MLA_QK_PAD
    return pl.pallas_call(
        _kv_up_kernel,
        grid=(TOKENS // bm,),
        in_specs=[pl.BlockSpec((bm, MLA_KV_LORA), lambda i: (i, 2)),
                  pl.BlockSpec((bm, MLA_QK_PAD), lambda i: (i, 3)),
                  _layer_spec((1, MLA_KV_LORA), lambda i: (layer, 0, 0)),
                  _layer_spec((MLA_KV_LORA, nk), lambda i: (layer, 0, 0)),
                  _layer_spec((MLA_WIDTH, MLA_KV_LORA), lambda i: (layer, 0, 0)),
                  tab_spec, tab_spec, tab_spec],
        out_specs=[pl.BlockSpec((bm, nk), lambda i: (i, 0)),
                   pl.BlockSpec((MLA_WIDTH, bm), lambda i: (0, i))],
        out_shape=[jax.ShapeDtypeStruct((TOKENS, nk), BF16),
                   jax.ShapeDtypeStruct((MLA_WIDTH, TOKENS), BF16)],
        compiler_params=_cparams("parallel"),
        name="mla_kv_up",
    )(proj, proj, g_kv.reshape(DEPTH, 1, -1), w_k, wt_v, *tabs)


ATT_BQ = 256
ATT_BK = 256
ATT_NQ = SEQ // ATT_BQ


def _mla_attn_kernel(qt_ref, k_ref, vt_ref, o_ref, m_sc, l_sc, acc_sc):
    key = lax.broadcasted_iota(jnp.int32, (ATT_BK, ATT_BQ), 0)
    qry = lax.broadcasted_iota(jnp.int32, (ATT_BK, ATT_BQ), 1)
    causal = key <= qry
    tiles = [(i, j) for j in range(ATT_NQ) for i in range(j, ATT_NQ)]

    def scores(i, j):
        kj = k_ref[j * ATT_BK:(j + 1) * ATT_BK, :]
        qi = qt_ref[:, i * ATT_BQ:(i + 1) * ATT_BQ]
        return jnp.dot(kj, qi, preferred_element_type=F32)

    def absorb(i, j, s):
        vj = vt_ref[:, j * ATT_BK:(j + 1) * ATT_BK]
        if i == j:
            s = jnp.where(causal, s, -jnp.inf)
        m_blk = jnp.max(s, axis=0, keepdims=True)
        if j == 0:
            m_new = m_blk
            p = jnp.exp2(s - m_new)
            l = jnp.sum(p, axis=0, keepdims=True)
            acc = jnp.dot(vj, p.astype(BF16), preferred_element_type=F32)
        else:
            m_old = m_sc[i]
            m_new = jnp.maximum(m_old, m_blk)
            alpha = jnp.exp2(m_old - m_new)
            p = jnp.exp2(s - m_new)
            l = alpha * l_sc[i] + jnp.sum(p, axis=0, keepdims=True)
            acc = alpha * acc_sc[i] + jnp.dot(vj, p.astype(BF16),
                                              preferred_element_type=F32)
        if i == j:
            o_ref[i * ATT_BQ:(i + 1) * ATT_BQ, :] = (acc / l).T.astype(o_ref.dtype)
        else:
            m_sc[i] = m_new
            l_sc[i] = l
            acc_sc[i] = acc

    s_next = scores(*tiles[0])
    for t, (i, j) in enumerate(tiles):
        s_cur = s_next
        if t + 1 < len(tiles):
            s_next = scores(*tiles[t + 1])
        absorb(i, j, s_cur)


def _mla_attention(qt, k, vt):
    return pl.pallas_call(
        _mla_attn_kernel,
        grid=(BATCH, MLA_HEADS),
        in_specs=[pl.BlockSpec((MLA_QK_PAD, SEQ), lambda b, h: (h, b)),
                  pl.BlockSpec((SEQ, MLA_QK_PAD), lambda b, h: (b, h)),
                  pl.BlockSpec((MLA_V, SEQ), lambda b, h: (h, b))],
        out_specs=pl.BlockSpec((SEQ, MLA_V), lambda b, h: (b, h)),
        out_shape=jax.ShapeDtypeStruct((TOKENS, MLA_WIDTH), F32),
        scratch_shapes=[pltpu.VMEM((ATT_NQ, 1, ATT_BQ), F32),
                        pltpu.VMEM((ATT_NQ, 1, ATT_BQ), F32),
                        pltpu.VMEM((ATT_NQ, MLA_V, ATT_BQ), F32)],
        compiler_params=_cparams("parallel", "parallel"),
        name="mla_attention",
    )(qt, k, vt)


def _dil_attn_kernel(q_ref, k_ref, v_ref, o_ref, m_sc, l_sc, n_sc):
    row2 = lax.broadcasted_iota(jnp.int32, (BLOCK, 2 * BLOCK), 0)
    col2 = lax.broadcasted_iota(jnp.int32, (BLOCK, 2 * BLOCK), 1)
    dist = row2 + BLOCK - col2
    band = (dist >= 0) & (dist <= BLOCK)
    row1 = lax.broadcasted_iota(jnp.int32, (BLOCK, BLOCK), 0)
    col1 = lax.broadcasted_iota(jnp.int32, (BLOCK, BLOCK), 1)
    tri = row1 >= col1
    q_scale = DIL_HEAD_DIM ** -0.5 * math.log2(math.e)
    heads = LANES // DIL_HEAD_DIM
    n_patterns = len(DIL_PATTERNS)

    def rows_at(start, dil):
        return pl.ds(start, BLOCK) if dil == 1 else pl.ds(start, BLOCK, stride=dil)

    blocks = [(pi, dil, r, n) for pi, (_, dil) in enumerate(reversed(DIL_PATTERNS))
              for r in range(dil) for n in range(SEQ // dil // BLOCK)]

    def load(pi, dil, r, n):
        rows = rows_at(r + dil * BLOCK * n, dil)
        q = (q_ref[rows, :] * q_scale).astype(BF16)
        if n == 0:
            return rows, q, k_ref[rows, :].astype(BF16), v_ref[rows, :].astype(BF16), tri
        prev = rows_at(r + dil * BLOCK * (n - 1), dil)
        kk = jnp.concatenate([k_ref[prev, :], k_ref[rows, :]], axis=0).astype(BF16)
        vv = jnp.concatenate([v_ref[prev, :], v_ref[rows, :]], axis=0).astype(BF16)
        return rows, q, kk, vv, band

    def scores(blk, hd):
        _, q, kk, _, _ = blk
        sl = slice(hd * DIL_HEAD_DIM, (hd + 1) * DIL_HEAD_DIM)
        return lax.dot_general(q[:, sl], kk[:, sl], NT_DIMS, preferred_element_type=F32)

    def softmax_pv(blk, hd, s):
        _, _, _, vv, mask = blk
        sl = slice(hd * DIL_HEAD_DIM, (hd + 1) * DIL_HEAD_DIM)
        s = jnp.where(mask, s, -jnp.inf)
        m = jnp.max(s, axis=-1, keepdims=True)
        p = jnp.exp2(s - m)
        l = jnp.sum(p, axis=-1, keepdims=True)
        acc = jnp.dot(p.astype(BF16), vv[:, sl], preferred_element_type=F32)
        return (jnp.broadcast_to(m, (BLOCK, DIL_HEAD_DIM)),
                jnp.broadcast_to(l, (BLOCK, DIL_HEAD_DIM)), acc)

    def merge(pi, rows, parts):
        m2, l2, a2 = (jnp.concatenate([p[c] for p in parts], axis=-1) for c in range(3))
        if pi > 0:
            m_old = m_sc[rows, :]
            m_new = jnp.maximum(m_old, m2)
            w_old, w_new = jnp.exp2(m_old - m_new), jnp.exp2(m2 - m_new)
            l2 = w_old * l_sc[rows, :] + w_new * l2
            a2 = w_old * n_sc[rows, :] + w_new * a2
            m2 = m_new
        if pi == n_patterns - 1:
            o_ref[rows, :] = a2 / l2
        else:
            m_sc[rows, :] = m2
            l_sc[rows, :] = l2
            n_sc[rows, :] = a2

    items = [(b, hd) for b in range(len(blocks)) for hd in range(heads)]
    blk_next = load(*blocks[0])
    s_next = scores(blk_next, 0)
    parts = []
    for t, (b, hd) in enumerate(items):
        blk, s_cur = blk_next, s_next
        if t + 1 < len(items):
            nb_, nhd = items[t + 1]
            if nb_ != b:
                blk_next = load(*blocks[nb_])
            s_next = scores(blk_next, nhd)
        parts.append(softmax_pv(blk, hd, s_cur))
        if hd == heads - 1:
            merge(blocks[b][0], blk[0], parts)
            parts = []


def _dil_attention(proj):
    def spec(col0):
        return pl.BlockSpec((SEQ, LANES), lambda b, hp: (b, col0 // LANES + hp))

    return pl.pallas_call(
        _dil_attn_kernel,
        grid=(BATCH, DIL_WIDTH // LANES),
        in_specs=[spec(DIL_Q_COL), spec(DIL_K_COL), spec(DIL_V_COL)],
        out_specs=pl.BlockSpec((SEQ, LANES), lambda b, hp: (b, hp)),
        out_shape=jax.ShapeDtypeStruct((TOKENS, DIL_WIDTH), F32),
        scratch_shapes=[pltpu.VMEM((SEQ, LANES), F32)] * 3,
        compiler_params=_cparams("parallel", "parallel"),
        name="dilated_attention",
    )(proj, proj, proj)


SLAB_GROUPS = LANES // SSM_GROUP
SSM_SLABS = SSM_GROUPS // SLAB_GROUPS
SSM_NCHUNK = SEQ // SSM_CHUNK
SSM_ROWS = BATCH * SSM_NCHUNK
SSM_ROW = SSM_CHUNK * LANES
SLAB_STATE = SLAB_GROUPS * SSM_STATE
STATE_TILES = SLAB_STATE // LANES
SUBLANES = 8


def _split_bf16(x):
    hi = x.astype(BF16)
    return hi, (x - hi.astype(F32)).astype(BF16)


def _ssm_kernel(u_ref, t0_ref, m1re_ref, m1im_ref, m2re_ref, m2im_ref,
                are_ref, aim_ref, d_ref, y_ref, fold_ref, st_ref):
    for b in range(BATCH):
        rows = slice(b * SSM_NCHUNK, (b + 1) * SSM_NCHUNK)
        for t in range(SSM_CHUNK):
            fold_ref[rows, t * LANES:(t + 1) * LANES] = (
                u_ref[pl.ds(b * SEQ + t, SSM_NCHUNK, stride=SSM_CHUNK), :])
    u = fold_ref[...]
    ub = u.astype(BF16)

    loc_re = jnp.dot(ub, m1re_ref[0], preferred_element_type=F32)
    loc_im = jnp.dot(ub, m1im_ref[0], preferred_element_type=F32)
    for b in range(BATCH):
        rows = slice(b * SSM_NCHUNK, (b + 1) * SSM_NCHUNK)
        for k in range(STATE_TILES):
            lanes = slice(k * LANES, (k + 1) * LANES)
            st_ref[k, pl.ds(b, SSM_NCHUNK, stride=SUBLANES), :] = loc_re[rows, lanes]
            st_ref[k, pl.ds(BATCH + b, SSM_NCHUNK, stride=SUBLANES), :] = loc_im[rows, lanes]

    a_re, a_im = are_ref[0], aim_ref[0]
    upper = lax.broadcasted_iota(jnp.int32, (SUBLANES, LANES), 0) < BATCH
    mul_same, mul_swap = [], []
    for k in range(STATE_TILES):
        lanes = slice(k * LANES, (k + 1) * LANES)
        mul_same.append(jnp.broadcast_to(a_re[:, lanes], (SUBLANES, LANES)))
        im = jnp.broadcast_to(a_im[:, lanes], (SUBLANES, LANES))
        mul_swap.append(jnp.where(upper, -im, im))

    def chunk_step(c, state):
        r0 = pl.multiple_of(c * SUBLANES, SUBLANES)
        new = []
        for k in range(STATE_TILES):
            loc = st_ref[k, pl.ds(r0, SUBLANES), :]
            st_ref[k, pl.ds(r0, SUBLANES), :] = state[k]
            swapped = pltpu.roll(state[k], BATCH, axis=0)
            new.append(mul_same[k] * state[k] + mul_swap[k] * swapped + loc)
        return tuple(new)

    zero = jnp.zeros((SUBLANES, LANES), F32)
    lax.fori_loop(0, SSM_NCHUNK, chunk_step, (zero,) * STATE_TILES, unroll=4)

    def entering(offset):
        return jnp.concatenate(
            [jnp.concatenate([st_ref[k, pl.ds(offset + b, SSM_NCHUNK, stride=SUBLANES), :]
                              for k in range(STATE_TILES)], axis=1)
             for b in range(BATCH)], axis=0)

    y = jnp.dot(ub, t0_ref[0], preferred_element_type=F32) + u * d_ref[0]
    for offset, m2_ref in ((0, m2re_ref), (BATCH, m2im_ref)):
        hi, lo = _split_bf16(entering(offset))
        m2 = m2_ref[0]
        y = y + (jnp.dot(hi, m2, preferred_element_type=F32)
                 + jnp.dot(lo, m2, preferred_element_type=F32))
    fold_ref[...] = jax.nn.gelu(y, approximate=True)

    for b in range(BATCH):
        rows = slice(b * SSM_NCHUNK, (b + 1) * SSM_NCHUNK)
        for t in range(SSM_CHUNK):
            y_ref[pl.ds(b * SEQ + t, SSM_NCHUNK, stride=SSM_CHUNK), :] = (
                fold_ref[rows, t * LANES:(t + 1) * LANES])


def _ssm_scan(proj, mats):
    t0, m1re, m1im, m2re, m2im, a_re, a_im, dvec = mats

    def spec(r, c):
        return pl.BlockSpec((1, r, c), lambda s: (s, 0, 0))

    return pl.pallas_call(
        _ssm_kernel,
        grid=(SSM_SLABS,),
        in_specs=[pl.BlockSpec((TOKENS, LANES), lambda s: (0, SSM_U_COL // LANES + s)),
                  spec(SSM_ROW, SSM_ROW),
                  spec(SSM_ROW, SLAB_STATE), spec(SSM_ROW, SLAB_STATE),
                  spec(SLAB_STATE, SSM_ROW), spec(SLAB_STATE, SSM_ROW),
                  spec(1, SLAB_STATE), spec(1, SLAB_STATE), spec(1, SSM_ROW)],
        out_specs=pl.BlockSpec((TOKENS, LANES), lambda s: (0, s)),
        out_shape=jax.ShapeDtypeStruct((TOKENS, SSM_WIDTH), F32),
        scratch_shapes=[pltpu.VMEM((SSM_ROWS, SSM_ROW), F32),
                        pltpu.VMEM((STATE_TILES, SUBLANES * SSM_NCHUNK, LANES), F32)],
        compiler_params=_cparams("arbitrary"),
        name="ssm_chunk_scan",
    )(proj, t0, m1re, m1im, m2re, m2im, a_re, a_im, dvec)


def _ssm_matrices(a_re, a_im, b_re, b_im, c_re, c_im, d_skip, log_dt):
    lam_re = jnp.minimum(a_re, -1e-4)
    lam_im = a_im
    dt = jnp.exp(log_dt)[:, None]
    mag = jnp.exp(lam_re * dt)
    ab_re, ab_im = mag * jnp.cos(lam_im * dt), mag * jnp.sin(lam_im * dt)
    n_re, n_im = ab_re - 1.0, ab_im
    den = lam_re * lam_re + lam_im * lam_im
    f_re = (n_re * lam_re + n_im * lam_im) / den
    f_im = (n_im * lam_re - n_re * lam_im) / den
    bb_re = f_re[..., None] * b_re - f_im[..., None] * b_im
    bb_im = f_re[..., None] * b_im + f_im[..., None] * b_re
    p_re, p_im = [jnp.ones_like(ab_re)], [jnp.zeros_like(ab_im)]
    for _ in range(SSM_CHUNK):
        p_re.append(p_re[-1] * ab_re - p_im[-1] * ab_im)
        p_im.append(p_re[-2] * ab_im + p_im[-1] * ab_re)
    pw_re, pw_im = jnp.stack(p_re), jnp.stack(p_im)

    cb_re = (c_re[:, None, :, :] * bb_re.transpose(0, 2, 1)[:, :, None, :]
             - c_im[:, None, :, :] * bb_im.transpose(0, 2, 1)[:, :, None, :])
    cb_im = (c_re[:, None, :, :] * bb_im.transpose(0, 2, 1)[:, :, None, :]
             + c_im[:, None, :, :] * bb_re.transpose(0, 2, 1)[:, :, None, :])
    lag_re = pw_re[:SSM_CHUNK].transpose(1, 0, 2)[:, :, None, None, :]
    lag_im = pw_im[:SSM_CHUNK].transpose(1, 0, 2)[:, :, None, None, :]
    kern = jnp.sum(lag_re * cb_re[:, None] - lag_im * cb_im[:, None], axis=-1)
    kern = jnp.concatenate([kern, jnp.zeros_like(kern[:, :1])], axis=1)
    t_idx = np.arange(SSM_CHUNK)
    lag = t_idx[None, :] - t_idx[:, None]
    lag = np.where(lag >= 0, lag, SSM_CHUNK)
    t0 = kern[:, lag]

    back = pw_re[SSM_CHUNK - 1 - t_idx], pw_im[SSM_CHUNK - 1 - t_idx]
    bre_t, bim_t = bb_re.transpose(0, 2, 1), bb_im.transpose(0, 2, 1)
    m1re = (back[0].transpose(1, 0, 2)[:, :, None, :] * bre_t[:, None]
            - back[1].transpose(1, 0, 2)[:, :, None, :] * bim_t[:, None])
    m1im = (back[0].transpose(1, 0, 2)[:, :, None, :] * bim_t[:, None]
            + back[1].transpose(1, 0, 2)[:, :, None, :] * bre_t[:, None])

    fwd_re = pw_re[1:].transpose(1, 0, 2)[:, :, None, :]
    fwd_im = pw_im[1:].transpose(1, 0, 2)[:, :, None, :]
    ca_re = c_re[:, None] * fwd_re - c_im[:, None] * fwd_im
    ca_im = c_re[:, None] * fwd_im + c_im[:, None] * fwd_re

    eye = jnp.eye(SLAB_GROUPS, dtype=F32)

    def per_slab(x):
        return x.reshape((SSM_SLABS, SLAB_GROUPS) + x.shape[1:])

    def slab_t0(x):
        x = per_slab(x).transpose(0, 2, 1, 4, 3, 5)
        x = x[:, :, :, :, :, None, :] * eye[None, None, :, None, None, :, None]
        return x.reshape(SSM_SLABS, SSM_ROW, SSM_ROW)

    def slab_m1(x):
        x = per_slab(x).transpose(0, 2, 1, 3, 4)
        x = x[:, :, :, :, None, :] * eye[None, None, :, None, :, None]
        return x.reshape(SSM_SLABS, SSM_ROW, SLAB_STATE)

    def slab_m2(x):
        x = per_slab(x).transpose(0, 1, 4, 2, 3)
        x = x[:, :, :, :, None, :] * eye[None, :, None, None, :, None]
        return x.reshape(SSM_SLABS, SLAB_STATE, SSM_ROW)

    a_step_re = pw_re[SSM_CHUNK].reshape(SSM_SLABS, 1, SLAB_STATE)
    a_step_im = pw_im[SSM_CHUNK].reshape(SSM_SLABS, 1, SLAB_STATE)
    dvec = jnp.tile(per_slab(d_skip)[:, None], (1, SSM_CHUNK, 1, 1)).reshape(SSM_SLABS, 1, SSM_ROW)
    return (slab_t0(t0).astype(BF16), slab_m1(m1re).astype(BF16), slab_m1(m1im).astype(BF16),
            slab_m2(ca_re).astype(BF16), slab_m2(-ca_im).astype(BF16),
            a_step_re, a_step_im, dvec)


def _glu_kernel(y_ref, w_ref, b_ref, o_ref):
    z = jnp.dot(y_ref[...].astype(BF16), _bf16(w_ref[...]),
                preferred_element_type=F32) + b_ref[...]
    o_ref[...] = z[:, :SSM_WIDTH] * jax.nn.sigmoid(z[:, SSM_WIDTH:])


def _glu(y, w_glu, b_glu, layer, *, bm):
    return pl.pallas_call(
        _glu_kernel,
        grid=(TOKENS // bm,),
        in_specs=[pl.BlockSpec((bm, SSM_WIDTH), lambda i: (i, 0)),
                  _layer_spec((SSM_WIDTH, 2 * SSM_WIDTH), lambda i: (layer, 0, 0)),
                  _layer_spec((1, 2 * SSM_WIDTH), lambda i: (layer, 0, 0))],
        out_specs=pl.BlockSpec((bm, SSM_WIDTH), lambda i: (i, 0)),
        out_shape=jax.ShapeDtypeStruct((TOKENS, SSM_WIDTH), F32),
        compiler_params=_cparams("parallel"),
        name="ssm_glu",
    )(y, w_glu, b_glu.reshape(DEPTH, 1, -1))


def _out_proj_kernel(ya_ref, yb_ref, yc_ref, ga_ref, gb_ref, gc_ref, w_ref, x_ref,
                     o_ref, h_ref):
    @pl.when(pl.program_id(1) == 0)
    def _():
        _norm_into(h_ref, ya_ref, ga_ref, 0)
        _norm_into(h_ref, yb_ref, gb_ref, MLA_WIDTH)
        _norm_into(h_ref, yc_ref, gc_ref, MLA_WIDTH + SSM_WIDTH)

    o_ref[...] = x_ref[...] + jnp.dot(h_ref[...], _bf16(w_ref[...]),
                                      preferred_element_type=F32)


def _out_proj(y_mla, y_ssm, y_dil, g_mla, g_ssm, g_dil, w_o, x, layer, *, bm, bn):
    def rows(width):
        return pl.BlockSpec((bm, width), lambda i, j: (i, 0))

    def gain(width):
        return _layer_spec((1, width), lambda i, j: (layer, 0, 0))

    return pl.pallas_call(
        _out_proj_kernel,
        grid=(TOKENS // bm, D_MODEL // bn),
        in_specs=[rows(MLA_WIDTH), rows(SSM_WIDTH), rows(DIL_WIDTH),
                  gain(MLA_WIDTH), gain(SSM_WIDTH), gain(DIL_WIDTH),
                  _layer_spec((D_MODEL, bn), lambda i, j: (layer, 0, j)),
                  pl.BlockSpec((bm, bn), lambda i, j: (i, j))],
        out_specs=pl.BlockSpec((bm, bn), lambda i, j: (i, j)),
        out_shape=jax.ShapeDtypeStruct((TOKENS, D_MODEL), F32),
        scratch_shapes=[pltpu.VMEM((bm, D_MODEL), BF16)],
        compiler_params=_cparams("parallel", "arbitrary"),
        name="out_proj",
    )(y_mla, y_ssm, y_dil, g_mla.reshape(DEPTH, 1, -1), g_ssm.reshape(DEPTH, 1, -1),
      g_dil.reshape(DEPTH, 1, -1), w_o, x)


def _ffn_up_kernel(x_ref, g_ref, wg_ref, wu_ref, o_ref, h_ref):
    @pl.when(pl.program_id(1) == 0)
    def _():
        _norm_into(h_ref, x_ref, g_ref)

    h = h_ref[...]
    gate = jnp.dot(h, _bf16(wg_ref[...]), preferred_element_type=F32)
    up = jnp.dot(h, _bf16(wu_ref[...]), preferred_element_type=F32)
    o_ref[...] = (jax.nn.silu(gate) * up).astype(o_ref.dtype)


def _ffn_up(x, g, w_gate, w_up, layer, *, bm, bn):
    return pl.pallas_call(
        _ffn_up_kernel,
        grid=(TOKENS // bm, D_FF // bn),
        in_specs=[pl.BlockSpec((bm, D_MODEL), lambda i, j: (i, 0)),
                  _layer_spec((1, D_MODEL), lambda i, j: (layer, 0, 0)),
                  _layer_spec((D_MODEL, bn), lambda i, j: (layer, 0, j)),
                  _layer_spec((D_MODEL, bn), lambda i, j: (layer, 0, j))],
        out_specs=pl.BlockSpec((bm, bn), lambda i, j: (i, j)),
        out_shape=jax.ShapeDtypeStruct((TOKENS, D_FF), BF16),
        scratch_shapes=[pltpu.VMEM((bm, D_MODEL), BF16)],
        compiler_params=_cparams("parallel", "arbitrary"),
        name="ffn_gate_up",
    )(x, g.reshape(DEPTH, 1, -1), w_gate, w_up)


def _ffn_down_kernel(a_ref, w_ref, x_ref, o_ref):
    o_ref[...] = x_ref[...] + jnp.dot(a_ref[...], _bf16(w_ref[...]),
                                      preferred_element_type=F32)


def _ffn_down(a, w_down, x, layer, *, bm, bn):
    return pl.pallas_call(
        _ffn_down_kernel,
        grid=(TOKENS // bm, D_MODEL // bn),
        in_specs=[pl.BlockSpec((bm, D_FF), lambda i, j: (i, 0)),
                  _layer_spec((D_FF, bn), lambda i, j: (layer, 0, j)),
                  pl.BlockSpec((bm, bn), lambda i, j: (i, j))],
        out_specs=pl.BlockSpec((bm, bn), lambda i, j: (i, j)),
        out_shape=jax.ShapeDtypeStruct((TOKENS, D_MODEL), F32),
        compiler_params=_cparams("parallel", "arbitrary"),
        name="ffn_down",
    )(a, w_down, x)


def _final_norm_kernel(x_ref, g_ref, o_ref):
    o_ref[...] = _rms_rows(x_ref[...], g_ref[...])


def _final_norm(x, g, *, bm):
    return pl.pallas_call(
        _final_norm_kernel,
        grid=(TOKENS // bm,),
        in_specs=[pl.BlockSpec((bm, D_MODEL), lambda i: (i, 0)),
                  pl.BlockSpec((1, D_MODEL), lambda i: (0, 0))],
        out_specs=pl.BlockSpec((bm, D_MODEL), lambda i: (i, 0)),
        out_shape=jax.ShapeDtypeStruct((TOKENS, D_MODEL), F32),
        compiler_params=_cparams("parallel"),
        name="final_norm",
    )(x, g.reshape(1, -1))


def _pad_w_in(w_in):
    lyr, d, _ = w_in.shape
    z = lambda n: jnp.zeros((lyr, d, n), BF16)
    w = w_in.astype(BF16)
    return jnp.concatenate([w[..., :768], z(128), w[..., 768:832], z(64), w[..., 832:]],
                           axis=-1)


def _pad_wt_uq(w_uq):
    lyr = w_uq.shape[0]
    w = w_uq.reshape(lyr, MLA_Q_LORA, MLA_HEADS, MLA_NOPE + MLA_ROPE)
    w = jnp.pad(w, ((0, 0), (0, 0), (0, 0), (0, MLA_QK_PAD - MLA_NOPE - MLA_ROPE)))
    w = w.reshape(lyr, MLA_Q_LORA, MLA_HEADS * MLA_QK_PAD)
    return w.transpose(0, 2, 1).astype(BF16)


def _split_w_ukv(w_ukv):
    lyr = w_ukv.shape[0]
    w = w_ukv.reshape(lyr, MLA_KV_LORA, MLA_HEADS, MLA_NOPE + MLA_V)
    wk = jnp.pad(w[..., :MLA_NOPE], ((0, 0), (0, 0), (0, 0), (0, MLA_QK_PAD - MLA_NOPE)))
    wk = wk.reshape(lyr, MLA_KV_LORA, -1).astype(BF16)
    wv = w[..., MLA_NOPE:].reshape(lyr, MLA_KV_LORA, -1)
    return wk, wv.transpose(0, 2, 1).astype(BF16)


def _rope_angles():
    inv_freq = ROPE_THETA ** (-jnp.arange(ROPE_HALF, dtype=F32) / ROPE_HALF)
    ang = jnp.arange(SEQ, dtype=F32)[:, None] * inv_freq[None, :]
    return jnp.cos(ang), jnp.sin(ang)


def _rope_tables_k():
    cos, sin = _rope_angles()
    one = jnp.ones((SEQ, MLA_NOPE), F32)
    z = lambda n: jnp.zeros((SEQ, n), F32)
    tail = MLA_QK_PAD - MLA_NOPE - MLA_ROPE
    cos_t = jnp.concatenate([one, cos, cos, z(tail)], axis=1)
    sin_a = jnp.concatenate([z(MLA_NOPE), -sin, z(ROPE_HALF), z(tail)], axis=1)
    sin_b = jnp.concatenate([z(MLA_NOPE), z(ROPE_HALF), sin, z(tail)], axis=1)
    return cos_t, sin_a, sin_b


def kernel(x, g_mix, w_in, g_q, w_uq, g_kv, w_ukv, a_re, a_im, b_re, b_im, c_re, c_im,
           d_skip, log_dt, w_glu, b_glu, g_out_mla, g_out_ssm, g_out_dil, w_o,
           g_ffn, w_gate, w_up, w_down, g_final):
    x = x.reshape(TOKENS, D_MODEL)
    w_in_p = _pad_w_in(w_in)
    wt_q = _pad_wt_uq(w_uq)
    w_k, wt_v = _split_w_ukv(w_ukv)
    q_scale = (MLA_NOPE + MLA_ROPE) ** -0.5 * math.log2(math.e)
    cos, sin = _rope_angles()
    q_cos_t, q_sin_t = (cos * q_scale).T, (sin * q_scale).T
    k_tabs = _rope_tables_k()

    for l in range(DEPTH):
        proj = _norm_matmul(x, g_mix, w_in_p, l, bm=1024, bn=1024, out_dtype=F32,
                            name="in_proj")
        qt = _q_up(proj, g_q, wt_q, q_cos_t, q_sin_t, l, q_scale, bm=512)
        k, vt = _kv_up(proj, g_kv, w_k, wt_v, k_tabs, l, bm=512)
        y_mla = _mla_attention(qt, k, vt)
        mats = _ssm_matrices(a_re[l], a_im[l], b_re[l], b_im[l], c_re[l], c_im[l],
                             d_skip[l], log_dt[l])
        y_ssm = _glu(_ssm_scan(proj, mats), w_glu, b_glu, l, bm=1024)
        y_dil = _dil_attention(proj)
        x = _out_proj(y_mla, y_ssm, y_dil, g_out_mla, g_out_ssm, g_out_dil,
                      w_o, x, l, bm=1024, bn=512)
        act = _ffn_up(x, g_ffn, w_gate, w_up, l, bm=1024, bn=512)
        x = _ffn_down(act, w_down, x, l, bm=1024, bn=256)
    out = _final_norm(x, g_final, bm=512)
    return out.reshape(BATCH, SEQ, D_MODEL)
```

```python
import functools
import math

import jax
import jax.numpy as jnp
import numpy as np
from jax import lax
from jax.experimental import pallas as pl
from jax.experimental.pallas import tpu as pltpu

F32 = jnp.float32
BF16 = jnp.bfloat16

D_MODEL = 2048
BATCH = 4
SEQ = 2048
DEPTH = 4
TOKENS = BATCH * SEQ

MLA_HEADS = 8
MLA_NOPE = 128
MLA_ROPE = 64
MLA_V = 128
MLA_Q_LORA = 512
MLA_KV_LORA = 256
MLA_WIDTH = MLA_HEADS * MLA_V
MLA_QK_PAD = 256
ROPE_THETA = 10000.0
ROPE_HALF = MLA_ROPE // 2

SSM_WIDTH = 512
SSM_GROUP = 16
SSM_GROUPS = 32
SSM_STATE = 64
SSM_CHUNK = 8

DIL_WIDTH = 512
DIL_HEAD_DIM = 64
DIL_HEADS = 8
DIL_PATTERNS = ((128, 1), (512, 4), (2048, 16))
BLOCK = 128

IN_PAD = 3072
DIL_Q_COL = 1536
DIL_K_COL = 2048
DIL_V_COL = 2560
SSM_U_COL = 1024
D_FF = 5632
NORM_EPS = 1e-6

LANES = 128
VMEM_LIMIT_BYTES = 56 * 1024 * 1024

NT_DIMS = (((1,), (1,)), ((), ()))


def _cparams(*semantics):
    return pltpu.CompilerParams(dimension_semantics=semantics,
                                vmem_limit_bytes=VMEM_LIMIT_BYTES)


def _rms_rows(x, g):
    ms = jnp.mean(x * x, axis=-1, keepdims=True)
    return x * lax.rsqrt(ms + NORM_EPS) * g


def _norm_into(h_ref, x_ref, g_ref, col0=0, chunk=256):
    rows, width = x_ref.shape
    g = g_ref[...]
    for r in range(0, rows, chunk):
        x = x_ref[r:r + chunk, :]
        h_ref[r:r + chunk, col0:col0 + width] = _rms_rows(x, g).astype(BF16)


def _bf16(w):
    return w if w.dtype == BF16 else w.astype(BF16)


def _layer_spec(shape, index_map):
    return pl.BlockSpec((None,) + tuple(shape), index_map)


def _norm_matmul_kernel(x_ref, g_ref, w_ref, o_ref, h_ref):
    @pl.when(pl.program_id(1) == 0)
    def _():
        _norm_into(h_ref, x_ref, g_ref)

    o_ref[...] = jnp.dot(h_ref[...], _bf16(w_ref[...]),
                         preferred_element_type=F32).astype(o_ref.dtype)


def _norm_matmul(x, g, w, layer, *, bm, bn, out_dtype, name):
    m, k = x.shape
    n = w.shape[2]
    return pl.pallas_call(
        _norm_matmul_kernel,
        grid=(m // bm, n // bn),
        in_specs=[pl.BlockSpec((bm, k), lambda i, j: (i, 0)),
                  _layer_spec((1, k), lambda i, j: (layer, 0, 0)),
                  _layer_spec((k, bn), lambda i, j: (layer, 0, j))],
        out_specs=pl.BlockSpec((bm, bn), lambda i, j: (i, j)),
        out_shape=jax.ShapeDtypeStruct((m, n), out_dtype),
        scratch_shapes=[pltpu.VMEM((bm, k), BF16)],
        compiler_params=_cparams("parallel", "arbitrary"),
        name=name,
    )(x, g.reshape(DEPTH, 1, k), w)


def _q_up_kernel(c_ref, g_ref, wt_ref, cos_ref, sin_ref, o_ref, *, scale):
    h = _rms_rows(c_ref[...], g_ref[...]).astype(BF16)
    qt = lax.dot_general(wt_ref[...], h, NT_DIMS, preferred_element_type=F32)
    cos, sin = cos_ref[...], sin_ref[...]
    for hd in range(MLA_HEADS):
        r0 = hd * MLA_QK_PAD
        r1, r2, r3 = r0 + MLA_NOPE, r0 + MLA_NOPE + ROPE_HALF, r0 + MLA_NOPE + MLA_ROPE
        x1, x2 = qt[r1:r2], qt[r2:r3]
        o_ref[r0:r1, :] = (qt[r0:r1] * scale).astype(o_ref.dtype)
        o_ref[r1:r2, :] = (x1 * cos - x2 * sin).astype(o_ref.dtype)
        o_ref[r2:r3, :] = (x2 * cos + x1 * sin).astype(o_ref.dtype)
        o_ref[r3:r0 + MLA_QK_PAD, :] = qt[r3:r0 + MLA_QK_PAD].astype(o_ref.dtype)


def _q_up(proj, g_q, wt_q, cos_t, sin_t, layer, scale, *, bm):
    pos_blocks = SEQ // bm
    tab_spec = pl.BlockSpec((ROPE_HALF, bm), lambda i: (0, i % pos_blocks))
    n = MLA_HEADS * MLA_QK_PAD
    return pl.pallas_call(
        functools.partial(_q_up_kernel, scale=scale),
        grid=(TOKENS // bm,),
        in_specs=[pl.BlockSpec((bm, MLA_Q_LORA), lambda i: (i, 0)),
                  _layer_spec((1, MLA_Q_LORA), lambda i: (layer, 0, 0)),
                  _layer_spec((n, MLA_Q_LORA), lambda i: (layer, 0, 0)),
                  tab_spec, tab_spec],
        out_specs=pl.BlockSpec((n, bm), lambda i: (0, i)),
        out_shape=jax.ShapeDtypeStruct((n, TOKENS), BF16),
        compiler_params=_cparams("parallel"),
        name="mla_q_up",
    )(proj, g_q.reshape(DEPTH, 1, -1), wt_q, cos_t, sin_t)


def _rope_pad(x, cos_t, sin_a, sin_b):
    up = pltpu.roll(x, MLA_QK_PAD - ROPE_HALF, axis=1)
    dn = pltpu.roll(x, ROPE_HALF, axis=1)
    return x * cos_t + up * sin_a + dn * sin_b


def _kv_up_kernel(c_ref, kr_ref, g_ref, wk_ref, wvt_ref, cos_ref, sina_ref, sinb_ref,
                  k_ref, vt_ref):
    h = _rms_rows(c_ref[...], g_ref[...]).astype(BF16)
    kn = jnp.dot(h, wk_ref[...], preferred_element_type=F32)
    k_pe = _rope_pad(kr_ref[...], cos_ref[...], sina_ref[...], sinb_ref[...])
    for hd in range(MLA_HEADS):
        sl = slice(hd * MLA_QK_PAD, (hd + 1) * MLA_QK_PAD)
        k_ref[:, sl] = (kn[:, sl] + k_pe).astype(k_ref.dtype)
    vt_ref[...] = lax.dot_general(wvt_ref[...], h, NT_DIMS,
                                  preferred_element_type=F32).astype(vt_ref.dtype)


def _kv_up(proj, g_kv, w_k, wt_v, tabs, layer, *, bm):
    pos_blocks = SEQ // bm
    tab_spec = pl.BlockSpec((bm, MLA_QK_PAD), lambda i: (i % pos_blocks, 0))
    nk = MLA_HEADS * MLA_QK_PAD
    return pl.pallas_call(
        _kv_up_kernel,
        grid=(TOKENS // bm,),
        in_specs=[pl.BlockSpec((bm, MLA_KV_LORA), lambda i: (i, 2)),
                  pl.BlockSpec((bm, MLA_QK_PAD), lambda i: (i, 3)),
                  _layer_spec((1, MLA_KV_LORA), lambda i: (layer, 0, 0)),
                  _layer_spec((MLA_KV_LORA, nk), lambda i: (layer, 0, 0)),
                  _layer_spec((MLA_WIDTH, MLA_KV_LORA), lambda i: (layer, 0, 0)),
                  tab_spec, tab_spec, tab_spec],
        out_specs=[pl.BlockSpec((bm, nk), lambda i: (i, 0)),
                   pl.BlockSpec((MLA_WIDTH, bm), lambda i: (0, i))],
        out_shape=[jax.ShapeDtypeStruct((TOKENS, nk), BF16),
                   jax.ShapeDtypeStruct((MLA_WIDTH, TOKENS), BF16)],
        compiler_params=_cparams("parallel"),
        name="mla_kv_up",
    )(proj, proj, g_kv.reshape(DEPTH, 1, -1), w_k, wt_v, *tabs)


ATT_BQ = 256
ATT_BK = 256
ATT_NQ = SEQ // ATT_BQ


def _mla_attn_kernel(qt_ref, k_ref, vt_ref, o_ref, m_sc, l_sc, acc_sc):
    key = lax.broadcasted_iota(jnp.int32, (ATT_BK, ATT_BQ), 0)
    qry = lax.broadcasted_iota(jnp.int32, (ATT_BK, ATT_BQ), 1)
    causal = key <= qry
    tiles = [(i, j) for j in range(ATT_NQ) for i in range(j, ATT_NQ)]

    def scores(i, j):
        kj = k_ref[j * ATT_BK:(j + 1) * ATT_BK, :]
        qi = qt_ref[:, i * ATT_BQ:(i + 1) * ATT_BQ]
        return jnp.dot(kj, qi, preferred_element_type=F32)

    def absorb(i, j, s):
        vj = vt_ref[:, j * ATT_BK:(j + 1) * ATT_BK]
        if i == j:
            s = jnp.where(causal, s, -jnp.inf)
        m_blk = jnp.max(s, axis=0, keepdims=True)
        if j == 0:
            m_new = m_blk
            p = jnp.exp2(s - m_new)
            l = jnp.sum(p, axis=0, keepdims=True)
            acc = jnp.dot(vj, p.astype(BF16), preferred_element_type=F32)
        else:
            m_old = m_sc[i]
            m_new = jnp.maximum(m_old, m_blk)
            alpha = jnp.exp2(m_old - m_new)
            p = jnp.exp2(s - m_new)
            l = alpha * l_sc[i] + jnp.sum(p, axis=0, keepdims=True)
            acc = alpha * acc_sc[i] + jnp.dot(vj, p.astype(BF16),
                                              preferred_element_type=F32)
        if i == j:
            o_ref[i * ATT_BQ:(i + 1) * ATT_BQ, :] = (acc / l).T.astype(o_ref.dtype)
        else:
            m_sc[i] = m_new
            l_sc[i] = l
            acc_sc[i] = acc

    s_next = scores(*tiles[0])
    for t, (i, j) in enumerate(tiles):
        s_cur = s_next
        if t + 1 < len(tiles):
            s_next = scores(*tiles[t + 1])
        absorb(i, j, s_cur)


def _mla_attention(qt, k, vt):
    return pl.pallas_call(
        _mla_attn_kernel,
        grid=(BATCH, MLA_HEADS),
        in_specs=[pl.BlockSpec((MLA_QK_PAD, SEQ), lambda b, h: (h, b)),
                  pl.BlockSpec((SEQ, MLA_QK_PAD), lambda b, h: (b, h)),
                  pl.BlockSpec((MLA_V, SEQ), lambda b, h: (h, b))],
        out_specs=pl.BlockSpec((SEQ, MLA_V), lambda b, h: (b, h)),
        out_shape=jax.ShapeDtypeStruct((TOKENS, MLA_WIDTH), F32),
        scratch_shapes=[pltpu.VMEM((ATT_NQ, 1, ATT_BQ), F32),
                        pltpu.VMEM((ATT_NQ, 1, ATT_BQ), F32),
                        pltpu.VMEM((ATT_NQ, MLA_V, ATT_BQ), F32)],
        compiler_params=_cparams("parallel", "parallel"),
        name="mla_attention",
    )(qt, k, vt)


def _dil_attn_kernel(q_ref, k_ref, v_ref, o_ref, m_sc, l_sc, n_sc):
    row2 = lax.broadcasted_iota(jnp.int32, (BLOCK, 2 * BLOCK), 0)
    col2 = lax.broadcasted_iota(jnp.int32, (BLOCK, 2 * BLOCK), 1)
    dist = row2 + BLOCK - col2
    band = (dist >= 0) & (dist <= BLOCK)
    row1 = lax.broadcasted_iota(jnp.int32, (BLOCK, BLOCK), 0)
    col1 = lax.broadcasted_iota(jnp.int32, (BLOCK, BLOCK), 1)
    tri = row1 >= col1
    q_scale = DIL_HEAD_DIM ** -0.5 * math.log2(math.e)
    heads = LANES // DIL_HEAD_DIM
    n_patterns = len(DIL_PATTERNS)

    def rows_at(start, dil):
        return pl.ds(start, BLOCK) if dil == 1 else pl.ds(start, BLOCK, stride=dil)

    blocks = [(pi, dil, r, n) for pi, (_, dil) in enumerate(reversed(DIL_PATTERNS))
              for r in range(dil) for n in range(SEQ // dil // BLOCK)]

    def load(pi, dil, r, n):
        rows = rows_at(r + dil * BLOCK * n, dil)
        q = (q_ref[rows, :] * q_scale).astype(BF16)
        if n == 0:
            return rows, q, k_ref[rows, :].astype(BF16), v_ref[rows, :].astype(BF16), tri
        prev = rows_at(r + dil * BLOCK * (n - 1), dil)
        kk = jnp.concatenate([k_ref[prev, :], k_ref[rows, :]], axis=0).astype(BF16)
        vv = jnp.concatenate([v_ref[prev, :], v_ref[rows, :]], axis=0).astype(BF16)
        return rows, q, kk, vv, band

    def scores(blk, hd):
        _, q, kk, _, _ = blk
        sl = slice(hd * DIL_HEAD_DIM, (hd + 1) * DIL_HEAD_DIM)
        return lax.dot_general(q[:, sl], kk[:, sl], NT_DIMS, preferred_element_type=F32)

    def softmax_pv(blk, hd, s):
        _, _, _, vv, mask = blk
        sl = slice(hd * DIL_HEAD_DIM, (hd + 1) * DIL_HEAD_DIM)
        s = jnp.where(mask, s, -jnp.inf)
        m = jnp.max(s, axis=-1, keepdims=True)
        p = jnp.exp2(s - m)
        l = jnp.sum(p, axis=-1, keepdims=True)
        acc = jnp.dot(p.astype(BF16), vv[:, sl], preferred_element_type=F32)
        return (jnp.broadcast_to(m, (BLOCK, DIL_HEAD_DIM)),
                jnp.broadcast_to(l, (BLOCK, DIL_HEAD_DIM)), acc)

    def merge(pi, rows, parts):
        m2, l2, a2 = (jnp.concatenate([p[c] for p in parts], axis=-1) for c in range(3))
        if pi > 0:
            m_old = m_sc[rows, :]
            m_new = jnp.maximum(m_old, m2)
            w_old, w_new = jnp.exp2(m_old - m_new), jnp.exp2(m2 - m_new)
            l2 = w_old * l_sc[rows, :] + w_new * l2
            a2 = w_old * n_sc[rows, :] + w_new * a2
            m2 = m_new
        if pi == n_patterns - 1:
            o_ref[rows, :] = a2 / l2
        else:
            m_sc[rows, :] = m2
            l_sc[rows, :] = l2
            n_sc[rows, :] = a2

    items = [(b, hd) for b in range(len(blocks)) for hd in range(heads)]
    blk_next = load(*blocks[0])
    s_next = scores(blk_next, 0)
    parts = []
    for t, (b, hd) in enumerate(items):
        blk, s_cur = blk_next, s_next
        if t + 1 < len(items):
            nb_, nhd = items[t + 1]
            if nb_ != b:
                blk_next = load(*blocks[nb_])
            s_next = scores(blk_next, nhd)
        parts.append(softmax_pv(blk, hd, s_cur))
        if hd == heads - 1:
            merge(blocks[b][0], blk[0], parts)
            parts = []


def _dil_attention(proj):
    def spec(col0):
        return pl.BlockSpec((SEQ, LANES), lambda b, hp: (b, col0 // LANES + hp))

    return pl.pallas_call(
        _dil_attn_kernel,
        grid=(BATCH, DIL_WIDTH // LANES),
        in_specs=[spec(DIL_Q_COL), spec(DIL_K_COL), spec(DIL_V_COL)],
        out_specs=pl.BlockSpec((SEQ, LANES), lambda b, hp: (b, hp)),
        out_shape=jax.ShapeDtypeStruct((TOKENS, DIL_WIDTH), F32),
        scratch_shapes=[pltpu.VMEM((SEQ, LANES), F32)] * 3,
        compiler_params=_cparams("parallel", "parallel"),
        name="dilated_attention",
    )(proj, proj, proj)


SLAB_GROUPS = LANES // SSM_GROUP
SSM_SLABS = SSM_GROUPS // SLAB_GROUPS
SSM_NCHUNK = SEQ // SSM_CHUNK
SSM_ROWS = BATCH * SSM_NCHUNK
SSM_ROW = SSM_CHUNK * LANES
SLAB_STATE = SLAB_GROUPS * SSM_STATE
STATE_TILES = SLAB_STATE // LANES
SUBLANES = 8


def _split_bf16(x):
    hi = x.astype(BF16)
    return hi, (x - hi.astype(F32)).astype(BF16)


def _ssm_kernel(u_ref, t0_ref, m1re_ref, m1im_ref, m2re_ref, m2im_ref,
                are_ref, aim_ref, d_ref, y_ref, fold_ref, st_ref):
    for b in range(BATCH):
        rows = slice(b * SSM_NCHUNK, (b + 1) * SSM_NCHUNK)
        for t in range(SSM_CHUNK):
            fold_ref[rows, t * LANES:(t + 1) * LANES] = (
                u_ref[pl.ds(b * SEQ + t, SSM_NCHUNK, stride=SSM_CHUNK), :])
    u = fold_ref[...]
    ub = u.astype(BF16)

    loc_re = jnp.dot(ub, m1re_ref[0], preferred_element_type=F32)
    loc_im = jnp.dot(ub, m1im_ref[0], preferred_element_type=F32)
    for b in range(BATCH):
        rows = slice(b * SSM_NCHUNK, (b + 1) * SSM_NCHUNK)
        for k in range(STATE_TILES):
            lanes = slice(k * LANES, (k + 1) * LANES)
            st_ref[k, pl.ds(b, SSM_NCHUNK, stride=SUBLANES), :] = loc_re[rows, lanes]
            st_ref[k, pl.ds(BATCH + b, SSM_NCHUNK, stride=SUBLANES), :] = loc_im[rows, lanes]

    a_re, a_im = are_ref[0], aim_ref[0]
    upper = lax.broadcasted_iota(jnp.int32, (SUBLANES, LANES), 0) < BATCH
    mul_same, mul_swap = [], []
    for k in range(STATE_TILES):
        lanes = slice(k * LANES, (k + 1) * LANES)
        mul_same.append(jnp.broadcast_to(a_re[:, lanes], (SUBLANES, LANES)))
        im = jnp.broadcast_to(a_im[:, lanes], (SUBLANES, LANES))
        mul_swap.append(jnp.where(upper, -im, im))

    def chunk_step(c, state):
        r0 = pl.multiple_of(c * SUBLANES, SUBLANES)
        new = []
        for k in range(STATE_TILES):
            loc = st_ref[k, pl.ds(r0, SUBLANES), :]
            st_ref[k, pl.ds(r0, SUBLANES), :] = state[k]
            swapped = pltpu.roll(state[k], BATCH, axis=0)
            new.append(mul_same[k] * state[k] + mul_swap[k] * swapped + loc)
        return tuple(new)

    zero = jnp.zeros((SUBLANES, LANES), F32)
    lax.fori_loop(0, SSM_NCHUNK, chunk_step, (zero,) * STATE_TILES, unroll=4)

    def entering(offset):
        return jnp.concatenate(
            [jnp.concatenate([st_ref[k, pl.ds(offset + b, SSM_NCHUNK, stride=SUBLANES), :]
                              for k in range(STATE_TILES)], axis=1)
             for b in range(BATCH)], axis=0)

    y = jnp.dot(ub, t0_ref[0], preferred_element_type=F32) + u * d_ref[0]
    for offset, m2_ref in ((0, m2re_ref), (BATCH, m2im_ref)):
        hi, lo = _split_bf16(entering(offset))
        m2 = m2_ref[0]
        y = y + (jnp.dot(hi, m2, preferred_element_type=F32)
                 + jnp.dot(lo, m2, preferred_element_type=F32))
    fold_ref[...] = jax.nn.gelu(y, approximate=True)

    for b in range(BATCH):
        rows = slice(b * SSM_NCHUNK, (b + 1) * SSM_NCHUNK)
        for t in range(SSM_CHUNK):
            y_ref[pl.ds(b * SEQ + t, SSM_NCHUNK, stride=SSM_CHUNK), :] = (
                fold_ref[rows, t * LANES:(t + 1) * LANES])


def _ssm_scan(proj, mats, layer):
    t0, m1re, m1im, m2re, m2im, a_re, a_im, dvec = mats

    def spec(r, c):
        return _layer_spec((1, r, c), lambda s: (layer, s, 0, 0))

    return pl.pallas_call(
        _ssm_kernel,
        grid=(SSM_SLABS,),
        in_specs=[pl.BlockSpec((TOKENS, LANES), lambda s: (0, SSM_U_COL // LANES + s)),
                  spec(SSM_ROW, SSM_ROW),
                  spec(SSM_ROW, SLAB_STATE), spec(SSM_ROW, SLAB_STATE),
                  spec(SLAB_STATE, SSM_ROW), spec(SLAB_STATE, SSM_ROW),
                  spec(1, SLAB_STATE), spec(1, SLAB_STATE), spec(1, SSM_ROW)],
        out_specs=pl.BlockSpec((TOKENS, LANES), lambda s: (0, s)),
        out_shape=jax.ShapeDtypeStruct((TOKENS, SSM_WIDTH), F32),
        scratch_shapes=[pltpu.VMEM((SSM_ROWS, SSM_ROW), F32),
                        pltpu.VMEM((STATE_TILES, SUBLANES * SSM_NCHUNK, LANES), F32)],
        compiler_params=_cparams("arbitrary"),
        name="ssm_chunk_scan",
    )(proj, t0, m1re, m1im, m2re, m2im, a_re, a_im, dvec)


def _ssm_matrices(a_re, a_im, b_re, b_im, c_re, c_im, d_skip, log_dt):
    lam_re = jnp.minimum(a_re, -1e-4)
    lam_im = a_im
    dt = jnp.exp(log_dt)[:, None]
    mag = jnp.exp(lam_re * dt)
    ab_re, ab_im = mag * jnp.cos(lam_im * dt), mag * jnp.sin(lam_im * dt)
    n_re, n_im = ab_re - 1.0, ab_im
    den = lam_re * lam_re + lam_im * lam_im
    f_re = (n_re * lam_re + n_im * lam_im) / den
    f_im = (n_im * lam_re - n_re * lam_im) / den
    bb_re = f_re[..., None] * b_re - f_im[..., None] * b_im
    bb_im = f_re[..., None] * b_im + f_im[..., None] * b_re
    p_re, p_im = [jnp.ones_like(ab_re)], [jnp.zeros_like(ab_im)]
    for _ in range(SSM_CHUNK):
        p_re.append(p_re[-1] * ab_re - p_im[-1] * ab_im)
        p_im.append(p_re[-2] * ab_im + p_im[-1] * ab_re)
    pw_re, pw_im = jnp.stack(p_re), jnp.stack(p_im)

    cb_re = (c_re[:, None, :, :] * bb_re.transpose(0, 2, 1)[:, :, None, :]
             - c_im[:, None, :, :] * bb_im.transpose(0, 2, 1)[:, :, None, :])
    cb_im = (c_re[:, None, :, :] * bb_im.transpose(0, 2, 1)[:, :, None, :]
             + c_im[:, None, :, :] * bb_re.transpose(0, 2, 1)[:, :, None, :])
    lag_re = pw_re[:SSM_CHUNK].transpose(1, 0, 2)[:, :, None, None, :]
    lag_im = pw_im[:SSM_CHUNK].transpose(1, 0, 2)[:, :, None, None, :]
    kern = jnp.sum(lag_re * cb_re[:, None] - lag_im * cb_im[:, None], axis=-1)
    t_idx = np.arange(SSM_CHUNK)

    back = pw_re[SSM_CHUNK - 1 - t_idx], pw_im[SSM_CHUNK - 1 - t_idx]
    bre_t, bim_t = bb_re.transpose(0, 2, 1), bb_im.transpose(0, 2, 1)
    m1re = (back[0].transpose(1, 0, 2)[:, :, None, :] * bre_t[:, None]
            - back[1].transpose(1, 0, 2)[:, :, None, :] * bim_t[:, None])
    m1im = (back[0].transpose(1, 0, 2)[:, :, None, :] * bim_t[:, None]
            + back[1].transpose(1, 0, 2)[:, :, None, :] * bre_t[:, None])

    fwd_re = pw_re[1:].transpose(1, 0, 2)[:, :, None, :]
    fwd_im = pw_im[1:].transpose(1, 0, 2)[:, :, None, :]
    ca_re = c_re[:, None] * fwd_re - c_im[:, None] * fwd_im
    ca_im = c_re[:, None] * fwd_im + c_im[:, None] * fwd_re

    def rows_by_group(x):
        k, r, c = x.shape[1:]
        x = x.reshape(SSM_SLABS, SLAB_GROUPS, k, r, c).transpose(0, 2, 1, 3, 4)
        return x.reshape(SSM_SLABS, k, SLAB_GROUPS * r, c)

    def block_diag(x):
        r, c = x.shape[2] // SLAB_GROUPS, x.shape[3]
        own = (np.arange(SLAB_GROUPS * r)[:, None] // r) == (np.arange(SLAB_GROUPS * c)[None, :] // c)
        return jnp.where(own, jnp.concatenate([x] * SLAB_GROUPS, axis=-1), 0.0).astype(BF16)

    d_lag = block_diag(rows_by_group(kern))
    none = jnp.zeros_like(d_lag[:, 0])
    t0 = jnp.concatenate(
        [jnp.concatenate([d_lag[:, s - t] if s >= t else none for s in range(SSM_CHUNK)], axis=-1)
         for t in range(SSM_CHUNK)], axis=-2)

    def slab_m1(x):
        return block_diag(rows_by_group(x)).reshape(SSM_SLABS, SSM_ROW, SLAB_STATE)

    def slab_m2(x):
        x = block_diag(rows_by_group(x.transpose(0, 1, 3, 2)))
        return x.transpose(0, 2, 1, 3).reshape(SSM_SLABS, SLAB_STATE, SSM_ROW)

    a_step_re = pw_re[SSM_CHUNK].reshape(SSM_SLABS, 1, SLAB_STATE)
    a_step_im = pw_im[SSM_CHUNK].reshape(SSM_SLABS, 1, SLAB_STATE)
    dvec = jnp.concatenate([d_skip.reshape(SSM_SLABS, 1, LANES)] * SSM_CHUNK, axis=-1)
    return (t0, slab_m1(m1re), slab_m1(m1im), slab_m2(ca_re), slab_m2(-ca_im),
            a_step_re, a_step_im, dvec)


def _glu_kernel(y_ref, w_ref, b_ref, o_ref):
    z = jnp.dot(y_ref[...].astype(BF16), _bf16(w_ref[...]),
                preferred_element_type=F32) + b_ref[...]
    o_ref[...] = z[:, :SSM_WIDTH] * jax.nn.sigmoid(z[:, SSM_WIDTH:])


def _glu(y, w_glu, b_glu, layer, *, bm):
    return pl.pallas_call(
        _glu_kernel,
        grid=(TOKENS // bm,),
        in_specs=[pl.BlockSpec((bm, SSM_WIDTH), lambda i: (i, 0)),
                  _layer_spec((SSM_WIDTH, 2 * SSM_WIDTH), lambda i: (layer, 0, 0)),
                  _layer_spec((1, 2 * SSM_WIDTH), lambda i: (layer, 0, 0))],
        out_specs=pl.BlockSpec((bm, SSM_WIDTH), lambda i: (i, 0)),
        out_shape=jax.ShapeDtypeStruct((TOKENS, SSM_WIDTH), F32),
        compiler_params=_cparams("parallel"),
        name="ssm_glu",
    )(y, w_glu, b_glu.reshape(DEPTH, 1, -1))


def _out_proj_kernel(ya_ref, yb_ref, yc_ref, ga_ref, gb_ref, gc_ref, w_ref, x_ref,
                     o_ref, h_ref):
    @pl.when(pl.program_id(1) == 0)
    def _():
        _norm_into(h_ref, ya_ref, ga_ref, 0)
        _norm_into(h_ref, yb_ref, gb_ref, MLA_WIDTH)
        _norm_into(h_ref, yc_ref, gc_ref, MLA_WIDTH + SSM_WIDTH)

    o_ref[...] = x_ref[...] + jnp.dot(h_ref[...], _bf16(w_ref[...]),
                                      preferred_element_type=F32)


def _out_proj(y_mla, y_ssm, y_dil, g_mla, g_ssm, g_dil, w_o, x, layer, *, bm, bn):
    def rows(width):
        return pl.BlockSpec((bm, width), lambda i, j: (i, 0))

    def gain(width):
        return _layer_spec((1, width), lambda i, j: (layer, 0, 0))

    return pl.pallas_call(
        _out_proj_kernel,
        grid=(TOKENS // bm, D_MODEL // bn),
        in_specs=[rows(MLA_WIDTH), rows(SSM_WIDTH), rows(DIL_WIDTH),
                  gain(MLA_WIDTH), gain(SSM_WIDTH), gain(DIL_WIDTH),
                  _layer_spec((D_MODEL, bn), lambda i, j: (layer, 0, j)),
                  pl.BlockSpec((bm, bn), lambda i, j: (i, j))],
        out_specs=pl.BlockSpec((bm, bn), lambda i, j: (i, j)),
        out_shape=jax.ShapeDtypeStruct((TOKENS, D_MODEL), F32),
        scratch_shapes=[pltpu.VMEM((bm, D_MODEL), BF16)],
        compiler_params=_cparams("parallel", "arbitrary"),
        name="out_proj",
    )(y_mla, y_ssm, y_dil, g_mla.reshape(DEPTH, 1, -1), g_ssm.reshape(DEPTH, 1, -1),
      g_dil.reshape(DEPTH, 1, -1), w_o, x)


def _ffn_up_kernel(x_ref, g_ref, wg_ref, wu_ref, o_ref, h_ref):
    @pl.when(pl.program_id(1) == 0)
    def _():
        _norm_into(h_ref, x_ref, g_ref)

    h = h_ref[...]
    gate = jnp.dot(h, _bf16(wg_ref[...]), preferred_element_type=F32)
    up = jnp.dot(h, _bf16(wu_ref[...]), preferred_element_type=F32)
    o_ref[...] = (jax.nn.silu(gate) * up).astype(o_ref.dtype)


def _ffn_up(x, g, w_gate, w_up, layer, *, bm, bn):
    return pl.pallas_call(
        _ffn_up_kernel,
        grid=(TOKENS // bm, D_FF // bn),
        in_specs=[pl.BlockSpec((bm, D_MODEL), lambda i, j: (i, 0)),
                  _layer_spec((1, D_MODEL), lambda i, j: (layer, 0, 0)),
                  _layer_spec((D_MODEL, bn), lambda i, j: (layer, 0, j)),
                  _layer_spec((D_MODEL, bn), lambda i, j: (layer, 0, j))],
        out_specs=pl.BlockSpec((bm, bn), lambda i, j: (i, j)),
        out_shape=jax.ShapeDtypeStruct((TOKENS, D_FF), BF16),
        scratch_shapes=[pltpu.VMEM((bm, D_MODEL), BF16)],
        compiler_params=_cparams("parallel", "arbitrary"),
        name="ffn_gate_up",
    )(x, g.reshape(DEPTH, 1, -1), w_gate, w_up)


def _ffn_down_kernel(a_ref, w_ref, x_ref, o_ref):
    o_ref[...] = x_ref[...] + jnp.dot(a_ref[...], _bf16(w_ref[...]),
                                      preferred_element_type=F32)


def _ffn_down(a, w_down, x, layer, *, bm, bn):
    return pl.pallas_call(
        _ffn_down_kernel,
        grid=(TOKENS // bm, D_MODEL // bn),
        in_specs=[pl.BlockSpec((bm, D_FF), lambda i, j: (i, 0)),
                  _layer_spec((D_FF, bn), lambda i, j: (layer, 0, j)),
                  pl.BlockSpec((bm, bn), lambda i, j: (i, j))],
        out_specs=pl.BlockSpec((bm, bn), lambda i, j: (i, j)),
        out_shape=jax.ShapeDtypeStruct((TOKENS, D_MODEL), F32),
        compiler_params=_cparams("parallel", "arbitrary"),
        name="ffn_down",
    )(a, w_down, x)


def _final_norm_kernel(x_ref, g_ref, o_ref):
    o_ref[...] = _rms_rows(x_ref[...], g_ref[...])


def _final_norm(x, g, *, bm):
    return pl.pallas_call(
        _final_norm_kernel,
        grid=(TOKENS // bm,),
        in_specs=[pl.BlockSpec((bm, D_MODEL), lambda i: (i, 0)),
                  pl.BlockSpec((1, D_MODEL), lambda i: (0, 0))],
        out_specs=pl.BlockSpec((bm, D_MODEL), lambda i: (i, 0)),
        out_shape=jax.ShapeDtypeStruct((TOKENS, D_MODEL), F32),
        compiler_params=_cparams("parallel"),
        name="final_norm",
    )(x, g.reshape(1, -1))


def _pad_w_in(w_in):
    lyr, d, _ = w_in.shape
    z = lambda n: jnp.zeros((lyr, d, n), BF16)
    w = w_in.astype(BF16)
    return jnp.concatenate([w[..., :768], z(128), w[..., 768:832], z(64), w[..., 832:]],
                           axis=-1)


def _pad_wt_uq(w_uq):
    lyr = w_uq.shape[0]
    w = w_uq.reshape(lyr, MLA_Q_LORA, MLA_HEADS, MLA_NOPE + MLA_ROPE)
    w = jnp.pad(w, ((0, 0), (0, 0), (0, 0), (0, MLA_QK_PAD - MLA_NOPE - MLA_ROPE)))
    w = w.reshape(lyr, MLA_Q_LORA, MLA_HEADS * MLA_QK_PAD)
    return w.transpose(0, 2, 1).astype(BF16)


def _split_w_ukv(w_ukv):
    lyr = w_ukv.shape[0]
    w = w_ukv.reshape(lyr, MLA_KV_LORA, MLA_HEADS, MLA_NOPE + MLA_V)
    wk = jnp.pad(w[..., :MLA_NOPE], ((0, 0), (0, 0), (0, 0), (0, MLA_QK_PAD - MLA_NOPE)))
    wk = wk.reshape(lyr, MLA_KV_LORA, -1).astype(BF16)
    wv = w[..., MLA_NOPE:].reshape(lyr, MLA_KV_LORA, -1)
    return wk, wv.transpose(0, 2, 1).astype(BF16)


def _rope_angles():
    inv_freq = ROPE_THETA ** (-jnp.arange(ROPE_HALF, dtype=F32) / ROPE_HALF)
    ang = jnp.arange(SEQ, dtype=F32)[:, None] * inv_freq[None, :]
    return jnp.cos(ang), jnp.sin(ang)


def _rope_tables_k():
    cos, sin = _rope_angles()
    one = jnp.ones((SEQ, MLA_NOPE), F32)
    z = lambda n: jnp.zeros((SEQ, n), F32)
    tail = MLA_QK_PAD - MLA_NOPE - MLA_ROPE
    cos_t = jnp.concatenate([one, cos, cos, z(tail)], axis=1)
    sin_a = jnp.concatenate([z(MLA_NOPE), -sin, z(ROPE_HALF), z(tail)], axis=1)
    sin_b = jnp.concatenate([z(MLA_NOPE), z(ROPE_HALF), sin, z(tail)], axis=1)
    return cos_t, sin_a, sin_b


def kernel(x, g_mix, w_in, g_q, w_uq, g_kv, w_ukv, a_re, a_im, b_re, b_im, c_re, c_im,
           d_skip, log_dt, w_glu, b_glu, g_out_mla, g_out_ssm, g_out_dil, w_o,
           g_ffn, w_gate, w_up, w_down, g_final):
    x = x.reshape(TOKENS, D_MODEL)
    w_in_p = _pad_w_in(w_in)
    wt_q = _pad_wt_uq(w_uq)
    w_k, wt_v = _split_w_ukv(w_ukv)
    q_scale = (MLA_NOPE + MLA_ROPE) ** -0.5 * math.log2(math.e)
    cos, sin = _rope_angles()
    q_cos_t, q_sin_t = (cos * q_scale).T, (sin * q_scale).T
    k_tabs = _rope_tables_k()
    ssm_mats = jax.vmap(_ssm_matrices)(a_re, a_im, b_re, b_im, c_re, c_im, d_skip, log_dt)

    for l in range(DEPTH):
        proj = _norm_matmul(x, g_mix, w_in_p, l, bm=1024, bn=1024, out_dtype=F32,
                            name="in_proj")
        qt = _q_up(proj, g_q, wt_q, q_cos_t, q_sin_t, l, q_scale, bm=512)
        k, vt = _kv_up(proj, g_kv, w_k, wt_v, k_tabs, l, bm=512)
        y_mla = _mla_attention(qt, k, vt)
        y_ssm = _glu(_ssm_scan(proj, ssm_mats, l), w_glu, b_glu, l, bm=1024)
        y_dil = _dil_attention(proj)
        x = _out_proj(y_mla, y_ssm, y_dil, g_out_mla, g_out_ssm, g_out_dil,
                      w_o, x, l, bm=1024, bn=512)
        act = _ffn_up(x, g_ffn, w_gate, w_up, l, bm=1024, bn=512)
        x = _ffn_down(act, w_down, x, l, bm=1024, bn=256)
    out = _final_norm(x, g_final, bm=512)
    return out.reshape(BATCH, SEQ, D_MODEL)
```

```python
import functools
import math

import jax
import jax.numpy as jnp
import numpy as np
from jax import lax
from jax.experimental import pallas as pl
from jax.experimental.pallas import tpu as pltpu

F32 = jnp.float32
BF16 = jnp.bfloat16

D_MODEL = 2048
BATCH = 4
SEQ = 2048
DEPTH = 4
TOKENS = BATCH * SEQ

MLA_HEADS = 8
MLA_NOPE = 128
MLA_ROPE = 64
MLA_V = 128
MLA_Q_LORA = 512
MLA_KV_LORA = 256
MLA_WIDTH = MLA_HEADS * MLA_V
MLA_QK_PAD = 256
ROPE_THETA = 10000.0
ROPE_HALF = MLA_ROPE // 2

SSM_WIDTH = 512
SSM_GROUP = 16
SSM_GROUPS = 32
SSM_STATE = 64
SSM_CHUNK = 8

DIL_WIDTH = 512
DIL_HEAD_DIM = 64
DIL_HEADS = 8
DIL_PATTERNS = ((128, 1), (512, 4), (2048, 16))
BLOCK = 128

IN_PAD = 3072
DIL_Q_COL = 1536
DIL_K_COL = 2048
DIL_V_COL = 2560
SSM_U_COL = 1024
D_FF = 5632
NORM_EPS = 1e-6

LANES = 128
VMEM_LIMIT_BYTES = 56 * 1024 * 1024

NT_DIMS = (((1,), (1,)), ((), ()))


def _cparams(*semantics):
    return pltpu.CompilerParams(dimension_semantics=semantics,
                                vmem_limit_bytes=VMEM_LIMIT_BYTES)


def _rms_rows(x, g):
    ms = jnp.mean(x * x, axis=-1, keepdims=True)
    return x * lax.rsqrt(ms + NORM_EPS) * g


NORM_CHUNK = 256


def _first_step_by_chunks(h_ref, normed_chunk, make_emit):
    @pl.when(pl.program_id(1) == 0)
    def _():
        emit = make_emit()
        for r in range(0, h_ref.shape[0], NORM_CHUNK):
            rows = slice(r, r + NORM_CHUNK)
            h = normed_chunk(rows)
            h_ref[rows, :] = h
            emit(rows, h)

    @pl.when(pl.program_id(1) != 0)
    def _():
        make_emit()(slice(None), h_ref[...])


def _bf16(w):
    return w if w.dtype == BF16 else w.astype(BF16)


def _layer_spec(shape, index_map):
    return pl.BlockSpec((None,) + tuple(shape), index_map)


def _norm_matmul_kernel(x_ref, g_ref, w_ref, o_ref, h_ref):
    g = g_ref[...]

    def make_emit():
        w = _bf16(w_ref[...])

        def emit(rows, h):
            o_ref[rows, :] = jnp.dot(h, w, preferred_element_type=F32).astype(o_ref.dtype)
        return emit

    _first_step_by_chunks(h_ref, lambda rows: _rms_rows(x_ref[rows, :], g).astype(BF16),
                          make_emit)


def _norm_matmul(x, g, w, layer, *, bm, bn, out_dtype, name):
    m, k = x.shape
    n = w.shape[2]
    return pl.pallas_call(
        _norm_matmul_kernel,
        grid=(m // bm, n // bn),
        in_specs=[pl.BlockSpec((bm, k), lambda i, j: (i, 0)),
                  _layer_spec((1, k), lambda i, j: (layer, 0, 0)),
                  _layer_spec((k, bn), lambda i, j: (layer, 0, j))],
        out_specs=pl.BlockSpec((bm, bn), lambda i, j: (i, j)),
        out_shape=jax.ShapeDtypeStruct((m, n), out_dtype),
        scratch_shapes=[pltpu.VMEM((bm, k), BF16)],
        compiler_params=_cparams("parallel", "arbitrary"),
        name=name,
    )(x, g.reshape(DEPTH, 1, k), w)


def _q_up_kernel(c_ref, g_ref, wt_ref, cos_ref, sin_ref, o_ref, *, scale):
    h = _rms_rows(c_ref[...], g_ref[...]).astype(BF16)
    qt = lax.dot_general(wt_ref[...], h, NT_DIMS, preferred_element_type=F32)
    cos, sin = cos_ref[...], sin_ref[...]
    for hd in range(MLA_HEADS):
        r0 = hd * MLA_QK_PAD
        r1, r2, r3 = r0 + MLA_NOPE, r0 + MLA_NOPE + ROPE_HALF, r0 + MLA_NOPE + MLA_ROPE
        x1, x2 = qt[r1:r2], qt[r2:r3]
        o_ref[r0:r1, :] = (qt[r0:r1] * scale).astype(o_ref.dtype)
        o_ref[r1:r2, :] = (x1 * cos - x2 * sin).astype(o_ref.dtype)
        o_ref[r2:r3, :] = (x2 * cos + x1 * sin).astype(o_ref.dtype)
        o_ref[r3:r0 + MLA_QK_PAD, :] = qt[r3:r0 + MLA_QK_PAD].astype(o_ref.dtype)


def _q_up(proj, g_q, wt_q, cos_t, sin_t, layer, scale, *, bm):
    pos_blocks = SEQ // bm
    tab_spec = pl.BlockSpec((ROPE_HALF, bm), lambda i: (0, i % pos_blocks))
    n = MLA_HEADS * MLA_QK_PAD
    return pl.pallas_call(
        functools.partial(_q_up_kernel, scale=scale),
        grid=(TOKENS // bm,),
        in_specs=[pl.BlockSpec((bm, MLA_Q_LORA), lambda i: (i, 0)),
                  _layer_spec((1, MLA_Q_LORA), lambda i: (layer, 0, 0)),
                  _layer_spec((n, MLA_Q_LORA), lambda i: (layer, 0, 0)),
                  tab_spec, tab_spec],
        out_specs=pl.BlockSpec((n, bm), lambda i: (0, i)),
        out_shape=jax.ShapeDtypeStruct((n, TOKENS), BF16),
        compiler_params=_cparams("parallel"),
        name="mla_q_up",
    )(proj, g_q.reshape(DEPTH, 1, -1), wt_q, cos_t, sin_t)


def _rope_pad(x, cos_t, sin_a, sin_b):
    up = pltpu.roll(x, MLA_QK_PAD - ROPE_HALF, axis=1)
    dn = pltpu.roll(x, ROPE_HALF, axis=1)
    return x * cos_t + up * sin_a + dn * sin_b


def _kv_up_kernel(c_ref, kr_ref, g_ref, wk_ref, wvt_ref, cos_ref, sina_ref, sinb_ref,
                  k_ref, vt_ref):
    h = _rms_rows(c_ref[...], g_ref[...]).astype(BF16)
    kn = jnp.dot(h, wk_ref[...], preferred_element_type=F32)
    k_pe = _rope_pad(kr_ref[...], cos_ref[...], sina_ref[...], sinb_ref[...])
    for hd in range(MLA_HEADS):
        sl = slice(hd * MLA_QK_PAD, (hd + 1) * MLA_QK_PAD)
        k_ref[:, sl] = (kn[:, sl] + k_pe).astype(k_ref.dtype)
    vt_ref[...] = lax.dot_general(wvt_ref[...], h, NT_DIMS,
                                  preferred_element_type=F32).astype(vt_ref.dtype)


def _kv_up(proj, g_kv, w_k, wt_v, tabs, layer, *, bm):
    pos_blocks = SEQ // bm
    tab_spec = pl.BlockSpec((bm, MLA_QK_PAD), lambda i: (i % pos_blocks, 0))
    nk = MLA_HEADS * MLA_QK_PAD
    return pl.pallas_call(
        _kv_up_kernel,
        grid=(TOKENS // bm,),
        in_specs=[pl.BlockSpec((bm, MLA_KV_LORA), lambda i: (i, 2)),
                  pl.BlockSpec((bm, MLA_QK_PAD), lambda i: (i, 3)),
                  _layer_spec((1, MLA_KV_LORA), lambda i: (layer, 0, 0)),
                  _layer_spec((MLA_KV_LORA, nk), lambda i: (layer, 0, 0)),
                  _layer_spec((MLA_WIDTH, MLA_KV_LORA), lambda i: (layer, 0, 0)),
                  tab_spec, tab_spec, tab_spec],
        out_specs=[pl.BlockSpec((bm, nk), lambda i: (i, 0)),
                   pl.BlockSpec((MLA_WIDTH, bm), lambda i: (0, i))],
        out_shape=[jax.ShapeDtypeStruct((TOKENS, nk), BF16),
                   jax.ShapeDtypeStruct((MLA_WIDTH, TOKENS), BF16)],
        compiler_params=_cparams("parallel"),
        name="mla_kv_up",
    )(proj, proj, g_kv.reshape(DEPTH, 1, -1), w_k, wt_v, *tabs)


ATT_BQ = 256
ATT_BK = 256
ATT_NQ = SEQ // ATT_BQ


def _mla_attn_kernel(qt_ref, k_ref, vt_ref, o_ref, m_sc, l_sc, acc_sc):
    key = lax.broadcasted_iota(jnp.int32, (ATT_BK, ATT_BQ), 0)
    qry = lax.broadcasted_iota(jnp.int32, (ATT_BK, ATT_BQ), 1)
    causal = key <= qry
    tiles = [(i, j) for j in range(ATT_NQ) for i in range(j, ATT_NQ)]

    def scores(i, j):
        kj = k_ref[j * ATT_BK:(j + 1) * ATT_BK, :]
        qi = qt_ref[:, i * ATT_BQ:(i + 1) * ATT_BQ]
        return jnp.dot(kj, qi, preferred_element_type=F32)

    def softmax(i, j, s):
        if i == j:
            s = jnp.where(causal, s, -jnp.inf)
        m_blk = jnp.max(s, axis=0, keepdims=True)
        if j == 0:
            m_new, alpha = m_blk, None
            p = jnp.exp2(s - m_new)
            l = jnp.sum(p, axis=0, keepdims=True)
        else:
            m_old = m_sc[i]
            m_new = jnp.maximum(m_old, m_blk)
            alpha = jnp.exp2(m_old - m_new)
            p = jnp.exp2(s - m_new)
            l = alpha * l_sc[i] + jnp.sum(p, axis=0, keepdims=True)
        if i != j:
            m_sc[i] = m_new
            l_sc[i] = l
        return p.astype(BF16), alpha, l

    def values(i, j, p, alpha, l):
        vj = vt_ref[:, j * ATT_BK:(j + 1) * ATT_BK]
        acc = jnp.dot(vj, p, preferred_element_type=F32)
        if alpha is not None:
            acc = alpha * acc_sc[i] + acc
        if i == j:
            o_ref[i * ATT_BQ:(i + 1) * ATT_BQ, :] = (acc / l).T.astype(o_ref.dtype)
        else:
            acc_sc[i] = acc

    n = len(tiles)
    s_ready = {0: scores(*tiles[0])}
    if n > 1:
        s_ready[1] = scores(*tiles[1])
    p_ready = {0: softmax(*tiles[0], s_ready.pop(0))}
    for t in range(n):
        if t + 2 < n:
            s_ready[t + 2] = scores(*tiles[t + 2])
        if t + 1 < n:
            p_ready[t + 1] = softmax(*tiles[t + 1], s_ready.pop(t + 1))
        values(*tiles[t], *p_ready.pop(t))


def _mla_attention(qt, k, vt):
    return pl.pallas_call(
        _mla_attn_kernel,
        grid=(BATCH, MLA_HEADS),
        in_specs=[pl.BlockSpec((MLA_QK_PAD, SEQ), lambda b, h: (h, b)),
                  pl.BlockSpec((SEQ, MLA_QK_PAD), lambda b, h: (b, h)),
                  pl.BlockSpec((MLA_V, SEQ), lambda b, h: (h, b))],
        out_specs=pl.BlockSpec((SEQ, MLA_V), lambda b, h: (b, h)),
        out_shape=jax.ShapeDtypeStruct((TOKENS, MLA_WIDTH), BF16),
        scratch_shapes=[pltpu.VMEM((ATT_NQ, 1, ATT_BQ), F32),
                        pltpu.VMEM((ATT_NQ, 1, ATT_BQ), F32),
                        pltpu.VMEM((ATT_NQ, MLA_V, ATT_BQ), F32)],
        compiler_params=_cparams("parallel", "parallel"),
        name="mla_attention",
    )(qt, k, vt)


def _dil_attn_kernel(q_ref, k_ref, v_ref, o_ref, m_sc, l_sc, n_sc):
    row2 = lax.broadcasted_iota(jnp.int32, (BLOCK, 2 * BLOCK), 0)
    col2 = lax.broadcasted_iota(jnp.int32, (BLOCK, 2 * BLOCK), 1)
    dist = row2 + BLOCK - col2
    band = (dist >= 0) & (dist <= BLOCK)
    row1 = lax.broadcasted_iota(jnp.int32, (BLOCK, BLOCK), 0)
    col1 = lax.broadcasted_iota(jnp.int32, (BLOCK, BLOCK), 1)
    tri = row1 >= col1
    q_scale = DIL_HEAD_DIM ** -0.5 * math.log2(math.e)
    heads = LANES // DIL_HEAD_DIM
    n_patterns = len(DIL_PATTERNS)

    def rows_at(start, dil):
        return pl.ds(start, BLOCK) if dil == 1 else pl.ds(start, BLOCK, stride=dil)

    blocks = [(pi, dil, r, n) for pi, (_, dil) in enumerate(reversed(DIL_PATTERNS))
              for r in range(dil) for n in range(SEQ // dil // BLOCK)]

    def load(pi, dil, r, n):
        rows = rows_at(r + dil * BLOCK * n, dil)
        q = (q_ref[rows, :] * q_scale).astype(BF16)
        if n == 0:
            return rows, q, k_ref[rows, :].astype(BF16), v_ref[rows, :].astype(BF16), tri
        prev = rows_at(r + dil * BLOCK * (n - 1), dil)
        kk = jnp.concatenate([k_ref[prev, :], k_ref[rows, :]], axis=0).astype(BF16)
        vv = jnp.concatenate([v_ref[prev, :], v_ref[rows, :]], axis=0).astype(BF16)
        return rows, q, kk, vv, band

    def scores(blk, hd):
        _, q, kk, _, _ = blk
        sl = slice(hd * DIL_HEAD_DIM, (hd + 1) * DIL_HEAD_DIM)
        return lax.dot_general(q[:, sl], kk[:, sl], NT_DIMS, preferred_element_type=F32)

    def softmax_pv(blk, hd, s):
        _, _, _, vv, mask = blk
        sl = slice(hd * DIL_HEAD_DIM, (hd + 1) * DIL_HEAD_DIM)
        s = jnp.where(mask, s, -jnp.inf)
        m = jnp.max(s, axis=-1, keepdims=True)
        p = jnp.exp2(s - m)
        l = jnp.sum(p, axis=-1, keepdims=True)
        acc = jnp.dot(p.astype(BF16), vv[:, sl], preferred_element_type=F32)
        return (jnp.broadcast_to(m, (BLOCK, DIL_HEAD_DIM)),
                jnp.broadcast_to(l, (BLOCK, DIL_HEAD_DIM)), acc)

    def merge(pi, rows, parts):
        m2, l2, a2 = (jnp.concatenate([p[c] for p in parts], axis=-1) for c in range(3))
        if pi > 0:
            m_old = m_sc[rows, :]
            m_new = jnp.maximum(m_old, m2)
            w_old, w_new = jnp.exp2(m_old - m_new), jnp.exp2(m2 - m_new)
            l2 = w_old * l_sc[rows, :] + w_new * l2
            a2 = w_old * n_sc[rows, :] + w_new * a2
            m2 = m_new
        if pi == n_patterns - 1:
            o_ref[rows, :] = a2 / l2
        else:
            m_sc[rows, :] = m2
            l_sc[rows, :] = l2
            n_sc[rows, :] = a2

    items = [(b, hd) for b in range(len(blocks)) for hd in range(heads)]
    blk_next = load(*blocks[0])
    s_next = scores(blk_next, 0)
    parts = []
    for t, (b, hd) in enumerate(items):
        blk, s_cur = blk_next, s_next
        if t + 1 < len(items):
            nb_, nhd = items[t + 1]
            if nb_ != b:
                blk_next = load(*blocks[nb_])
            s_next = scores(blk_next, nhd)
        parts.append(softmax_pv(blk, hd, s_cur))
        if hd == heads - 1:
            merge(blocks[b][0], blk[0], parts)
            parts = []


def _dil_attention(proj):
    def spec(col0):
        return pl.BlockSpec((SEQ, LANES), lambda b, hp: (b, col0 // LANES + hp))

    return pl.pallas_call(
        _dil_attn_kernel,
        grid=(BATCH, DIL_WIDTH // LANES),
        in_specs=[spec(DIL_Q_COL), spec(DIL_K_COL), spec(DIL_V_COL)],
        out_specs=pl.BlockSpec((SEQ, LANES), lambda b, hp: (b, hp)),
        out_shape=jax.ShapeDtypeStruct((TOKENS, DIL_WIDTH), F32),
        scratch_shapes=[pltpu.VMEM((SEQ, LANES), F32)] * 3,
        compiler_params=_cparams("parallel", "parallel"),
        name="dilated_attention",
    )(proj, proj, proj)


SLAB_GROUPS = LANES // SSM_GROUP
SSM_SLABS = SSM_GROUPS // SLAB_GROUPS
SSM_NCHUNK = SEQ // SSM_CHUNK
SSM_ROWS = BATCH * SSM_NCHUNK
SSM_ROW = SSM_CHUNK * LANES
SLAB_STATE = SLAB_GROUPS * SSM_STATE
STATE_TILES = SLAB_STATE // LANES
SUBLANES = 8


def _split_bf16(x):
    hi = x.astype(BF16)
    return hi, (x - hi.astype(F32)).astype(BF16)


def _ssm_kernel(u_ref, t0_ref, m1re_ref, m1im_ref, m2re_ref, m2im_ref,
                are_ref, aim_ref, d_ref, y_ref, fold_ref, st_ref):
    for b in range(BATCH):
        rows = slice(b * SSM_NCHUNK, (b + 1) * SSM_NCHUNK)
        for t in range(SSM_CHUNK):
            fold_ref[rows, t * LANES:(t + 1) * LANES] = (
                u_ref[pl.ds(b * SEQ + t, SSM_NCHUNK, stride=SSM_CHUNK), :])
    u = fold_ref[...]
    ub = u.astype(BF16)

    loc_re = jnp.dot(ub, m1re_ref[0], preferred_element_type=F32)
    loc_im = jnp.dot(ub, m1im_ref[0], preferred_element_type=F32)
    for b in range(BATCH):
        rows = slice(b * SSM_NCHUNK, (b + 1) * SSM_NCHUNK)
        for k in range(STATE_TILES):
            lanes = slice(k * LANES, (k + 1) * LANES)
            st_ref[k, pl.ds(b, SSM_NCHUNK, stride=SUBLANES), :] = loc_re[rows, lanes]
            st_ref[k, pl.ds(BATCH + b, SSM_NCHUNK, stride=SUBLANES), :] = loc_im[rows, lanes]

    a_re, a_im = are_ref[0], aim_ref[0]
    upper = lax.broadcasted_iota(jnp.int32, (SUBLANES, LANES), 0) < BATCH
    mul_same, mul_swap = [], []
    for k in range(STATE_TILES):
        lanes = slice(k * LANES, (k + 1) * LANES)
        mul_same.append(jnp.broadcast_to(a_re[:, lanes], (SUBLANES, LANES)))
        im = jnp.broadcast_to(a_im[:, lanes], (SUBLANES, LANES))
        mul_swap.append(jnp.where(upper, -im, im))

    def chunk_step(c, state):
        r0 = pl.multiple_of(c * SUBLANES, SUBLANES)
        new = []
        for k in range(STATE_TILES):
            loc = st_ref[k, pl.ds(r0, SUBLANES), :]
            st_ref[k, pl.ds(r0, SUBLANES), :] = state[k]
            swapped = pltpu.roll(state[k], BATCH, axis=0)
            new.append(mul_same[k] * state[k] + mul_swap[k] * swapped + loc)
        return tuple(new)

    zero = jnp.zeros((SUBLANES, LANES), F32)
    lax.fori_loop(0, SSM_NCHUNK, chunk_step, (zero,) * STATE_TILES, unroll=4)

    def entering(offset):
        return jnp.concatenate(
            [jnp.concatenate([st_ref[k, pl.ds(offset + b, SSM_NCHUNK, stride=SUBLANES), :]
                              for k in range(STATE_TILES)], axis=1)
             for b in range(BATCH)], axis=0)

    y = jnp.dot(ub, t0_ref[0], preferred_element_type=F32) + u * d_ref[0]
    for offset, m2t_ref in ((0, m2re_ref), (BATCH, m2im_ref)):
        hi, lo = _split_bf16(entering(offset))
        m2t = m2t_ref[0]
        y = y + (lax.dot_general(hi, m2t, NT_DIMS, preferred_element_type=F32)
                 + lax.dot_general(lo, m2t, NT_DIMS, preferred_element_type=F32))
    fold_ref[...] = jax.nn.gelu(y, approximate=True)

    for b in range(BATCH):
        rows = slice(b * SSM_NCHUNK, (b + 1) * SSM_NCHUNK)
        for t in range(SSM_CHUNK):
            y_ref[pl.ds(b * SEQ + t, SSM_NCHUNK, stride=SSM_CHUNK), :] = (
                fold_ref[rows, t * LANES:(t + 1) * LANES])


def _ssm_scan(proj, mats, layer):
    t0, m1re, m1im, m2re, m2im, a_re, a_im, dvec = mats

    def spec(r, c):
        return _layer_spec((1, r, c), lambda s: (layer, s, 0, 0))

    return pl.pallas_call(
        _ssm_kernel,
        grid=(SSM_SLABS,),
        in_specs=[pl.BlockSpec((TOKENS, LANES), lambda s: (0, SSM_U_COL // LANES + s)),
                  spec(SSM_ROW, SSM_ROW),
                  spec(SSM_ROW, SLAB_STATE), spec(SSM_ROW, SLAB_STATE),
                  spec(SSM_ROW, SLAB_STATE), spec(SSM_ROW, SLAB_STATE),
                  spec(1, SLAB_STATE), spec(1, SLAB_STATE), spec(1, SSM_ROW)],
        out_specs=pl.BlockSpec((TOKENS, LANES), lambda s: (0, s)),
        out_shape=jax.ShapeDtypeStruct((TOKENS, SSM_WIDTH), F32),
        scratch_shapes=[pltpu.VMEM((SSM_ROWS, SSM_ROW), F32),
                        pltpu.VMEM((STATE_TILES, SUBLANES * SSM_NCHUNK, LANES), F32)],
        compiler_params=_cparams("arbitrary"),
        name="ssm_chunk_scan",
    )(proj, t0, m1re, m1im, m2re, m2im, a_re, a_im, dvec)


def _ssm_matrices(a_re, a_im, b_re, b_im, c_re, c_im, d_skip, log_dt):
    lam_re = jnp.minimum(a_re, -1e-4)
    lam_im = a_im
    dt = jnp.exp(log_dt)[:, None]
    mag = jnp.exp(lam_re * dt)
    ab_re, ab_im = mag * jnp.cos(lam_im * dt), mag * jnp.sin(lam_im * dt)
    n_re, n_im = ab_re - 1.0, ab_im
    den = lam_re * lam_re + lam_im * lam_im
    f_re = (n_re * lam_re + n_im * lam_im) / den
    f_im = (n_im * lam_re - n_re * lam_im) / den
    bb_re = f_re[..., None] * b_re - f_im[..., None] * b_im
    bb_im = f_re[..., None] * b_im + f_im[..., None] * b_re
    p_re, p_im = [jnp.ones_like(ab_re)], [jnp.zeros_like(ab_im)]
    for _ in range(SSM_CHUNK):
        p_re.append(p_re[-1] * ab_re - p_im[-1] * ab_im)
        p_im.append(p_re[-2] * ab_im + p_im[-1] * ab_re)
    pw_re, pw_im = jnp.stack(p_re), jnp.stack(p_im)

    cb_re = (c_re[:, None, :, :] * bb_re.transpose(0, 2, 1)[:, :, None, :]
             - c_im[:, None, :, :] * bb_im.transpose(0, 2, 1)[:, :, None, :])
    cb_im = (c_re[:, None, :, :] * bb_im.transpose(0, 2, 1)[:, :, None, :]
             + c_im[:, None, :, :] * bb_re.transpose(0, 2, 1)[:, :, None, :])
    lag_re = pw_re[:SSM_CHUNK].transpose(1, 0, 2)[:, :, None, None, :]
    lag_im = pw_im[:SSM_CHUNK].transpose(1, 0, 2)[:, :, None, None, :]
    kern = jnp.sum(lag_re * cb_re[:, None] - lag_im * cb_im[:, None], axis=-1)
    t_idx = np.arange(SSM_CHUNK)

    back = pw_re[SSM_CHUNK - 1 - t_idx], pw_im[SSM_CHUNK - 1 - t_idx]
    bre_t, bim_t = bb_re.transpose(0, 2, 1), bb_im.transpose(0, 2, 1)
    m1re = (back[0].transpose(1, 0, 2)[:, :, None, :] * bre_t[:, None]
            - back[1].transpose(1, 0, 2)[:, :, None, :] * bim_t[:, None])
    m1im = (back[0].transpose(1, 0, 2)[:, :, None, :] * bim_t[:, None]
            + back[1].transpose(1, 0, 2)[:, :, None, :] * bre_t[:, None])

    fwd_re = pw_re[1:].transpose(1, 0, 2)[:, :, None, :]
    fwd_im = pw_im[1:].transpose(1, 0, 2)[:, :, None, :]
    ca_re = c_re[:, None] * fwd_re - c_im[:, None] * fwd_im
    ca_im = c_re[:, None] * fwd_im + c_im[:, None] * fwd_re

    def rows_by_group(x):
        k, r, c = x.shape[1:]
        x = x.reshape(SSM_SLABS, SLAB_GROUPS, k, r, c).transpose(0, 2, 1, 3, 4)
        return x.reshape(SSM_SLABS, k, SLAB_GROUPS * r, c)

    def block_diag(x):
        r, c = x.shape[2] // SLAB_GROUPS, x.shape[3]
        own = (np.arange(SLAB_GROUPS * r)[:, None] // r) == (np.arange(SLAB_GROUPS * c)[None, :] // c)
        return jnp.where(own, jnp.concatenate([x] * SLAB_GROUPS, axis=-1), 0.0).astype(BF16)

    d_lag = block_diag(rows_by_group(kern))
    none = jnp.zeros_like(d_lag[:, 0])
    t0 = jnp.concatenate(
        [jnp.concatenate([d_lag[:, s - t] if s >= t else none for s in range(SSM_CHUNK)], axis=-1)
         for t in range(SSM_CHUNK)], axis=-2)

    def slab_state_op(x):
        return block_diag(rows_by_group(x)).reshape(SSM_SLABS, SSM_ROW, SLAB_STATE)

    a_step_re = pw_re[SSM_CHUNK].reshape(SSM_SLABS, 1, SLAB_STATE)
    a_step_im = pw_im[SSM_CHUNK].reshape(SSM_SLABS, 1, SLAB_STATE)
    dvec = jnp.concatenate([d_skip.reshape(SSM_SLABS, 1, LANES)] * SSM_CHUNK, axis=-1)
    return (t0, slab_state_op(m1re), slab_state_op(m1im),
            slab_state_op(ca_re), slab_state_op(-ca_im), a_step_re, a_step_im, dvec)


def _glu_kernel(y_ref, w_ref, b_ref, o_ref):
    z = jnp.dot(y_ref[...].astype(BF16), _bf16(w_ref[...]),
                preferred_element_type=F32) + b_ref[...]
    o_ref[...] = (z[:, :SSM_WIDTH] * jax.nn.sigmoid(z[:, SSM_WIDTH:])).astype(o_ref.dtype)


def _glu(y, w_glu, b_glu, layer, *, bm):
    return pl.pallas_call(
        _glu_kernel,
        grid=(TOKENS // bm,),
        in_specs=[pl.BlockSpec((bm, SSM_WIDTH), lambda i: (i, 0)),
                  _layer_spec((SSM_WIDTH, 2 * SSM_WIDTH), lambda i: (layer, 0, 0)),
                  _layer_spec((1, 2 * SSM_WIDTH), lambda i: (layer, 0, 0))],
        out_specs=pl.BlockSpec((bm, SSM_WIDTH), lambda i: (i, 0)),
        out_shape=jax.ShapeDtypeStruct((TOKENS, SSM_WIDTH), BF16),
        compiler_params=_cparams("parallel"),
        name="ssm_glu",
    )(y, w_glu, b_glu.reshape(DEPTH, 1, -1))


def _out_proj_kernel(ya_ref, yb_ref, yc_ref, ga_ref, gb_ref, gc_ref, w_ref, x_ref,
                     o_ref, h_ref):
    parts = ((ya_ref, ga_ref[...]), (yb_ref, gb_ref[...]), (yc_ref, gc_ref[...]))

    def normed_chunk(rows):
        return jnp.concatenate([_rms_rows(y_ref[rows, :].astype(F32), g).astype(BF16)
                                for y_ref, g in parts], axis=-1)

    def make_emit():
        w = _bf16(w_ref[...])

        def emit(rows, h):
            o_ref[rows, :] = x_ref[rows, :] + jnp.dot(h, w, preferred_element_type=F32)
        return emit

    _first_step_by_chunks(h_ref, normed_chunk, make_emit)


def _out_proj(y_mla, y_ssm, y_dil, g_mla, g_ssm, g_dil, w_o, x, layer, *, bm, bn):
    def rows(width):
        return pl.BlockSpec((bm, width), lambda i, j: (i, 0))

    def gain(width):
        return _layer_spec((1, width), lambda i, j: (layer, 0, 0))

    return pl.pallas_call(
        _out_proj_kernel,
        grid=(TOKENS // bm, D_MODEL // bn),
        in_specs=[rows(MLA_WIDTH), rows(SSM_WIDTH), rows(DIL_WIDTH),
                  gain(MLA_WIDTH), gain(SSM_WIDTH), gain(DIL_WIDTH),
                  _layer_spec((D_MODEL, bn), lambda i, j: (layer, 0, j)),
                  pl.BlockSpec((bm, bn), lambda i, j: (i, j))],
        out_specs=pl.BlockSpec((bm, bn), lambda i, j: (i, j)),
        out_shape=jax.ShapeDtypeStruct((TOKENS, D_MODEL), F32),
        scratch_shapes=[pltpu.VMEM((bm, D_MODEL), BF16)],
        compiler_params=_cparams("parallel", "arbitrary"),
        name="out_proj",
    )(y_mla, y_ssm, y_dil, g_mla.reshape(DEPTH, 1, -1), g_ssm.reshape(DEPTH, 1, -1),
      g_dil.reshape(DEPTH, 1, -1), w_o, x)


def _ffn_up_kernel(x_ref, g_ref, wg_ref, wu_ref, o_ref, h_ref):
    g = g_ref[...]

    def make_emit():
        wg, wu = _bf16(wg_ref[...]), _bf16(wu_ref[...])

        def emit(rows, h):
            gate = jnp.dot(h, wg, preferred_element_type=F32)
            up = jnp.dot(h, wu, preferred_element_type=F32)
            o_ref[rows, :] = (jax.nn.silu(gate) * up).astype(o_ref.dtype)
        return emit

    _first_step_by_chunks(h_ref, lambda rows: _rms_rows(x_ref[rows, :], g).astype(BF16),
                          make_emit)


def _ffn_up(x, g, w_gate, w_up, layer, *, bm, bn):
    return pl.pallas_call(
        _ffn_up_kernel,
        grid=(TOKENS // bm, D_FF // bn),
        in_specs=[pl.BlockSpec((bm, D_MODEL), lambda i, j: (i, 0)),
                  _layer_spec((1, D_MODEL), lambda i, j: (layer, 0, 0)),
                  _layer_spec((D_MODEL, bn), lambda i, j: (layer, 0, j)),
                  _layer_spec((D_MODEL, bn), lambda i, j: (layer, 0, j))],
        out_specs=pl.BlockSpec((bm, bn), lambda i, j: (i, j)),
        out_shape=jax.ShapeDtypeStruct((TOKENS, D_FF), BF16),
        scratch_shapes=[pltpu.VMEM((bm, D_MODEL), BF16)],
        compiler_params=_cparams("parallel", "arbitrary"),
        name="ffn_gate_up",
    )(x, g.reshape(DEPTH, 1, -1), w_gate, w_up)


def _ffn_down_kernel(a_ref, w_ref, x_ref, o_ref):
    o_ref[...] = x_ref[...] + jnp.dot(a_ref[...], _bf16(w_ref[...]),
                                      preferred_element_type=F32)


def _ffn_down(a, w_down, x, layer, *, bm, bn):
    return pl.pallas_call(
        _ffn_down_kernel,
        grid=(TOKENS // bm, D_MODEL // bn),
        in_specs=[pl.BlockSpec((bm, D_FF), lambda i, j: (i, 0)),
                  _layer_spec((D_FF, bn), lambda i, j: (layer, 0, j)),
                  pl.BlockSpec((bm, bn), lambda i, j: (i, j))],
        out_specs=pl.BlockSpec((bm, bn), lambda i, j: (i, j)),
        out_shape=jax.ShapeDtypeStruct((TOKENS, D_MODEL), F32),
        compiler_params=_cparams("parallel", "arbitrary"),
        name="ffn_down",
    )(a, w_down, x)


def _final_norm_kernel(x_ref, g_ref, o_ref):
    o_ref[...] = _rms_rows(x_ref[...], g_ref[...])


def _final_norm(x, g, *, bm):
    return pl.pallas_call(
        _final_norm_kernel,
        grid=(TOKENS // bm,),
        in_specs=[pl.BlockSpec((bm, D_MODEL), lambda i: (i, 0)),
                  pl.BlockSpec((1, D_MODEL), lambda i: (0, 0))],
        out_specs=pl.BlockSpec((bm, D_MODEL), lambda i: (i, 0)),
        out_shape=jax.ShapeDtypeStruct((TOKENS, D_MODEL), F32),
        compiler_params=_cparams("parallel"),
        name="final_norm",
    )(x, g.reshape(1, -1))


def _pad_w_in(w_in):
    lyr, d, _ = w_in.shape
    z = lambda n: jnp.zeros((lyr, d, n), BF16)
    w = w_in.astype(BF16)
    return jnp.concatenate([w[..., :768], z(128), w[..., 768:832], z(64), w[..., 832:]],
                           axis=-1)


def _pad_wt_uq(w_uq):
    lyr = w_uq.shape[0]
    w = w_uq.reshape(lyr, MLA_Q_LORA, MLA_HEADS, MLA_NOPE + MLA_ROPE)
    w = jnp.pad(w, ((0, 0), (0, 0), (0, 0), (0, MLA_QK_PAD - MLA_NOPE - MLA_ROPE)))
    w = w.reshape(lyr, MLA_Q_LORA, MLA_HEADS * MLA_QK_PAD)
    return w.transpose(0, 2, 1).astype(BF16)


def _split_w_ukv(w_ukv):
    lyr = w_ukv.shape[0]
    w = w_ukv.reshape(lyr, MLA_KV_LORA, MLA_HEADS, MLA_NOPE + MLA_V)
    wk = jnp.pad(w[..., :MLA_NOPE], ((0, 0), (0, 0), (0, 0), (0, MLA_QK_PAD - MLA_NOPE)))
    wk = wk.reshape(lyr, MLA_KV_LORA, -1).astype(BF16)
    wv = w[..., MLA_NOPE:].reshape(lyr, MLA_KV_LORA, -1)
    return wk, wv.transpose(0, 2, 1).astype(BF16)


def _rope_angles():
    inv_freq = ROPE_THETA ** (-jnp.arange(ROPE_HALF, dtype=F32) / ROPE_HALF)
    ang = jnp.arange(SEQ, dtype=F32)[:, None] * inv_freq[None, :]
    return jnp.cos(ang), jnp.sin(ang)


def _rope_tables_k():
    cos, sin = _rope_angles()
    one = jnp.ones((SEQ, MLA_NOPE), F32)
    z = lambda n: jnp.zeros((SEQ, n), F32)
    tail = MLA_QK_PAD - MLA_NOPE - MLA_ROPE
    cos_t = jnp.concatenate([one, cos, cos, z(tail)], axis=1)
    sin_a = jnp.concatenate([z(MLA_NOPE), -sin, z(ROPE_HALF), z(tail)], axis=1)
    sin_b = jnp.concatenate([z(MLA_NOPE), z(ROPE_HALF), sin, z(tail)], axis=1)
    return cos_t, sin_a, sin_b


def kernel(x, g_mix, w_in, g_q, w_uq, g_kv, w_ukv, a_re, a_im, b_re, b_im, c_re, c_im,
           d_skip, log_dt, w_glu, b_glu, g_out_mla, g_out_ssm, g_out_dil, w_o,
           g_ffn, w_gate, w_up, w_down, g_final):
    x = x.reshape(TOKENS, D_MODEL)
    w_in_p = _pad_w_in(w_in)
    wt_q = _pad_wt_uq(w_uq)
    w_k, wt_v = _split_w_ukv(w_ukv)
    w_o_b = w_o.astype(BF16)
    q_scale = (MLA_NOPE + MLA_ROPE) ** -0.5 * math.log2(math.e)
    cos, sin = _rope_angles()
    q_cos_t, q_sin_t = (cos * q_scale).T, (sin * q_scale).T
    k_tabs = _rope_tables_k()
    ssm_mats = jax.vmap(_ssm_matrices)(a_re, a_im, b_re, b_im, c_re, c_im, d_skip, log_dt)

    for l in range(DEPTH):
        proj = _norm_matmul(x, g_mix, w_in_p, l, bm=1024, bn=1024, out_dtype=F32,
                            name="in_proj")
        qt = _q_up(proj, g_q, wt_q, q_cos_t, q_sin_t, l, q_scale, bm=512)
        k, vt = _kv_up(proj, g_kv, w_k, wt_v, k_tabs, l, bm=512)
        y_mla = _mla_attention(qt, k, vt)
        y_ssm = _glu(_ssm_scan(proj, ssm_mats, l), w_glu, b_glu, l, bm=1024)
        y_dil = _dil_attention(proj)
        x = _out_proj(y_mla, y_ssm, y_dil, g_out_mla, g_out_ssm, g_out_dil,
                      w_o_b, x, l, bm=1024, bn=1024)
        act = _ffn_up(x, g_ffn, w_gate, w_up, l, bm=1024, bn=512)
        x = _ffn_down(act, w_down, x, l, bm=1024, bn=256)
    out = _final_norm(x, g_final, bm=512)
    return out.reshape(BATCH, SEQ, D_MODEL)
```

```python
import functools
import math

import jax
import jax.numpy as jnp
import numpy as np
from jax import lax
from jax.experimental import pallas as pl
from jax.experimental.pallas import tpu as pltpu

F32 = jnp.float32
BF16 = jnp.bfloat16

D_MODEL = 2048
BATCH = 4
SEQ = 2048
DEPTH = 4
TOKENS = BATCH * SEQ

MLA_HEADS = 8
MLA_NOPE = 128
MLA_ROPE = 64
MLA_V = 128
MLA_Q_LORA = 512
MLA_KV_LORA = 256
MLA_WIDTH = MLA_HEADS * MLA_V
MLA_QK_PAD = 256
ROPE_THETA = 10000.0
ROPE_HALF = MLA_ROPE // 2

SSM_WIDTH = 512
SSM_GROUP = 16
SSM_GROUPS = 32
SSM_STATE = 64
SSM_CHUNK = 8

DIL_WIDTH = 512
DIL_HEAD_DIM = 64
DIL_HEADS = 8
DIL_PATTERNS = ((128, 1), (512, 4), (2048, 16))
BLOCK = 128

IN_PAD = 3072
DIL_Q_COL = 1536
DIL_K_COL = 2048
DIL_V_COL = 2560
SSM_U_COL = 1024
D_FF = 5632
NORM_EPS = 1e-6

LANES = 128
VMEM_LIMIT_BYTES = 56 * 1024 * 1024

NT_DIMS = (((1,), (1,)), ((), ()))


def _cparams(*semantics):
    return pltpu.CompilerParams(dimension_semantics=semantics,
                                vmem_limit_bytes=VMEM_LIMIT_BYTES)


def _rms_rows(x, g):
    ms = jnp.mean(x * x, axis=-1, keepdims=True)
    return x * lax.rsqrt(ms + NORM_EPS) * g


NORM_CHUNK = 256


def _first_step_by_chunks(h_ref, normed_chunk, make_emit):
    @pl.when(pl.program_id(1) == 0)
    def _():
        emit = make_emit()
        for r in range(0, h_ref.shape[0], NORM_CHUNK):
            rows = slice(r, r + NORM_CHUNK)
            h = normed_chunk(rows)
            h_ref[rows, :] = h
            emit(rows, h)

    @pl.when(pl.program_id(1) != 0)
    def _():
        make_emit()(slice(None), h_ref[...])


def _bf16(w):
    return w if w.dtype == BF16 else w.astype(BF16)


def _layer_spec(shape, index_map):
    return pl.BlockSpec((None,) + tuple(shape), index_map)


def _norm_matmul_kernel(x_ref, g_ref, w_ref, o_ref, h_ref):
    g = g_ref[...]

    def make_emit():
        w = _bf16(w_ref[...])

        def emit(rows, h):
            o_ref[rows, :] = jnp.dot(h, w, preferred_element_type=F32).astype(o_ref.dtype)
        return emit

    _first_step_by_chunks(h_ref, lambda rows: _rms_rows(x_ref[rows, :], g).astype(BF16),
                          make_emit)


def _norm_matmul(x, g, w, layer, *, bm, bn, out_dtype, name):
    m, k = x.shape
    n = w.shape[2]
    return pl.pallas_call(
        _norm_matmul_kernel,
        grid=(m // bm, n // bn),
        in_specs=[pl.BlockSpec((bm, k), lambda i, j: (i, 0)),
                  _layer_spec((1, k), lambda i, j: (layer, 0, 0)),
                  _layer_spec((k, bn), lambda i, j: (layer, 0, j))],
        out_specs=pl.BlockSpec((bm, bn), lambda i, j: (i, j)),
        out_shape=jax.ShapeDtypeStruct((m, n), out_dtype),
        scratch_shapes=[pltpu.VMEM((bm, k), BF16)],
        compiler_params=_cparams("parallel", "arbitrary"),
        name=name,
    )(x, g.reshape(DEPTH, 1, k), w)


def _q_up_kernel(c_ref, g_ref, wt_ref, cos_ref, sin_ref, o_ref, *, scale):
    h = _rms_rows(c_ref[...], g_ref[...]).astype(BF16)
    qt = lax.dot_general(wt_ref[...], h, NT_DIMS, preferred_element_type=F32)
    cos, sin = cos_ref[...], sin_ref[...]
    for hd in range(MLA_HEADS):
        r0 = hd * MLA_QK_PAD
        r1, r2, r3 = r0 + MLA_NOPE, r0 + MLA_NOPE + ROPE_HALF, r0 + MLA_NOPE + MLA_ROPE
        x1, x2 = qt[r1:r2], qt[r2:r3]
        o_ref[r0:r1, :] = (qt[r0:r1] * scale).astype(o_ref.dtype)
        o_ref[r1:r2, :] = (x1 * cos - x2 * sin).astype(o_ref.dtype)
        o_ref[r2:r3, :] = (x2 * cos + x1 * sin).astype(o_ref.dtype)
        o_ref[r3:r0 + MLA_QK_PAD, :] = qt[r3:r0 + MLA_QK_PAD].astype(o_ref.dtype)


def _q_up(proj, g_q, wt_q, cos_t, sin_t, layer, scale, *, bm):
    pos_blocks = SEQ // bm
    tab_spec = pl.BlockSpec((ROPE_HALF, bm), lambda i: (0, i % pos_blocks))
    n = MLA_HEADS * MLA_QK_PAD
    return pl.pallas_call(
        functools.partial(_q_up_kernel, scale=scale),
        grid=(TOKENS // bm,),
        in_specs=[pl.BlockSpec((bm, MLA_Q_LORA), lambda i: (i, 0)),
                  _layer_spec((1, MLA_Q_LORA), lambda i: (layer, 0, 0)),
                  _layer_spec((n, MLA_Q_LORA), lambda i: (layer, 0, 0)),
                  tab_spec, tab_spec],
        out_specs=pl.BlockSpec((n, bm), lambda i: (0, i)),
        out_shape=jax.ShapeDtypeStruct((n, TOKENS), BF16),
        compiler_params=_cparams("parallel"),
        name="mla_q_up",
    )(proj, g_q.reshape(DEPTH, 1, -1), wt_q, cos_t, sin_t)


def _rope_pad(x, cos_t, sin_a, sin_b):
    up = pltpu.roll(x, MLA_QK_PAD - ROPE_HALF, axis=1)
    dn = pltpu.roll(x, ROPE_HALF, axis=1)
    return x * cos_t + up * sin_a + dn * sin_b


def _kv_up_kernel(c_ref, kr_ref, g_ref, wk_ref, wvt_ref, cos_ref, sina_ref, sinb_ref,
                  k_ref, vt_ref):
    h = _rms_rows(c_ref[...], g_ref[...]).astype(BF16)
    kn = jnp.dot(h, wk_ref[...], preferred_element_type=F32)
    k_pe = _rope_pad(kr_ref[...], cos_ref[...], sina_ref[...], sinb_ref[...])
    for hd in range(MLA_HEADS):
        sl = slice(hd * MLA_QK_PAD, (hd + 1) * MLA_QK_PAD)
        k_ref[:, sl] = (kn[:, sl] + k_pe).astype(k_ref.dtype)
    vt_ref[...] = lax.dot_general(wvt_ref[...], h, NT_DIMS,
                                  preferred_element_type=F32).astype(vt_ref.dtype)


def _kv_up(proj, g_kv, w_k, wt_v, tabs, layer, *, bm):
    pos_blocks = SEQ // bm
    tab_spec = pl.BlockSpec((bm, MLA_QK_PAD), lambda i: (i % pos_blocks, 0))
    nk = MLA_HEADS * MLA_QK_PAD
    return pl.pallas_call(
        _kv_up_kernel,
        grid=(TOKENS // bm,),
        in_specs=[pl.BlockSpec((bm, MLA_KV_LORA), lambda i: (i, 2)),
                  pl.BlockSpec((bm, MLA_QK_PAD), lambda i: (i, 3)),
                  _layer_spec((1, MLA_KV_LORA), lambda i: (layer, 0, 0)),
                  _layer_spec((MLA_KV_LORA, nk), lambda i: (layer, 0, 0)),
                  _layer_spec((MLA_WIDTH, MLA_KV_LORA), lambda i: (layer, 0, 0)),
                  tab_spec, tab_spec, tab_spec],
        out_specs=[pl.BlockSpec((bm, nk), lambda i: (i, 0)),
                   pl.BlockSpec((MLA_WIDTH, bm), lambda i: (0, i))],
        out_shape=[jax.ShapeDtypeStruct((TOKENS, nk), BF16),
                   jax.ShapeDtypeStruct((MLA_WIDTH, TOKENS), BF16)],
        compiler_params=_cparams("parallel"),
        name="mla_kv_up",
    )(proj, proj, g_kv.reshape(DEPTH, 1, -1), w_k, wt_v, *tabs)


ATT_BQ = 256
ATT_BK = 256
ATT_NQ = SEQ // ATT_BQ


def _mla_attn_kernel(qt_ref, k_ref, vt_ref, o_ref, m_sc, l_sc, acc_sc):
    key = lax.broadcasted_iota(jnp.int32, (ATT_BK, ATT_BQ), 0)
    qry = lax.broadcasted_iota(jnp.int32, (ATT_BK, ATT_BQ), 1)
    causal = key <= qry
    tiles = [(i, j) for j in range(ATT_NQ) for i in range(j, ATT_NQ)]

    def scores(i, j):
        kj = k_ref[j * ATT_BK:(j + 1) * ATT_BK, :]
        qi = qt_ref[:, i * ATT_BQ:(i + 1) * ATT_BQ]
        return jnp.dot(kj, qi, preferred_element_type=F32)

    def softmax(i, j, s):
        if i == j:
            s = jnp.where(causal, s, -jnp.inf)
        m_blk = jnp.max(s, axis=0, keepdims=True)
        if j == 0:
            m_new, alpha = m_blk, None
            p = jnp.exp2(s - m_new)
            l = jnp.sum(p, axis=0, keepdims=True)
        else:
            m_old = m_sc[i]
            m_new = jnp.maximum(m_old, m_blk)
            alpha = jnp.exp2(m_old - m_new)
            p = jnp.exp2(s - m_new)
            l = alpha * l_sc[i] + jnp.sum(p, axis=0, keepdims=True)
        if i != j:
            m_sc[i] = m_new
            l_sc[i] = l
        return p.astype(BF16), alpha, l

    def values(i, j, p, alpha, l):
        vj = vt_ref[:, j * ATT_BK:(j + 1) * ATT_BK]
        acc = jnp.dot(vj, p, preferred_element_type=F32)
        if alpha is not None:
            acc = alpha * acc_sc[i] + acc
        if i == j:
            o_ref[i * ATT_BQ:(i + 1) * ATT_BQ, :] = (acc / l).T.astype(o_ref.dtype)
        else:
            acc_sc[i] = acc

    n = len(tiles)
    s_ready = {0: scores(*tiles[0])}
    if n > 1:
        s_ready[1] = scores(*tiles[1])
    p_ready = {0: softmax(*tiles[0], s_ready.pop(0))}
    for t in range(n):
        if t + 2 < n:
            s_ready[t + 2] = scores(*tiles[t + 2])
        if t + 1 < n:
            p_ready[t + 1] = softmax(*tiles[t + 1], s_ready.pop(t + 1))
        values(*tiles[t], *p_ready.pop(t))


def _mla_attention(qt, k, vt):
    return pl.pallas_call(
        _mla_attn_kernel,
        grid=(BATCH, MLA_HEADS),
        in_specs=[pl.BlockSpec((MLA_QK_PAD, SEQ), lambda b, h: (h, b)),
                  pl.BlockSpec((SEQ, MLA_QK_PAD), lambda b, h: (b, h)),
                  pl.BlockSpec((MLA_V, SEQ), lambda b, h: (h, b))],
        out_specs=pl.BlockSpec((SEQ, MLA_V), lambda b, h: (b, h)),
        out_shape=jax.ShapeDtypeStruct((TOKENS, MLA_WIDTH), BF16),
        scratch_shapes=[pltpu.VMEM((ATT_NQ, 1, ATT_BQ), F32),
                        pltpu.VMEM((ATT_NQ, 1, ATT_BQ), F32),
                        pltpu.VMEM((ATT_NQ, MLA_V, ATT_BQ), F32)],
        compiler_params=_cparams("parallel", "parallel"),
        name="mla_attention",
    )(qt, k, vt)


def _dil_attn_kernel(q_ref, k_ref, v_ref, o_ref, m_sc, l_sc, n_sc):
    row2 = lax.broadcasted_iota(jnp.int32, (BLOCK, 2 * BLOCK), 0)
    col2 = lax.broadcasted_iota(jnp.int32, (BLOCK, 2 * BLOCK), 1)
    dist = row2 + BLOCK - col2
    band = (dist >= 0) & (dist <= BLOCK)
    row1 = lax.broadcasted_iota(jnp.int32, (BLOCK, BLOCK), 0)
    col1 = lax.broadcasted_iota(jnp.int32, (BLOCK, BLOCK), 1)
    tri = row1 >= col1
    q_scale = DIL_HEAD_DIM ** -0.5 * math.log2(math.e)
    heads = LANES // DIL_HEAD_DIM
    n_patterns = len(DIL_PATTERNS)

    def rows_at(start, dil):
        return pl.ds(start, BLOCK) if dil == 1 else pl.ds(start, BLOCK, stride=dil)

    blocks = [(pi, dil, r, n) for pi, (_, dil) in enumerate(reversed(DIL_PATTERNS))
              for r in range(dil) for n in range(SEQ // dil // BLOCK)]

    def load(pi, dil, r, n):
        rows = rows_at(r + dil * BLOCK * n, dil)
        q = (q_ref[rows, :] * q_scale).astype(BF16)
        if n == 0:
            return rows, q, k_ref[rows, :].astype(BF16), v_ref[rows, :].astype(BF16), tri
        prev = rows_at(r + dil * BLOCK * (n - 1), dil)
        kk = jnp.concatenate([k_ref[prev, :], k_ref[rows, :]], axis=0).astype(BF16)
        vv = jnp.concatenate([v_ref[prev, :], v_ref[rows, :]], axis=0).astype(BF16)
        return rows, q, kk, vv, band

    def scores(blk, hd):
        _, q, kk, _, _ = blk
        sl = slice(hd * DIL_HEAD_DIM, (hd + 1) * DIL_HEAD_DIM)
        return lax.dot_general(q[:, sl], kk[:, sl], NT_DIMS, preferred_element_type=F32)

    def softmax(blk, s):
        s = jnp.where(blk[4], s, -jnp.inf)
        m = jnp.max(s, axis=-1, keepdims=True)
        p = jnp.exp2(s - m)
        l = jnp.sum(p, axis=-1, keepdims=True)
        return (p.astype(BF16), jnp.broadcast_to(m, (BLOCK, DIL_HEAD_DIM)),
                jnp.broadcast_to(l, (BLOCK, DIL_HEAD_DIM)))

    def values(blk, hd, p):
        sl = slice(hd * DIL_HEAD_DIM, (hd + 1) * DIL_HEAD_DIM)
        return jnp.dot(p, blk[3][:, sl], preferred_element_type=F32)

    def merge(pi, rows, parts):
        m2, l2, a2 = (jnp.concatenate([p[c] for p in parts], axis=-1) for c in range(3))
        if pi > 0:
            m_old = m_sc[rows, :]
            m_new = jnp.maximum(m_old, m2)
            w_old, w_new = jnp.exp2(m_old - m_new), jnp.exp2(m2 - m_new)
            l2 = w_old * l_sc[rows, :] + w_new * l2
            a2 = w_old * n_sc[rows, :] + w_new * a2
            m2 = m_new
        if pi == n_patterns - 1:
            o_ref[rows, :] = a2 / l2
        else:
            m_sc[rows, :] = m2
            l_sc[rows, :] = l2
            n_sc[rows, :] = a2

    items = [(b, hd) for b in range(len(blocks)) for hd in range(heads)]
    blk_next = load(*blocks[0])
    s_next = scores(blk_next, 0)
    parts = []
    for t, (b, hd) in enumerate(items):
        blk, s_cur = blk_next, s_next
        if t + 1 < len(items):
            nb_, nhd = items[t + 1]
            if nb_ != b:
                blk_next = load(*blocks[nb_])
            s_next = scores(blk_next, nhd)
        p, m_b, l_b = softmax(blk, s_cur)
        parts.append((m_b, l_b, values(blk, hd, p)))
        if hd == heads - 1:
            merge(blocks[b][0], blk[0], parts)
            parts = []


def _dil_attention(proj):
    def spec(col0):
        return pl.BlockSpec((SEQ, LANES), lambda b, hp: (b, col0 // LANES + hp))

    return pl.pallas_call(
        _dil_attn_kernel,
        grid=(BATCH, DIL_WIDTH // LANES),
        in_specs=[spec(DIL_Q_COL), spec(DIL_K_COL), spec(DIL_V_COL)],
        out_specs=pl.BlockSpec((SEQ, LANES), lambda b, hp: (b, hp)),
        out_shape=jax.ShapeDtypeStruct((TOKENS, DIL_WIDTH), F32),
        scratch_shapes=[pltpu.VMEM((SEQ, LANES), F32)] * 3,
        compiler_params=_cparams("parallel", "parallel"),
        name="dilated_attention",
    )(proj, proj, proj)


SLAB_GROUPS = LANES // SSM_GROUP
SSM_SLABS = SSM_GROUPS // SLAB_GROUPS
SSM_NCHUNK = SEQ // SSM_CHUNK
SSM_ROWS = BATCH * SSM_NCHUNK
SSM_ROW = SSM_CHUNK * LANES
SLAB_STATE = SLAB_GROUPS * SSM_STATE
STATE_TILES = SLAB_STATE // LANES
SUBLANES = 8


def _split_bf16(x):
    hi = x.astype(BF16)
    return hi, (x - hi.astype(F32)).astype(BF16)


def _expand_block_diag(x, width):
    rows, c = x.shape
    src = lax.broadcasted_iota(jnp.int32, (c, width), 0)
    dst = lax.broadcasted_iota(jnp.int32, (c, width), 1)
    repeat = jnp.where(dst % c == src, 1.0, 0.0).astype(BF16)
    wide = jnp.dot(x.astype(BF16), repeat, preferred_element_type=F32)
    row_g = lax.broadcasted_iota(jnp.int32, (rows, width), 0) // (rows // SLAB_GROUPS)
    col_g = lax.broadcasted_iota(jnp.int32, (rows, width), 1) // c
    return jnp.where(row_g == col_g, wide, 0.0).astype(BF16)


def _ssm_kernel(u_ref, dlag_ref, m1re_ref, m1im_ref, m2re_ref, m2im_ref,
                are_ref, aim_ref, d_ref, y_ref, fold_ref, st_ref, t0_sc, op_sc):
    lag_blocks = [_expand_block_diag(dlag_ref[0, tau], LANES) for tau in range(SSM_CHUNK)]
    no_block = jnp.zeros((LANES, LANES), BF16)
    for t in range(SSM_CHUNK):
        for s in range(SSM_CHUNK):
            t0_sc[t * LANES:(t + 1) * LANES, s * LANES:(s + 1) * LANES] = (
                lag_blocks[s - t] if s >= t else no_block)
    for which, ref in enumerate((m1re_ref, m1im_ref, m2re_ref, m2im_ref)):
        for t in range(SSM_CHUNK):
            op_sc[which, t * LANES:(t + 1) * LANES, :] = _expand_block_diag(ref[0, t], SLAB_STATE)

    for b in range(BATCH):
        rows = slice(b * SSM_NCHUNK, (b + 1) * SSM_NCHUNK)
        for t in range(SSM_CHUNK):
            fold_ref[rows, t * LANES:(t + 1) * LANES] = (
                u_ref[pl.ds(b * SEQ + t, SSM_NCHUNK, stride=SSM_CHUNK), :])
    u = fold_ref[...]
    ub = u.astype(BF16)

    loc_re = jnp.dot(ub, op_sc[0], preferred_element_type=F32)
    loc_im = jnp.dot(ub, op_sc[1], preferred_element_type=F32)
    for b in range(BATCH):
        rows = slice(b * SSM_NCHUNK, (b + 1) * SSM_NCHUNK)
        for k in range(STATE_TILES):
            lanes = slice(k * LANES, (k + 1) * LANES)
            st_ref[k, pl.ds(b, SSM_NCHUNK, stride=SUBLANES), :] = loc_re[rows, lanes]
            st_ref[k, pl.ds(BATCH + b, SSM_NCHUNK, stride=SUBLANES), :] = loc_im[rows, lanes]

    a_re, a_im = are_ref[0], aim_ref[0]
    upper = lax.broadcasted_iota(jnp.int32, (SUBLANES, LANES), 0) < BATCH
    mul_same, mul_swap = [], []
    for k in range(STATE_TILES):
        lanes = slice(k * LANES, (k + 1) * LANES)
        mul_same.append(jnp.broadcast_to(a_re[:, lanes], (SUBLANES, LANES)))
        im = jnp.broadcast_to(a_im[:, lanes], (SUBLANES, LANES))
        mul_swap.append(jnp.where(upper, -im, im))

    def chunk_step(c, state):
        r0 = pl.multiple_of(c * SUBLANES, SUBLANES)
        new = []
        for k in range(STATE_TILES):
            loc = st_ref[k, pl.ds(r0, SUBLANES), :]
            st_ref[k, pl.ds(r0, SUBLANES), :] = state[k]
            swapped = pltpu.roll(state[k], BATCH, axis=0)
            new.append(mul_same[k] * state[k] + mul_swap[k] * swapped + loc)
        return tuple(new)

    zero = jnp.zeros((SUBLANES, LANES), F32)
    lax.fori_loop(0, SSM_NCHUNK, chunk_step, (zero,) * STATE_TILES, unroll=4)

    def entering(offset):
        return jnp.concatenate(
            [jnp.concatenate([st_ref[k, pl.ds(offset + b, SSM_NCHUNK, stride=SUBLANES), :]
                              for k in range(STATE_TILES)], axis=1)
             for b in range(BATCH)], axis=0)

    y = jnp.dot(ub, t0_sc[...], preferred_element_type=F32) + u * d_ref[0]
    for offset, which in ((0, 2), (BATCH, 3)):
        hi, lo = _split_bf16(entering(offset))
        m2t = op_sc[which]
        y = y + (lax.dot_general(hi, m2t, NT_DIMS, preferred_element_type=F32)
                 + lax.dot_general(lo, m2t, NT_DIMS, preferred_element_type=F32))
    fold_ref[...] = jax.nn.gelu(y, approximate=True)

    for b in range(BATCH):
        rows = slice(b * SSM_NCHUNK, (b + 1) * SSM_NCHUNK)
        for t in range(SSM_CHUNK):
            y_ref[pl.ds(b * SEQ + t, SSM_NCHUNK, stride=SSM_CHUNK), :] = (
                fold_ref[rows, t * LANES:(t + 1) * LANES])


def _ssm_scan(proj, mats, layer):
    dlag, m1re, m1im, m2re, m2im, a_re, a_im, dvec = mats

    def spec(r, c):
        return _layer_spec((1, r, c), lambda s: (layer, s, 0, 0))

    def compact(c):
        return _layer_spec((1, SSM_CHUNK, LANES, c), lambda s: (layer, s, 0, 0, 0))

    return pl.pallas_call(
        _ssm_kernel,
        grid=(SSM_SLABS,),
        in_specs=[pl.BlockSpec((TOKENS, LANES), lambda s: (0, SSM_U_COL // LANES + s)),
                  compact(SSM_GROUP),
                  compact(SSM_STATE), compact(SSM_STATE),
                  compact(SSM_STATE), compact(SSM_STATE),
                  spec(1, SLAB_STATE), spec(1, SLAB_STATE), spec(1, SSM_ROW)],
        out_specs=pl.BlockSpec((TOKENS, LANES), lambda s: (0, s)),
        out_shape=jax.ShapeDtypeStruct((TOKENS, SSM_WIDTH), F32),
        scratch_shapes=[pltpu.VMEM((SSM_ROWS, SSM_ROW), F32),
                        pltpu.VMEM((STATE_TILES, SUBLANES * SSM_NCHUNK, LANES), F32),
                        pltpu.VMEM((SSM_ROW, SSM_ROW), BF16),
                        pltpu.VMEM((4, SSM_ROW, SLAB_STATE), BF16)],
        compiler_params=_cparams("arbitrary"),
        name="ssm_chunk_scan",
    )(proj, dlag, m1re, m1im, m2re, m2im, a_re, a_im, dvec)


def _ssm_matrices(a_re, a_im, b_re, b_im, c_re, c_im, d_skip, log_dt):
    lam_re = jnp.minimum(a_re, -1e-4)
    lam_im = a_im
    dt = jnp.exp(log_dt)[:, None]
    mag = jnp.exp(lam_re * dt)
    ab_re, ab_im = mag * jnp.cos(lam_im * dt), mag * jnp.sin(lam_im * dt)
    n_re, n_im = ab_re - 1.0, ab_im
    den = lam_re * lam_re + lam_im * lam_im
    f_re = (n_re * lam_re + n_im * lam_im) / den
    f_im = (n_im * lam_re - n_re * lam_im) / den
    bb_re = f_re[..., None] * b_re - f_im[..., None] * b_im
    bb_im = f_re[..., None] * b_im + f_im[..., None] * b_re
    p_re, p_im = [jnp.ones_like(ab_re)], [jnp.zeros_like(ab_im)]
    for _ in range(SSM_CHUNK):
        p_re.append(p_re[-1] * ab_re - p_im[-1] * ab_im)
        p_im.append(p_re[-2] * ab_im + p_im[-1] * ab_re)
    pw_re, pw_im = jnp.stack(p_re), jnp.stack(p_im)

    cb_re = (c_re[:, None, :, :] * bb_re.transpose(0, 2, 1)[:, :, None, :]
             - c_im[:, None, :, :] * bb_im.transpose(0, 2, 1)[:, :, None, :])
    cb_im = (c_re[:, None, :, :] * bb_im.transpose(0, 2, 1)[:, :, None, :]
             + c_im[:, None, :, :] * bb_re.transpose(0, 2, 1)[:, :, None, :])
    lag_re = pw_re[:SSM_CHUNK].transpose(1, 0, 2)[:, :, None, None, :]
    lag_im = pw_im[:SSM_CHUNK].transpose(1, 0, 2)[:, :, None, None, :]
    kern = jnp.sum(lag_re * cb_re[:, None] - lag_im * cb_im[:, None], axis=-1)
    t_idx = np.arange(SSM_CHUNK)

    back = pw_re[SSM_CHUNK - 1 - t_idx], pw_im[SSM_CHUNK - 1 - t_idx]
    bre_t, bim_t = bb_re.transpose(0, 2, 1), bb_im.transpose(0, 2, 1)
    m1re = (back[0].transpose(1, 0, 2)[:, :, None, :] * bre_t[:, None]
            - back[1].transpose(1, 0, 2)[:, :, None, :] * bim_t[:, None])
    m1im = (back[0].transpose(1, 0, 2)[:, :, None, :] * bim_t[:, None]
            + back[1].transpose(1, 0, 2)[:, :, None, :] * bre_t[:, None])

    fwd_re = pw_re[1:].transpose(1, 0, 2)[:, :, None, :]
    fwd_im = pw_im[1:].transpose(1, 0, 2)[:, :, None, :]
    ca_re = c_re[:, None] * fwd_re - c_im[:, None] * fwd_im
    ca_im = c_re[:, None] * fwd_im + c_im[:, None] * fwd_re

    def rows_by_group(x):
        k, r, c = x.shape[1:]
        x = x.reshape(SSM_SLABS, SLAB_GROUPS, k, r, c).transpose(0, 2, 1, 3, 4)
        return x.reshape(SSM_SLABS, k, SLAB_GROUPS * r, c)

    a_step_re = pw_re[SSM_CHUNK].reshape(SSM_SLABS, 1, SLAB_STATE)
    a_step_im = pw_im[SSM_CHUNK].reshape(SSM_SLABS, 1, SLAB_STATE)
    dvec = jnp.concatenate([d_skip.reshape(SSM_SLABS, 1, LANES)] * SSM_CHUNK, axis=-1)
    return (rows_by_group(kern), rows_by_group(m1re), rows_by_group(m1im),
            rows_by_group(ca_re), rows_by_group(-ca_im), a_step_re, a_step_im, dvec)


def _glu_kernel(y_ref, w_ref, b_ref, o_ref):
    z = jnp.dot(y_ref[...].astype(BF16), _bf16(w_ref[...]),
                preferred_element_type=F32) + b_ref[...]
    o_ref[...] = (z[:, :SSM_WIDTH] * jax.nn.sigmoid(z[:, SSM_WIDTH:])).astype(o_ref.dtype)


def _glu(y, w_glu, b_glu, layer, *, bm):
    return pl.pallas_call(
        _glu_kernel,
        grid=(TOKENS // bm,),
        in_specs=[pl.BlockSpec((bm, SSM_WIDTH), lambda i: (i, 0)),
                  _layer_spec((SSM_WIDTH, 2 * SSM_WIDTH), lambda i: (layer, 0, 0)),
                  _layer_spec((1, 2 * SSM_WIDTH), lambda i: (layer, 0, 0))],
        out_specs=pl.BlockSpec((bm, SSM_WIDTH), lambda i: (i, 0)),
        out_shape=jax.ShapeDtypeStruct((TOKENS, SSM_WIDTH), BF16),
        compiler_params=_cparams("parallel"),
        name="ssm_glu",
    )(y, w_glu, b_glu.reshape(DEPTH, 1, -1))


def _out_proj_kernel(ya_ref, yb_ref, yc_ref, ga_ref, gb_ref, gc_ref, w_ref, x_ref,
                     o_ref, h_ref):
    parts = ((ya_ref, ga_ref[...]), (yb_ref, gb_ref[...]), (yc_ref, gc_ref[...]))

    def normed_chunk(rows):
        return jnp.concatenate([_rms_rows(y_ref[rows, :].astype(F32), g).astype(BF16)
                                for y_ref, g in parts], axis=-1)

    def make_emit():
        w = _bf16(w_ref[...])

        def emit(rows, h):
            o_ref[rows, :] = x_ref[rows, :] + jnp.dot(h, w, preferred_element_type=F32)
        return emit

    _first_step_by_chunks(h_ref, normed_chunk, make_emit)


def _out_proj(y_mla, y_ssm, y_dil, g_mla, g_ssm, g_dil, w_o, x, layer, *, bm, bn):
    def rows(width):
        return pl.BlockSpec((bm, width), lambda i, j: (i, 0))

    def gain(width):
        return _layer_spec((1, width), lambda i, j: (layer, 0, 0))

    return pl.pallas_call(
        _out_proj_kernel,
        grid=(TOKENS // bm, D_MODEL // bn),
        in_specs=[rows(MLA_WIDTH), rows(SSM_WIDTH), rows(DIL_WIDTH),
                  gain(MLA_WIDTH), gain(SSM_WIDTH), gain(DIL_WIDTH),
                  _layer_spec((D_MODEL, bn), lambda i, j: (layer, 0, j)),
                  pl.BlockSpec((bm, bn), lambda i, j: (i, j))],
        out_specs=pl.BlockSpec((bm, bn), lambda i, j: (i, j)),
        out_shape=jax.ShapeDtypeStruct((TOKENS, D_MODEL), F32),
        scratch_shapes=[pltpu.VMEM((bm, D_MODEL), BF16)],
        compiler_params=_cparams("parallel", "arbitrary"),
        name="out_proj",
    )(y_mla, y_ssm, y_dil, g_mla.reshape(DEPTH, 1, -1), g_ssm.reshape(DEPTH, 1, -1),
      g_dil.reshape(DEPTH, 1, -1), w_o, x)


def _ffn_up_kernel(x_ref, g_ref, wg_ref, wu_ref, wd_ref, o_ref, wd_bf16_ref, h_ref):
    g = g_ref[...]
    wd_bf16_ref[...] = wd_ref[...].astype(BF16)

    def make_emit():
        wg, wu = _bf16(wg_ref[...]), _bf16(wu_ref[...])

        def emit(rows, h):
            gate = jnp.dot(h, wg, preferred_element_type=F32)
            up = jnp.dot(h, wu, preferred_element_type=F32)
            o_ref[rows, :] = (jax.nn.silu(gate) * up).astype(o_ref.dtype)
        return emit

    _first_step_by_chunks(h_ref, lambda rows: _rms_rows(x_ref[rows, :], g).astype(BF16),
                          make_emit)


def _ffn_up(x, g, w_gate, w_up, w_down, layer, *, bm, bn):
    col_steps = D_FF // bn
    wd_rows = D_FF // (TOKENS // bm * col_steps)
    return pl.pallas_call(
        _ffn_up_kernel,
        grid=(TOKENS // bm, col_steps),
        in_specs=[pl.BlockSpec((bm, D_MODEL), lambda i, j: (i, 0)),
                  _layer_spec((1, D_MODEL), lambda i, j: (layer, 0, 0)),
                  _layer_spec((D_MODEL, bn), lambda i, j: (layer, 0, j)),
                  _layer_spec((D_MODEL, bn), lambda i, j: (layer, 0, j)),
                  _layer_spec((wd_rows, D_MODEL), lambda i, j: (layer, i * col_steps + j, 0))],
        out_specs=[pl.BlockSpec((bm, bn), lambda i, j: (i, j)),
                   pl.BlockSpec((wd_rows, D_MODEL), lambda i, j: (i * col_steps + j, 0))],
        out_shape=[jax.ShapeDtypeStruct((TOKENS, D_FF), BF16),
                   jax.ShapeDtypeStruct((D_FF, D_MODEL), BF16)],
        scratch_shapes=[pltpu.VMEM((bm, D_MODEL), BF16)],
        compiler_params=_cparams("arbitrary", "arbitrary"),
        name="ffn_gate_up",
    )(x, g.reshape(DEPTH, 1, -1), w_gate, w_up, w_down)


def _ffn_down_kernel(a_ref, w_ref, x_ref, o_ref):
    o_ref[...] = x_ref[...] + jnp.dot(a_ref[...], _bf16(w_ref[...]),
                                      preferred_element_type=F32)


def _ffn_down(a, w_down, x, *, bm, bn):
    return pl.pallas_call(
        _ffn_down_kernel,
        grid=(TOKENS // bm, D_MODEL // bn),
        in_specs=[pl.BlockSpec((bm, D_FF), lambda i, j: (i, 0)),
                  pl.BlockSpec((D_FF, bn), lambda i, j: (0, j)),
                  pl.BlockSpec((bm, bn), lambda i, j: (i, j))],
        out_specs=pl.BlockSpec((bm, bn), lambda i, j: (i, j)),
        out_shape=jax.ShapeDtypeStruct((TOKENS, D_MODEL), F32),
        compiler_params=_cparams("parallel", "arbitrary"),
        name="ffn_down",
    )(a, w_down, x)


def _final_norm_kernel(x_ref, g_ref, o_ref):
    o_ref[...] = _rms_rows(x_ref[...], g_ref[...])


def _final_norm(x, g, *, bm):
    return pl.pallas_call(
        _final_norm_kernel,
        grid=(TOKENS // bm,),
        in_specs=[pl.BlockSpec((bm, D_MODEL), lambda i: (i, 0)),
                  pl.BlockSpec((1, D_MODEL), lambda i: (0, 0))],
        out_specs=pl.BlockSpec((bm, D_MODEL), lambda i: (i, 0)),
        out_shape=jax.ShapeDtypeStruct((TOKENS, D_MODEL), F32),
        compiler_params=_cparams("parallel"),
        name="final_norm",
    )(x, g.reshape(1, -1))


def _pad_w_in(w_in):
    lyr, d, _ = w_in.shape
    z = lambda n: jnp.zeros((lyr, d, n), BF16)
    w = w_in.astype(BF16)
    return jnp.concatenate([w[..., :768], z(128), w[..., 768:832], z(64), w[..., 832:]],
                           axis=-1)


def _pad_wt_uq(w_uq):
    lyr = w_uq.shape[0]
    w = w_uq.reshape(lyr, MLA_Q_LORA, MLA_HEADS, MLA_NOPE + MLA_ROPE)
    w = jnp.pad(w, ((0, 0), (0, 0), (0, 0), (0, MLA_QK_PAD - MLA_NOPE - MLA_ROPE)))
    w = w.reshape(lyr, MLA_Q_LORA, MLA_HEADS * MLA_QK_PAD)
    return w.transpose(0, 2, 1).astype(BF16)


def _split_w_ukv(w_ukv):
    lyr = w_ukv.shape[0]
    w = w_ukv.reshape(lyr, MLA_KV_LORA, MLA_HEADS, MLA_NOPE + MLA_V)
    wk = jnp.pad(w[..., :MLA_NOPE], ((0, 0), (0, 0), (0, 0), (0, MLA_QK_PAD - MLA_NOPE)))
    wk = wk.reshape(lyr, MLA_KV_LORA, -1).astype(BF16)
    wv = w[..., MLA_NOPE:].reshape(lyr, MLA_KV_LORA, -1)
    return wk, wv.transpose(0, 2, 1).astype(BF16)


def _rope_angles():
    inv_freq = ROPE_THETA ** (-jnp.arange(ROPE_HALF, dtype=F32) / ROPE_HALF)
    ang = jnp.arange(SEQ, dtype=F32)[:, None] * inv_freq[None, :]
    return jnp.cos(ang), jnp.sin(ang)


def _rope_tables_k():
    cos, sin = _rope_angles()
    one = jnp.ones((SEQ, MLA_NOPE), F32)
    z = lambda n: jnp.zeros((SEQ, n), F32)
    tail = MLA_QK_PAD - MLA_NOPE - MLA_ROPE
    cos_t = jnp.concatenate([one, cos, cos, z(tail)], axis=1)
    sin_a = jnp.concatenate([z(MLA_NOPE), -sin, z(ROPE_HALF), z(tail)], axis=1)
    sin_b = jnp.concatenate([z(MLA_NOPE), z(ROPE_HALF), sin, z(tail)], axis=1)
    return cos_t, sin_a, sin_b


def kernel(x, g_mix, w_in, g_q, w_uq, g_kv, w_ukv, a_re, a_im, b_re, b_im, c_re, c_im,
           d_skip, log_dt, w_glu, b_glu, g_out_mla, g_out_ssm, g_out_dil, w_o,
           g_ffn, w_gate, w_up, w_down, g_final):
    x = x.reshape(TOKENS, D_MODEL)
    w_in_p = _pad_w_in(w_in)
    wt_q = _pad_wt_uq(w_uq)
    w_k, wt_v = _split_w_ukv(w_ukv)
    w_o_b = w_o.astype(BF16)
    q_scale = (MLA_NOPE + MLA_ROPE) ** -0.5 * math.log2(math.e)
    cos, sin = _rope_angles()
    q_cos_t, q_sin_t = (cos * q_scale).T, (sin * q_scale).T
    k_tabs = _rope_tables_k()
    ssm_mats = jax.vmap(_ssm_matrices)(a_re, a_im, b_re, b_im, c_re, c_im, d_skip, log_dt)

    for l in range(DEPTH):
        proj = _norm_matmul(x, g_mix, w_in_p, l, bm=1024, bn=1024, out_dtype=F32,
                            name="in_proj")
        qt = _q_up(proj, g_q, wt_q, q_cos_t, q_sin_t, l, q_scale, bm=512)
        k, vt = _kv_up(proj, g_kv, w_k, wt_v, k_tabs, l, bm=512)
        y_mla = _mla_attention(qt, k, vt)
        y_ssm = _glu(_ssm_scan(proj, ssm_mats, l), w_glu, b_glu, l, bm=1024)
        y_dil = _dil_attention(proj)
        x = _out_proj(y_mla, y_ssm, y_dil, g_out_mla, g_out_ssm, g_out_dil,
                      w_o_b, x, l, bm=1024, bn=1024)
        act, w_down_b = _ffn_up(x, g_ffn, w_gate, w_up, w_down, l, bm=1024, bn=512)
        x = _ffn_down(act, w_down_b, x, bm=1024, bn=512)
    out = _final_norm(x, g_final, bm=512)
    return out.reshape(BATCH, SEQ, D_MODEL)
```

```python
import functools
import math

import jax
import jax.numpy as jnp
import numpy as np
from jax import lax
from jax.experimental import pallas as pl
from jax.experimental.pallas import tpu as pltpu

F32 = jnp.float32
BF16 = jnp.bfloat16

D_MODEL = 2048
BATCH = 4
SEQ = 2048
DEPTH = 4
TOKENS = BATCH * SEQ

MLA_HEADS = 8
MLA_NOPE = 128
MLA_ROPE = 64
MLA_V = 128
MLA_Q_LORA = 512
MLA_KV_LORA = 256
MLA_WIDTH = MLA_HEADS * MLA_V
MLA_QK_PAD = 256
ROPE_THETA = 10000.0
ROPE_HALF = MLA_ROPE // 2

SSM_WIDTH = 512
SSM_GROUP = 16
SSM_GROUPS = 32
SSM_STATE = 64
SSM_CHUNK = 8

DIL_WIDTH = 512
DIL_HEAD_DIM = 64
DIL_HEADS = 8
DIL_PATTERNS = ((128, 1), (512, 4), (2048, 16))
BLOCK = 128
DIL_GROUP = 1

IN_PAD = 3072
DIL_Q_COL = 1536
DIL_K_COL = 2048
DIL_V_COL = 2560
SSM_U_COL = 1024
D_FF = 5632
NORM_EPS = 1e-6

LANES = 128
VMEM_LIMIT_BYTES = 56 * 1024 * 1024

NT_DIMS = (((1,), (1,)), ((), ()))


def _cparams(*semantics):
    return pltpu.CompilerParams(dimension_semantics=semantics,
                                vmem_limit_bytes=VMEM_LIMIT_BYTES)


def _rms_rows(x, g):
    ms = jnp.mean(x * x, axis=-1, keepdims=True)
    return x * lax.rsqrt(ms + NORM_EPS) * g


NORM_CHUNK = 256


def _first_step_by_chunks(h_ref, normed_chunk, make_emit):
    @pl.when(pl.program_id(1) == 0)
    def _():
        emit = make_emit()
        for r in range(0, h_ref.shape[0], NORM_CHUNK):
            rows = slice(r, r + NORM_CHUNK)
            h = normed_chunk(rows)
            h_ref[rows, :] = h
            emit(rows, h)

    @pl.when(pl.program_id(1) != 0)
    def _():
        make_emit()(slice(None), h_ref[...])


def _bf16(w):
    return w if w.dtype == BF16 else w.astype(BF16)


def _layer_spec(shape, index_map):
    return pl.BlockSpec((None,) + tuple(shape), index_map)


def _norm_matmul_kernel(x_ref, g_ref, w_ref, o_ref, h_ref):
    g = g_ref[...]

    def make_emit():
        w = _bf16(w_ref[...])

        def emit(rows, h):
            o_ref[rows, :] = jnp.dot(h, w, preferred_element_type=F32).astype(o_ref.dtype)
        return emit

    _first_step_by_chunks(h_ref, lambda rows: _rms_rows(x_ref[rows, :], g).astype(BF16),
                          make_emit)


def _norm_matmul(x, g, w, layer, *, bm, bn, out_dtype, name):
    m, k = x.shape
    n = w.shape[2]
    return pl.pallas_call(
        _norm_matmul_kernel,
        grid=(m // bm, n // bn),
        in_specs=[pl.BlockSpec((bm, k), lambda i, j: (i, 0)),
                  _layer_spec((1, k), lambda i, j: (layer, 0, 0)),
                  _layer_spec((k, bn), lambda i, j: (layer, 0, j))],
        out_specs=pl.BlockSpec((bm, bn), lambda i, j: (i, j)),
        out_shape=jax.ShapeDtypeStruct((m, n), out_dtype),
        scratch_shapes=[pltpu.VMEM((bm, k), BF16)],
        compiler_params=_cparams("parallel", "arbitrary"),
        name=name,
    )(x, g.reshape(DEPTH, 1, k), w)


def _q_up_kernel(c_ref, g_ref, wt_ref, cos_ref, sin_ref, o_ref, *, scale):
    h = _rms_rows(c_ref[...], g_ref[...]).astype(BF16)
    qt = lax.dot_general(wt_ref[...], h, NT_DIMS, preferred_element_type=F32)
    cos, sin = cos_ref[...], sin_ref[...]
    for hd in range(MLA_HEADS):
        r0 = hd * MLA_QK_PAD
        r1, r2, r3 = r0 + MLA_NOPE, r0 + MLA_NOPE + ROPE_HALF, r0 + MLA_NOPE + MLA_ROPE
        x1, x2 = qt[r1:r2], qt[r2:r3]
        o_ref[r0:r1, :] = (qt[r0:r1] * scale).astype(o_ref.dtype)
        o_ref[r1:r2, :] = (x1 * cos - x2 * sin).astype(o_ref.dtype)
        o_ref[r2:r3, :] = (x2 * cos + x1 * sin).astype(o_ref.dtype)
        o_ref[r3:r0 + MLA_QK_PAD, :] = qt[r3:r0 + MLA_QK_PAD].astype(o_ref.dtype)


def _q_up(proj, g_q, wt_q, cos_t, sin_t, layer, scale, *, bm):
    pos_blocks = SEQ // bm
    tab_spec = pl.BlockSpec((ROPE_HALF, bm), lambda i: (0, i % pos_blocks))
    n = MLA_HEADS * MLA_QK_PAD
    return pl.pallas_call(
        functools.partial(_q_up_kernel, scale=scale),
        grid=(TOKENS // bm,),
        in_specs=[pl.BlockSpec((bm, MLA_Q_LORA), lambda i: (i, 0)),
                  _layer_spec((1, MLA_Q_LORA), lambda i: (layer, 0, 0)),
                  _layer_spec((n, MLA_Q_LORA), lambda i: (layer, 0, 0)),
                  tab_spec, tab_spec],
        out_specs=pl.BlockSpec((n, bm), lambda i: (0, i)),
        out_shape=jax.ShapeDtypeStruct((n, TOKENS), BF16),
        compiler_params=_cparams("parallel"),
        name="mla_q_up",
    )(proj, g_q.reshape(DEPTH, 1, -1), wt_q, cos_t, sin_t)


def _rope_pad(x, cos_t, sin_a, sin_b):
    up = pltpu.roll(x, MLA_QK_PAD - ROPE_HALF, axis=1)
    dn = pltpu.roll(x, ROPE_HALF, axis=1)
    return x * cos_t + up * sin_a + dn * sin_b


def _kv_up_kernel(c_ref, kr_ref, g_ref, wk_ref, wvt_ref, cos_ref, sina_ref, sinb_ref,
                  k_ref, vt_ref):
    h = _rms_rows(c_ref[...], g_ref[...]).astype(BF16)
    kn = jnp.dot(h, wk_ref[...], preferred_element_type=F32)
    k_pe = _rope_pad(kr_ref[...], cos_ref[...], sina_ref[...], sinb_ref[...])
    for hd in range(MLA_HEADS):
        sl = slice(hd * MLA_QK_PAD, (hd + 1) * MLA_QK_PAD)
        k_ref[:, sl] = (kn[:, sl] + k_pe).astype(k_ref.dtype)
    vt_ref[...] = lax.dot_general(wvt_ref[...], h, NT_DIMS,
                                  preferred_element_type=F32).astype(vt_ref.dtype)


def _kv_up(proj, g_kv, w_k, wt_v, tabs, layer, *, bm):
    pos_blocks = SEQ // bm
    tab_spec = pl.BlockSpec((bm, MLA_QK_PAD), lambda i: (i % pos_blocks, 0))
    nk = MLA_HEADS * MLA_QK_PAD
    return pl.pallas_call(
        _kv_up_kernel,
        grid=(TOKENS // bm,),
        in_specs=[pl.BlockSpec((bm, MLA_KV_LORA), lambda i: (i, 2)),
                  pl.BlockSpec((bm, MLA_QK_PAD), lambda i: (i, 3)),
                  _layer_spec((1, MLA_KV_LORA), lambda i: (layer, 0, 0)),
                  _layer_spec((MLA_KV_LORA, nk), lambda i: (layer, 0, 0)),
                  _layer_spec((MLA_WIDTH, MLA_KV_LORA), lambda i: (layer, 0, 0)),
                  tab_spec, tab_spec, tab_spec],
        out_specs=[pl.BlockSpec((bm, nk), lambda i: (i, 0)),
                   pl.BlockSpec((MLA_WIDTH, bm), lambda i: (0, i))],
        out_shape=[jax.ShapeDtypeStruct((TOKENS, nk), BF16),
                   jax.ShapeDtypeStruct((MLA_WIDTH, TOKENS), BF16)],
        compiler_params=_cparams("parallel"),
        name="mla_kv_up",
    )(proj, proj, g_kv.reshape(DEPTH, 1, -1), w_k, wt_v, *tabs)


ATT_BQ = 256
ATT_BK = 256
ATT_NQ = SEQ // ATT_BQ
ATT_ONES = 16
ATT_GROUP = 2


def _mla_attn_kernel(qt_ref, k_ref, vt_ref, o_ref, m_sc, acc_sc):
    key = lax.broadcasted_iota(jnp.int32, (ATT_BK, ATT_BQ), 0)
    qry = lax.broadcasted_iota(jnp.int32, (ATT_BK, ATT_BQ), 1)
    causal = key <= qry
    tiles = [(i, j) for j in range(ATT_NQ) for i in range(j, ATT_NQ)]

    def scores(i, j):
        kj = k_ref[j * ATT_BK:(j + 1) * ATT_BK, :]
        qi = qt_ref[:, i * ATT_BQ:(i + 1) * ATT_BQ]
        return jnp.dot(kj, qi, preferred_element_type=F32)

    def softmax(i, j, s):
        if i == j:
            s = jnp.where(causal, s, -jnp.inf)
        m_blk = jnp.max(s, axis=0, keepdims=True)
        if j == 0:
            m_new, alpha = m_blk, None
        else:
            m_old = m_sc[i]
            m_new = jnp.maximum(m_old, m_blk)
            alpha = jnp.exp2(m_old - m_new)
        if i != j:
            m_sc[i] = m_new
        return jnp.exp2(s - m_new).astype(BF16), alpha

    ones_rows = jnp.ones((ATT_ONES, ATT_BK), BF16)

    def values(i, j, p, alpha):
        vj = jnp.concatenate([vt_ref[:, j * ATT_BK:(j + 1) * ATT_BK], ones_rows], axis=0)
        acc = jnp.dot(vj, p, preferred_element_type=F32)
        if alpha is not None:
            acc = alpha * acc_sc[i] + acc
        if i == j:
            out = acc[:MLA_V] / acc[MLA_V:MLA_V + 1]
            o_ref[i * ATT_BQ:(i + 1) * ATT_BQ, :] = out.T.astype(o_ref.dtype)
        else:
            acc_sc[i] = acc

    groups = [tiles[g:g + ATT_GROUP] for g in range(0, len(tiles), ATT_GROUP)]
    n = len(groups)

    def stage_scores(g):
        return [scores(*tile) for tile in groups[g]]

    def stage_softmax(g, s_list):
        return [softmax(*tile, s) for tile, s in zip(groups[g], s_list)]

    def stage_values(g, p_list):
        for tile, p_alpha in zip(groups[g], p_list):
            values(*tile, *p_alpha)

    s_ready = {g: stage_scores(g) for g in range(min(2, n))}
    p_ready = {0: stage_softmax(0, s_ready.pop(0))}
    for g in range(n):
        if g + 2 < n:
            s_ready[g + 2] = stage_scores(g + 2)
        if g + 1 < n:
            p_ready[g + 1] = stage_softmax(g + 1, s_ready.pop(g + 1))
        stage_values(g, p_ready.pop(g))


def _mla_attention(qt, k, vt):
    return pl.pallas_call(
        _mla_attn_kernel,
        grid=(BATCH, MLA_HEADS),
        in_specs=[pl.BlockSpec((MLA_QK_PAD, SEQ), lambda b, h: (h, b)),
                  pl.BlockSpec((SEQ, MLA_QK_PAD), lambda b, h: (b, h)),
                  pl.BlockSpec((MLA_V, SEQ), lambda b, h: (h, b))],
        out_specs=pl.BlockSpec((SEQ, MLA_V), lambda b, h: (b, h)),
        out_shape=jax.ShapeDtypeStruct((TOKENS, MLA_WIDTH), BF16),
        scratch_shapes=[pltpu.VMEM((ATT_NQ, 1, ATT_BQ), F32),
                        pltpu.VMEM((ATT_NQ, MLA_V + ATT_ONES, ATT_BQ), F32)],
        compiler_params=_cparams("parallel", "parallel"),
        name="mla_attention",
    )(qt, k, vt)


def _dil_attn_kernel(q_ref, k_ref, v_ref, o_ref, m_sc, l_sc, n_sc):
    row2 = lax.broadcasted_iota(jnp.int32, (BLOCK, 2 * BLOCK), 0)
    col2 = lax.broadcasted_iota(jnp.int32, (BLOCK, 2 * BLOCK), 1)
    dist = row2 + BLOCK - col2
    band = (dist >= 0) & (dist <= BLOCK)
    row1 = lax.broadcasted_iota(jnp.int32, (BLOCK, BLOCK), 0)
    col1 = lax.broadcasted_iota(jnp.int32, (BLOCK, BLOCK), 1)
    tri = row1 >= col1
    q_scale = DIL_HEAD_DIM ** -0.5 * math.log2(math.e)
    heads = LANES // DIL_HEAD_DIM
    n_patterns = len(DIL_PATTERNS)

    def rows_at(start, dil):
        return pl.ds(start, BLOCK) if dil == 1 else pl.ds(start, BLOCK, stride=dil)

    blocks = [(pi, dil, r, n) for pi, (_, dil) in enumerate(reversed(DIL_PATTERNS))
              for r in range(dil) for n in range(SEQ // dil // BLOCK)]

    assert heads == 2
    head0 = lax.broadcasted_iota(jnp.int32, (BLOCK, LANES), 1) < DIL_HEAD_DIM

    def load(pi, dil, r, n):
        rows = rows_at(r + dil * BLOCK * n, dil)
        q = q_ref[rows, :] * q_scale
        if n == 0:
            k, v, mask = k_ref[rows, :], v_ref[rows, :], tri
        else:
            prev = rows_at(r + dil * BLOCK * (n - 1), dil)
            k = jnp.concatenate([k_ref[prev, :], k_ref[rows, :]], axis=0)
            v = jnp.concatenate([v_ref[prev, :], v_ref[rows, :]], axis=0)
            mask = band
        v_head0 = lax.broadcasted_iota(jnp.int32, v.shape, 1) < DIL_HEAD_DIM
        q_heads = (jnp.where(head0, q, 0.0).astype(BF16), jnp.where(head0, 0.0, q).astype(BF16))
        v_heads = (jnp.where(v_head0, v, 1.0).astype(BF16),
                   jnp.where(v_head0, 1.0, v).astype(BF16))
        return rows, q_heads, k.astype(BF16), v_heads, mask

    def scores(blk, hd):
        return lax.dot_general(blk[1][hd], blk[2], NT_DIMS, preferred_element_type=F32)

    def softmax(blk, s):
        s = jnp.where(blk[4], s, -jnp.inf)
        m = jnp.max(s, axis=-1, keepdims=True)
        return jnp.exp2(s - m).astype(BF16), m

    def values(blk, hd, p):
        return jnp.dot(p, blk[3][hd], preferred_element_type=F32)

    def merge(pi, rows, parts):
        (m_a, ext_a), (m_b, ext_b) = parts
        m2 = jnp.where(head0, m_a, m_b)
        a2 = jnp.where(head0, ext_a, ext_b)
        l2 = pltpu.roll(jnp.where(head0, ext_b, ext_a), DIL_HEAD_DIM, axis=1)
        if pi > 0:
            m_old = m_sc[rows, :]
            m_new = jnp.maximum(m_old, m2)
            w_old, w_new = jnp.exp2(m_old - m_new), jnp.exp2(m2 - m_new)
            l2 = w_old * l_sc[rows, :] + w_new * l2
            a2 = w_old * n_sc[rows, :] + w_new * a2
            m2 = m_new
        if pi == n_patterns - 1:
            o_ref[rows, :] = a2 / l2
        else:
            m_sc[rows, :] = m2
            l_sc[rows, :] = l2
            n_sc[rows, :] = a2

    groups = [blocks[g:g + DIL_GROUP] for g in range(0, len(blocks), DIL_GROUP)]
    n = len(groups)

    def stage_scores(g):
        out = []
        for spec in groups[g]:
            blk = load(*spec)
            out.append((blk, [scores(blk, hd) for hd in range(heads)]))
        return out

    def stage_softmax(scored):
        return [(blk, [softmax(blk, s) for s in s_list]) for blk, s_list in scored]

    def stage_values(g, probs):
        for spec, (blk, p_list) in zip(groups[g], probs):
            merge(spec[0], blk[0],
                  [(m_rows, values(blk, hd, p)) for hd, (p, m_rows) in enumerate(p_list)])

    s_ready = {g: stage_scores(g) for g in range(min(2, n))}
    p_ready = {0: stage_softmax(s_ready.pop(0))}
    for g in range(n):
        if g + 2 < n:
            s_ready[g + 2] = stage_scores(g + 2)
        if g + 1 < n:
            p_ready[g + 1] = stage_softmax(s_ready.pop(g + 1))
        stage_values(g, p_ready.pop(g))


def _dil_attention(proj):
    def spec(col0):
        return pl.BlockSpec((SEQ, LANES), lambda b, hp: (b, col0 // LANES + hp))

    return pl.pallas_call(
        _dil_attn_kernel,
        grid=(BATCH, DIL_WIDTH // LANES),
        in_specs=[spec(DIL_Q_COL), spec(DIL_K_COL), spec(DIL_V_COL)],
        out_specs=pl.BlockSpec((SEQ, LANES), lambda b, hp: (b, hp)),
        out_shape=jax.ShapeDtypeStruct((TOKENS, DIL_WIDTH), F32),
        scratch_shapes=[pltpu.VMEM((SEQ, LANES), F32)] * 3,
        compiler_params=_cparams("parallel", "parallel"),
        name="dilated_attention",
    )(proj, proj, proj)


SLAB_GROUPS = LANES // SSM_GROUP
SSM_SLABS = SSM_GROUPS // SLAB_GROUPS
SSM_NCHUNK = SEQ // SSM_CHUNK
SSM_ROWS = BATCH * SSM_NCHUNK
SSM_ROW = SSM_CHUNK * LANES
SLAB_STATE = SLAB_GROUPS * SSM_STATE
STATE_TILES = SLAB_STATE // LANES
SUBLANES = 8


def _split_bf16(x):
    hi = x.astype(BF16)
    return hi, (x - hi.astype(F32)).astype(BF16)


def _expand_block_diag(x, width):
    rows, c = x.shape
    src = lax.broadcasted_iota(jnp.int32, (c, width), 0)
    dst = lax.broadcasted_iota(jnp.int32, (c, width), 1)
    repeat = jnp.where(dst % c == src, 1.0, 0.0).astype(BF16)
    wide = jnp.dot(x.astype(BF16), repeat, preferred_element_type=F32)
    row_g = lax.broadcasted_iota(jnp.int32, (rows, width), 0) // (rows // SLAB_GROUPS)
    col_g = lax.broadcasted_iota(jnp.int32, (rows, width), 1) // c
    return jnp.where(row_g == col_g, wide, 0.0).astype(BF16)


def _ssm_kernel(u_ref, dlag_ref, m1re_ref, m1im_ref, m2re_ref, m2im_ref,
                are_ref, aim_ref, d_ref, y_ref, fold_ref, st_ref, t0_sc, op_sc):
    lag_blocks = [_expand_block_diag(dlag_ref[0, tau], LANES) for tau in range(SSM_CHUNK)]
    no_block = jnp.zeros((LANES, LANES), BF16)
    for t in range(SSM_CHUNK):
        for s in range(SSM_CHUNK):
            t0_sc[t * LANES:(t + 1) * LANES, s * LANES:(s + 1) * LANES] = (
                lag_blocks[s - t] if s >= t else no_block)
    for which, ref in enumerate((m1re_ref, m1im_ref, m2re_ref, m2im_ref)):
        for t in range(SSM_CHUNK):
            op_sc[which, t * LANES:(t + 1) * LANES, :] = _expand_block_diag(ref[0, t], SLAB_STATE)

    for b in range(BATCH):
        rows = slice(b * SSM_NCHUNK, (b + 1) * SSM_NCHUNK)
        for t in range(SSM_CHUNK):
            fold_ref[rows, t * LANES:(t + 1) * LANES] = (
                u_ref[pl.ds(b * SEQ + t, SSM_NCHUNK, stride=SSM_CHUNK), :])
    u = fold_ref[...]
    ub = u.astype(BF16)

    loc_re = jnp.dot(ub, op_sc[0], preferred_element_type=F32)
    loc_im = jnp.dot(ub, op_sc[1], preferred_element_type=F32)
    for b in range(BATCH):
        rows = slice(b * SSM_NCHUNK, (b + 1) * SSM_NCHUNK)
        for k in range(STATE_TILES):
            lanes = slice(k * LANES, (k + 1) * LANES)
            st_ref[k, pl.ds(b, SSM_NCHUNK, stride=SUBLANES), :] = loc_re[rows, lanes]
            st_ref[k, pl.ds(BATCH + b, SSM_NCHUNK, stride=SUBLANES), :] = loc_im[rows, lanes]

    a_re, a_im = are_ref[0], aim_ref[0]
    upper = lax.broadcasted_iota(jnp.int32, (SUBLANES, LANES), 0) < BATCH
    mul_same, mul_swap = [], []
    for k in range(STATE_TILES):
        lanes = slice(k * LANES, (k + 1) * LANES)
        mul_same.append(jnp.broadcast_to(a_re[:, lanes], (SUBLANES, LANES)))
        im = jnp.broadcast_to(a_im[:, lanes], (SUBLANES, LANES))
        mul_swap.append(jnp.where(upper, -im, im))

    def chunk_step(c, state):
        r0 = pl.multiple_of(c * SUBLANES, SUBLANES)
        new = []
        for k in range(STATE_TILES):
            loc = st_ref[k, pl.ds(r0, SUBLANES), :]
            st_ref[k, pl.ds(r0, SUBLANES), :] = state[k]
            swapped = pltpu.roll(state[k], BATCH, axis=0)
            new.append(mul_same[k] * state[k] + mul_swap[k] * swapped + loc)
        return tuple(new)

    zero = jnp.zeros((SUBLANES, LANES), F32)
    lax.fori_loop(0, SSM_NCHUNK, chunk_step, (zero,) * STATE_TILES, unroll=4)

    def entering(offset):
        return jnp.concatenate(
            [jnp.concatenate([st_ref[k, pl.ds(offset + b, SSM_NCHUNK, stride=SUBLANES), :]
                              for k in range(STATE_TILES)], axis=1)
             for b in range(BATCH)], axis=0)

    y = jnp.dot(ub, t0_sc[...], preferred_element_type=F32) + u * d_ref[0]
    for offset, which in ((0, 2), (BATCH, 3)):
        hi, lo = _split_bf16(entering(offset))
        m2t = op_sc[which]
        y = y + (lax.dot_general(hi, m2t, NT_DIMS, preferred_element_type=F32)
                 + lax.dot_general(lo, m2t, NT_DIMS, preferred_element_type=F32))
    fold_ref[...] = jax.nn.gelu(y, approximate=True)

    for b in range(BATCH):
        rows = slice(b * SSM_NCHUNK, (b + 1) * SSM_NCHUNK)
        for t in range(SSM_CHUNK):
            y_ref[pl.ds(b * SEQ + t, SSM_NCHUNK, stride=SSM_CHUNK), :] = (
                fold_ref[rows, t * LANES:(t + 1) * LANES])


def _ssm_scan(proj, mats, layer):
    dlag, m1re, m1im, m2re, m2im, a_re, a_im, dvec = mats

    def spec(r, c):
        return _layer_spec((1, r, c), lambda s: (layer, s, 0, 0))

    def compact(c):
        return _layer_spec((1, SSM_CHUNK, LANES, c), lambda s: (layer, s, 0, 0, 0))

    return pl.pallas_call(
        _ssm_kernel,
        grid=(SSM_SLABS,),
        in_specs=[pl.BlockSpec((TOKENS, LANES), lambda s: (0, SSM_U_COL // LANES + s)),
                  compact(SSM_GROUP),
                  compact(SSM_STATE), compact(SSM_STATE),
                  compact(SSM_STATE), compact(SSM_STATE),
                  spec(1, SLAB_STATE), spec(1, SLAB_STATE), spec(1, SSM_ROW)],
        out_specs=pl.BlockSpec((TOKENS, LANES), lambda s: (0, s)),
        out_shape=jax.ShapeDtypeStruct((TOKENS, SSM_WIDTH), F32),
        scratch_shapes=[pltpu.VMEM((SSM_ROWS, SSM_ROW), F32),
                        pltpu.VMEM((STATE_TILES, SUBLANES * SSM_NCHUNK, LANES), F32),
                        pltpu.VMEM((SSM_ROW, SSM_ROW), BF16),
                        pltpu.VMEM((4, SSM_ROW, SLAB_STATE), BF16)],
        compiler_params=_cparams("arbitrary"),
        name="ssm_chunk_scan",
    )(proj, dlag, m1re, m1im, m2re, m2im, a_re, a_im, dvec)


def _ssm_matrices(a_re, a_im, b_re, b_im, c_re, c_im, d_skip, log_dt):
    lam_re = jnp.minimum(a_re, -1e-4)
    lam_im = a_im
    dt = jnp.exp(log_dt)[:, None]
    mag = jnp.exp(lam_re * dt)
    ab_re, ab_im = mag * jnp.cos(lam_im * dt), mag * jnp.sin(lam_im * dt)
    n_re, n_im = ab_re - 1.0, ab_im
    den = lam_re * lam_re + lam_im * lam_im
    f_re = (n_re * lam_re + n_im * lam_im) / den
    f_im = (n_im * lam_re - n_re * lam_im) / den
    bb_re = f_re[..., None] * b_re - f_im[..., None] * b_im
    bb_im = f_re[..., None] * b_im + f_im[..., None] * b_re
    p_re, p_im = [jnp.ones_like(ab_re)], [jnp.zeros_like(ab_im)]
    for _ in range(SSM_CHUNK):
        p_re.append(p_re[-1] * ab_re - p_im[-1] * ab_im)
        p_im.append(p_re[-2] * ab_im + p_im[-1] * ab_re)
    pw_re, pw_im = jnp.stack(p_re), jnp.stack(p_im)

    cb_re = (c_re[:, None, :, :] * bb_re.transpose(0, 2, 1)[:, :, None, :]
             - c_im[:, None, :, :] * bb_im.transpose(0, 2, 1)[:, :, None, :])
    cb_im = (c_re[:, None, :, :] * bb_im.transpose(0, 2, 1)[:, :, None, :]
             + c_im[:, None, :, :] * bb_re.transpose(0, 2, 1)[:, :, None, :])
    lag_re = pw_re[:SSM_CHUNK].transpose(1, 0, 2)[:, :, None, None, :]
    lag_im = pw_im[:SSM_CHUNK].transpose(1, 0, 2)[:, :, None, None, :]
    kern = jnp.sum(lag_re * cb_re[:, None] - lag_im * cb_im[:, None], axis=-1)
    t_idx = np.arange(SSM_CHUNK)

    back = pw_re[SSM_CHUNK - 1 - t_idx], pw_im[SSM_CHUNK - 1 - t_idx]
    bre_t, bim_t = bb_re.transpose(0, 2, 1), bb_im.transpose(0, 2, 1)
    m1re = (back[0].transpose(1, 0, 2)[:, :, None, :] * bre_t[:, None]
            - back[1].transpose(1, 0, 2)[:, :, None, :] * bim_t[:, None])
    m1im = (back[0].transpose(1, 0, 2)[:, :, None, :] * bim_t[:, None]
            + back[1].transpose(1, 0, 2)[:, :, None, :] * bre_t[:, None])

    fwd_re = pw_re[1:].transpose(1, 0, 2)[:, :, None, :]
    fwd_im = pw_im[1:].transpose(1, 0, 2)[:, :, None, :]
    ca_re = c_re[:, None] * fwd_re - c_im[:, None] * fwd_im
    ca_im = c_re[:, None] * fwd_im + c_im[:, None] * fwd_re

    def rows_by_group(x):
        k, r, c = x.shape[1:]
        x = x.reshape(SSM_SLABS, SLAB_GROUPS, k, r, c).transpose(0, 2, 1, 3, 4)
        return x.reshape(SSM_SLABS, k, SLAB_GROUPS * r, c)

    a_step_re = pw_re[SSM_CHUNK].reshape(SSM_SLABS, 1, SLAB_STATE)
    a_step_im = pw_im[SSM_CHUNK].reshape(SSM_SLABS, 1, SLAB_STATE)
    dvec = jnp.concatenate([d_skip.reshape(SSM_SLABS, 1, LANES)] * SSM_CHUNK, axis=-1)
    return (rows_by_group(kern), rows_by_group(m1re), rows_by_group(m1im),
            rows_by_group(ca_re), rows_by_group(-ca_im), a_step_re, a_step_im, dvec)


def _glu_kernel(y_ref, w_ref, b_ref, o_ref):
    z = jnp.dot(y_ref[...].astype(BF16), _bf16(w_ref[...]),
                preferred_element_type=F32) + b_ref[...]
    o_ref[...] = (z[:, :SSM_WIDTH] * jax.nn.sigmoid(z[:, SSM_WIDTH:])).astype(o_ref.dtype)


def _glu(y, w_glu, b_glu, layer, *, bm):
    return pl.pallas_call(
        _glu_kernel,
        grid=(TOKENS // bm,),
        in_specs=[pl.BlockSpec((bm, SSM_WIDTH), lambda i: (i, 0)),
                  _layer_spec((SSM_WIDTH, 2 * SSM_WIDTH), lambda i: (layer, 0, 0)),
                  _layer_spec((1, 2 * SSM_WIDTH), lambda i: (layer, 0, 0))],
        out_specs=pl.BlockSpec((bm, SSM_WIDTH), lambda i: (i, 0)),
        out_shape=jax.ShapeDtypeStruct((TOKENS, SSM_WIDTH), BF16),
        compiler_params=_cparams("parallel"),
        name="ssm_glu",
    )(y, w_glu, b_glu.reshape(DEPTH, 1, -1))


def _out_proj_kernel(ya_ref, yb_ref, yc_ref, ga_ref, gb_ref, gc_ref, w_ref, x_ref,
                     o_ref, h_ref):
    parts = ((ya_ref, ga_ref[...]), (yb_ref, gb_ref[...]), (yc_ref, gc_ref[...]))

    def normed_chunk(rows):
        return jnp.concatenate([_rms_rows(y_ref[rows, :].astype(F32), g).astype(BF16)
                                for y_ref, g in parts], axis=-1)

    def make_emit():
        w = _bf16(w_ref[...])

        def emit(rows, h):
            o_ref[rows, :] = x_ref[rows, :] + jnp.dot(h, w, preferred_element_type=F32)
        return emit

    _first_step_by_chunks(h_ref, normed_chunk, make_emit)


def _out_proj(y_mla, y_ssm, y_dil, g_mla, g_ssm, g_dil, w_o, x, layer, *, bm, bn):
    def rows(width):
        return pl.BlockSpec((bm, width), lambda i, j: (i, 0))

    def gain(width):
        return _layer_spec((1, width), lambda i, j: (layer, 0, 0))

    return pl.pallas_call(
        _out_proj_kernel,
        grid=(TOKENS // bm, D_MODEL // bn),
        in_specs=[rows(MLA_WIDTH), rows(SSM_WIDTH), rows(DIL_WIDTH),
                  gain(MLA_WIDTH), gain(SSM_WIDTH), gain(DIL_WIDTH),
                  _layer_spec((D_MODEL, bn), lambda i, j: (layer, 0, j)),
                  pl.BlockSpec((bm, bn), lambda i, j: (i, j))],
        out_specs=pl.BlockSpec((bm, bn), lambda i, j: (i, j)),
        out_shape=jax.ShapeDtypeStruct((TOKENS, D_MODEL), F32),
        scratch_shapes=[pltpu.VMEM((bm, D_MODEL), BF16)],
        compiler_params=_cparams("parallel", "arbitrary"),
        name="out_proj",
    )(y_mla, y_ssm, y_dil, g_mla.reshape(DEPTH, 1, -1), g_ssm.reshape(DEPTH, 1, -1),
      g_dil.reshape(DEPTH, 1, -1), w_o, x)


def _ffn_up_kernel(x_ref, g_ref, wg_ref, wu_ref, wd_ref, o_ref, wd_bf16_ref, h_ref):
    g = g_ref[...]
    wd_bf16_ref[...] = wd_ref[...].astype(BF16)

    def make_emit():
        wg, wu = _bf16(wg_ref[...]), _bf16(wu_ref[...])

        def emit(rows, h):
            gate = jnp.dot(h, wg, preferred_element_type=F32)
            up = jnp.dot(h, wu, preferred_element_type=F32)
            o_ref[rows, :] = (jax.nn.silu(gate) * up).astype(o_ref.dtype)
        return emit

    _first_step_by_chunks(h_ref, lambda rows: _rms_rows(x_ref[rows, :], g).astype(BF16),
                          make_emit)


def _ffn_up(x, g, w_gate, w_up, w_down, layer, *, bm, bn):
    col_steps = D_FF // bn
    wd_rows = D_FF // (TOKENS // bm * col_steps)
    return pl.pallas_call(
        _ffn_up_kernel,
        grid=(TOKENS // bm, col_steps),
        in_specs=[pl.BlockSpec((bm, D_MODEL), lambda i, j: (i, 0)),
                  _layer_spec((1, D_MODEL), lambda i, j: (layer, 0, 0)),
                  _layer_spec((D_MODEL, bn), lambda i, j: (layer, 0, j)),
                  _layer_spec((D_MODEL, bn), lambda i, j: (layer, 0, j)),
                  _layer_spec((wd_rows, D_MODEL), lambda i, j: (layer, i * col_steps + j, 0))],
        out_specs=[pl.BlockSpec((bm, bn), lambda i, j: (i, j)),
                   pl.BlockSpec((wd_rows, D_MODEL), lambda i, j: (i * col_steps + j, 0))],
        out_shape=[jax.ShapeDtypeStruct((TOKENS, D_FF), BF16),
                   jax.ShapeDtypeStruct((D_FF, D_MODEL), BF16)],
        scratch_shapes=[pltpu.VMEM((bm, D_MODEL), BF16)],
        compiler_params=_cparams("arbitrary", "arbitrary"),
        name="ffn_gate_up",
    )(x, g.reshape(DEPTH, 1, -1), w_gate, w_up, w_down)


def _ffn_down_kernel(a_ref, w_ref, x_ref, o_ref):
    o_ref[...] = x_ref[...] + jnp.dot(a_ref[...], _bf16(w_ref[...]),
                                      preferred_element_type=F32)


def _ffn_down(a, w_down, x, *, bm, bn):
    return pl.pallas_call(
        _ffn_down_kernel,
        grid=(TOKENS // bm, D_MODEL // bn),
        in_specs=[pl.BlockSpec((bm, D_FF), lambda i, j: (i, 0)),
                  pl.BlockSpec((D_FF, bn), lambda i, j: (0, j)),
                  pl.BlockSpec((bm, bn), lambda i, j: (i, j))],
        out_specs=pl.BlockSpec((bm, bn), lambda i, j: (i, j)),
        out_shape=jax.ShapeDtypeStruct((TOKENS, D_MODEL), F32),
        compiler_params=_cparams("parallel", "arbitrary"),
        name="ffn_down",
    )(a, w_down, x)


def _final_norm_kernel(x_ref, g_ref, o_ref):
    o_ref[...] = _rms_rows(x_ref[...], g_ref[...])


def _final_norm(x, g, *, bm):
    return pl.pallas_call(
        _final_norm_kernel,
        grid=(TOKENS // bm,),
        in_specs=[pl.BlockSpec((bm, D_MODEL), lambda i: (i, 0)),
                  pl.BlockSpec((1, D_MODEL), lambda i: (0, 0))],
        out_specs=pl.BlockSpec((bm, D_MODEL), lambda i: (i, 0)),
        out_shape=jax.ShapeDtypeStruct((TOKENS, D_MODEL), F32),
        compiler_params=_cparams("parallel"),
        name="final_norm",
    )(x, g.reshape(1, -1))


def _pad_w_in(w_in):
    lyr, d, _ = w_in.shape
    z = lambda n: jnp.zeros((lyr, d, n), BF16)
    w = w_in.astype(BF16)
    return jnp.concatenate([w[..., :768], z(128), w[..., 768:832], z(64), w[..., 832:]],
                           axis=-1)


def _pad_wt_uq(w_uq):
    lyr = w_uq.shape[0]
    w = w_uq.reshape(lyr, MLA_Q_LORA, MLA_HEADS, MLA_NOPE + MLA_ROPE)
    w = jnp.pad(w, ((0, 0), (0, 0), (0, 0), (0, MLA_QK_PAD - MLA_NOPE - MLA_ROPE)))
    w = w.reshape(lyr, MLA_Q_LORA, MLA_HEADS * MLA_QK_PAD)
    return w.transpose(0, 2, 1).astype(BF16)


def _split_w_ukv(w_ukv):
    lyr = w_ukv.shape[0]
    w = w_ukv.reshape(lyr, MLA_KV_LORA, MLA_HEADS, MLA_NOPE + MLA_V)
    wk = jnp.pad(w[..., :MLA_NOPE], ((0, 0), (0, 0), (0, 0), (0, MLA_QK_PAD - MLA_NOPE)))
    wk = wk.reshape(lyr, MLA_KV_LORA, -1).astype(BF16)
    wv = w[..., MLA_NOPE:].reshape(lyr, MLA_KV_LORA, -1)
    return wk, wv.transpose(0, 2, 1).astype(BF16)


def _rope_angles():
    inv_freq = ROPE_THETA ** (-jnp.arange(ROPE_HALF, dtype=F32) / ROPE_HALF)
    ang = jnp.arange(SEQ, dtype=F32)[:, None] * inv_freq[None, :]
    return jnp.cos(ang), jnp.sin(ang)


def _rope_tables_k():
    cos, sin = _rope_angles()
    one = jnp.ones((SEQ, MLA_NOPE), F32)
    z = lambda n: jnp.zeros((SEQ, n), F32)
    tail = MLA_QK_PAD - MLA_NOPE - MLA_ROPE
    cos_t = jnp.concatenate([one, cos, cos, z(tail)], axis=1)
    sin_a = jnp.concatenate([z(MLA_NOPE), -sin, z(ROPE_HALF), z(tail)], axis=1)
    sin_b = jnp.concatenate([z(MLA_NOPE), z(ROPE_HALF), sin, z(tail)], axis=1)
    return cos_t, sin_a, sin_b


def kernel(x, g_mix, w_in, g_q, w_uq, g_kv, w_ukv, a_re, a_im, b_re, b_im, c_re, c_im,
           d_skip, log_dt, w_glu, b_glu, g_out_mla, g_out_ssm, g_out_dil, w_o,
           g_ffn, w_gate, w_up, w_down, g_final):
    x = x.reshape(TOKENS, D_MODEL)
    w_in_p = _pad_w_in(w_in)
    wt_q = _pad_wt_uq(w_uq)
    w_k, wt_v = _split_w_ukv(w_ukv)
    w_o_b = w_o.astype(BF16)
    q_scale = (MLA_NOPE + MLA_ROPE) ** -0.5 * math.log2(math.e)
    cos, sin = _rope_angles()
    q_cos_t, q_sin_t = (cos * q_scale).T, (sin * q_scale).T
    k_tabs = _rope_tables_k()
    ssm_mats = jax.vmap(_ssm_matrices)(a_re, a_im, b_re, b_im, c_re, c_im, d_skip, log_dt)

    for l in range(DEPTH):
        proj = _norm_matmul(x, g_mix, w_in_p, l, bm=1024, bn=1024, out_dtype=F32,
                            name="in_proj")
        qt = _q_up(proj, g_q, wt_q, q_cos_t, q_sin_t, l, q_scale, bm=512)
        k, vt = _kv_up(proj, g_kv, w_k, wt_v, k_tabs, l, bm=512)
        y_mla = _mla_attention(qt, k, vt)
        y_ssm = _glu(_ssm_scan(proj, ssm_mats, l), w_glu, b_glu, l, bm=1024)
        y_dil = _dil_attention(proj)
        x = _out_proj(y_mla, y_ssm, y_dil, g_out_mla, g_out_ssm, g_out_dil,
                      w_o_b, x, l, bm=1024, bn=1024)
        act, w_down_b = _ffn_up(x, g_ffn, w_gate, w_up, w_down, l, bm=1024, bn=512)
        x = _ffn_down(act, w_down_b, x, bm=1024, bn=512)
    out = _final_norm(x, g_final, bm=512)
    return out.reshape(BATCH, SEQ, D_MODEL)
```

```python
import functools
import math

import jax
import jax.numpy as jnp
import numpy as np
from jax import lax
from jax.experimental import pallas as pl
from jax.experimental.pallas import tpu as pltpu

F32 = jnp.float32
BF16 = jnp.bfloat16

D_MODEL = 2048
BATCH = 4
SEQ = 2048
DEPTH = 4
TOKENS = BATCH * SEQ

MLA_HEADS = 8
MLA_NOPE = 128
MLA_ROPE = 64
MLA_V = 128
MLA_Q_LORA = 512
MLA_KV_LORA = 256
MLA_WIDTH = MLA_HEADS * MLA_V
MLA_QK_PAD = 256
ROPE_THETA = 10000.0
ROPE_HALF = MLA_ROPE // 2

SSM_WIDTH = 512
SSM_GROUP = 16
SSM_GROUPS = 32
SSM_STATE = 64
SSM_CHUNK = 8

DIL_WIDTH = 512
DIL_HEAD_DIM = 64
DIL_HEADS = 8
DIL_PATTERNS = ((128, 1), (512, 4), (2048, 16))
BLOCK = 128
DIL_GROUP = 1

IN_PAD = 3072
DIL_Q_COL = 1536
DIL_K_COL = 2048
DIL_V_COL = 2560
SSM_U_COL = 1024
D_FF = 5632
NORM_EPS = 1e-6

LANES = 128
VMEM_LIMIT_BYTES = 56 * 1024 * 1024

NT_DIMS = (((1,), (1,)), ((), ()))


def _cparams(*semantics):
    return pltpu.CompilerParams(dimension_semantics=semantics,
                                vmem_limit_bytes=VMEM_LIMIT_BYTES)


def _rms_rows(x, g):
    ms = jnp.mean(x * x, axis=-1, keepdims=True)
    return x * lax.rsqrt(ms + NORM_EPS) * g


NORM_CHUNK = 256


def _first_step_by_chunks(h_ref, normed_chunk, make_emit):
    @pl.when(pl.program_id(1) == 0)
    def _():
        emit = make_emit()
        for r in range(0, h_ref.shape[0], NORM_CHUNK):
            rows = slice(r, r + NORM_CHUNK)
            h = normed_chunk(rows)
            h_ref[rows, :] = h
            emit(rows, h)

    @pl.when(pl.program_id(1) != 0)
    def _():
        make_emit()(slice(None), h_ref[...])


def _bf16(w):
    return w if w.dtype == BF16 else w.astype(BF16)


def _layer_spec(shape, index_map):
    return pl.BlockSpec((None,) + tuple(shape), index_map)


def _norm_matmul_kernel(x_ref, g_ref, w_ref, o_ref, h_ref):
    g = g_ref[...]

    def make_emit():
        w = _bf16(w_ref[...])

        def emit(rows, h):
            o_ref[rows, :] = jnp.dot(h, w, preferred_element_type=F32).astype(o_ref.dtype)
        return emit

    _first_step_by_chunks(h_ref, lambda rows: _rms_rows(x_ref[rows, :], g).astype(BF16),
                          make_emit)


def _norm_matmul(x, g, w, layer, *, bm, bn, out_dtype, name):
    m, k = x.shape
    n = w.shape[2]
    return pl.pallas_call(
        _norm_matmul_kernel,
        grid=(m // bm, n // bn),
        in_specs=[pl.BlockSpec((bm, k), lambda i, j: (i, 0)),
                  _layer_spec((1, k), lambda i, j: (layer, 0, 0)),
                  _layer_spec((k, bn), lambda i, j: (layer, 0, j))],
        out_specs=pl.BlockSpec((bm, bn), lambda i, j: (i, j)),
        out_shape=jax.ShapeDtypeStruct((m, n), out_dtype),
        scratch_shapes=[pltpu.VMEM((bm, k), BF16)],
        compiler_params=_cparams("parallel", "arbitrary"),
        name=name,
    )(x, g.reshape(DEPTH, 1, k), w)


def _q_up_kernel(c_ref, g_ref, wt_ref, cos_ref, sin_ref, o_ref, *, scale):
    h = _rms_rows(c_ref[...], g_ref[...]).astype(BF16)
    cos, sin = cos_ref[...], sin_ref[...]

    def project(hd):
        rows = slice(hd * MLA_QK_PAD, (hd + 1) * MLA_QK_PAD)
        return lax.dot_general(wt_ref[rows, :], h, NT_DIMS, preferred_element_type=F32)

    def finish(hd, qt):
        r0 = hd * MLA_QK_PAD
        n1, n2, n3 = MLA_NOPE, MLA_NOPE + ROPE_HALF, MLA_NOPE + MLA_ROPE
        x1, x2 = qt[n1:n2], qt[n2:n3]
        o_ref[r0:r0 + n1, :] = (qt[:n1] * scale).astype(o_ref.dtype)
        o_ref[r0 + n1:r0 + n2, :] = (x1 * cos - x2 * sin).astype(o_ref.dtype)
        o_ref[r0 + n2:r0 + n3, :] = (x2 * cos + x1 * sin).astype(o_ref.dtype)
        o_ref[r0 + n3:r0 + MLA_QK_PAD, :] = qt[n3:].astype(o_ref.dtype)

    qt_next = project(0)
    for hd in range(MLA_HEADS):
        qt = qt_next
        if hd + 1 < MLA_HEADS:
            qt_next = project(hd + 1)
        finish(hd, qt)


def _q_up(proj, g_q, wt_q, cos_t, sin_t, layer, scale, *, bm):
    pos_blocks = SEQ // bm
    tab_spec = pl.BlockSpec((ROPE_HALF, bm), lambda i: (0, i % pos_blocks))
    n = MLA_HEADS * MLA_QK_PAD
    return pl.pallas_call(
        functools.partial(_q_up_kernel, scale=scale),
        grid=(TOKENS // bm,),
        in_specs=[pl.BlockSpec((bm, MLA_Q_LORA), lambda i: (i, 0)),
                  _layer_spec((1, MLA_Q_LORA), lambda i: (layer, 0, 0)),
                  _layer_spec((n, MLA_Q_LORA), lambda i: (layer, 0, 0)),
                  tab_spec, tab_spec],
        out_specs=pl.BlockSpec((n, bm), lambda i: (0, i)),
        out_shape=jax.ShapeDtypeStruct((n, TOKENS), BF16),
        compiler_params=_cparams("parallel"),
        name="mla_q_up",
    )(proj, g_q.reshape(DEPTH, 1, -1), wt_q, cos_t, sin_t)


def _rope_pad(x, cos_t, sin_a, sin_b):
    up = pltpu.roll(x, MLA_QK_PAD - ROPE_HALF, axis=1)
    dn = pltpu.roll(x, ROPE_HALF, axis=1)
    return x * cos_t + up * sin_a + dn * sin_b


def _kv_up_kernel(c_ref, kr_ref, g_ref, wk_ref, wvt_ref, cos_ref, sina_ref, sinb_ref,
                  k_ref, vt_ref):
    h = _rms_rows(c_ref[...], g_ref[...]).astype(BF16)
    kn = jnp.dot(h, wk_ref[...], preferred_element_type=F32)
    k_pe = _rope_pad(kr_ref[...], cos_ref[...], sina_ref[...], sinb_ref[...])
    for hd in range(MLA_HEADS):
        sl = slice(hd * MLA_QK_PAD, (hd + 1) * MLA_QK_PAD)
        k_ref[:, sl] = (kn[:, sl] + k_pe).astype(k_ref.dtype)
    vt_ref[...] = lax.dot_general(wvt_ref[...], h, NT_DIMS,
                                  preferred_element_type=F32).astype(vt_ref.dtype)


def _kv_up(proj, g_kv, w_k, wt_v, tabs, layer, *, bm):
    pos_blocks = SEQ // bm
    tab_spec = pl.BlockSpec((bm, MLA_QK_PAD), lambda i: (i % pos_blocks, 0))
    nk = MLA_HEADS * MLA_QK_PAD
    return pl.pallas_call(
        _kv_up_kernel,
        grid=(TOKENS // bm,),
        in_specs=[pl.BlockSpec((bm, MLA_KV_LORA), lambda i: (i, 2)),
                  pl.BlockSpec((bm, MLA_QK_PAD), lambda i: (i, 3)),
                  _layer_spec((1, MLA_KV_LORA), lambda i: (layer, 0, 0)),
                  _layer_spec((MLA_KV_LORA, nk), lambda i: (layer, 0, 0)),
                  _layer_spec((MLA_WIDTH, MLA_KV_LORA), lambda i: (layer, 0, 0)),
                  tab_spec, tab_spec, tab_spec],
        out_specs=[pl.BlockSpec((bm, nk), lambda i: (i, 0)),
                   pl.BlockSpec((MLA_WIDTH, bm), lambda i: (0, i))],
        out_shape=[jax.ShapeDtypeStruct((TOKENS, nk), BF16),
                   jax.ShapeDtypeStruct((MLA_WIDTH, TOKENS), BF16)],
        compiler_params=_cparams("parallel"),
        name="mla_kv_up",
    )(proj, proj, g_kv.reshape(DEPTH, 1, -1), w_k, wt_v, *tabs)


ATT_BQ = 256
ATT_BK = 256
ATT_NQ = SEQ // ATT_BQ
ATT_ONES = 16
ATT_GROUP = 2


def _mla_attn_kernel(qt_ref, k_ref, vt_ref, o_ref, m_sc, acc_sc):
    key = lax.broadcasted_iota(jnp.int32, (ATT_BK, ATT_BQ), 0)
    qry = lax.broadcasted_iota(jnp.int32, (ATT_BK, ATT_BQ), 1)
    causal = key <= qry
    tiles = [(i, j) for j in range(ATT_NQ) for i in range(j, ATT_NQ)]

    def scores(i, j):
        kj = k_ref[j * ATT_BK:(j + 1) * ATT_BK, :]
        qi = qt_ref[:, i * ATT_BQ:(i + 1) * ATT_BQ]
        return jnp.dot(kj, qi, preferred_element_type=F32)

    def softmax(i, j, s):
        if i == j:
            s = jnp.where(causal, s, -jnp.inf)
        m_blk = jnp.max(s, axis=0, keepdims=True)
        if j == 0:
            m_new, alpha = m_blk, None
        else:
            m_old = m_sc[i]
            m_new = jnp.maximum(m_old, m_blk)
            alpha = jnp.exp2(m_old - m_new)
        if i != j:
            m_sc[i] = m_new
        return jnp.exp2(s - m_new).astype(BF16), alpha

    ones_rows = jnp.ones((ATT_ONES, ATT_BK), BF16)

    def values(i, j, p, alpha):
        vj = jnp.concatenate([vt_ref[:, j * ATT_BK:(j + 1) * ATT_BK], ones_rows], axis=0)
        acc = jnp.dot(vj, p, preferred_element_type=F32)
        if alpha is not None:
            acc = alpha * acc_sc[i] + acc
        if i == j:
            out = acc[:MLA_V] / acc[MLA_V:MLA_V + 1]
            o_ref[i * ATT_BQ:(i + 1) * ATT_BQ, :] = out.T.astype(o_ref.dtype)
        else:
            acc_sc[i] = acc

    groups = [tiles[g:g + ATT_GROUP] for g in range(0, len(tiles), ATT_GROUP)]
    n = len(groups)

    def stage_scores(g):
        return [scores(*tile) for tile in groups[g]]

    def stage_softmax(g, s_list):
        return [softmax(*tile, s) for tile, s in zip(groups[g], s_list)]

    def stage_values(g, p_list):
        for tile, p_alpha in zip(groups[g], p_list):
            values(*tile, *p_alpha)

    s_ready = {g: stage_scores(g) for g in range(min(2, n))}
    p_ready = {0: stage_softmax(0, s_ready.pop(0))}
    for g in range(n):
        if g + 2 < n:
            s_ready[g + 2] = stage_scores(g + 2)
        if g + 1 < n:
            p_ready[g + 1] = stage_softmax(g + 1, s_ready.pop(g + 1))
        stage_values(g, p_ready.pop(g))


def _mla_attention(qt, k, vt):
    return pl.pallas_call(
        _mla_attn_kernel,
        grid=(BATCH, MLA_HEADS),
        in_specs=[pl.BlockSpec((MLA_QK_PAD, SEQ), lambda b, h: (h, b)),
                  pl.BlockSpec((SEQ, MLA_QK_PAD), lambda b, h: (b, h)),
                  pl.BlockSpec((MLA_V, SEQ), lambda b, h: (h, b))],
        out_specs=pl.BlockSpec((SEQ, MLA_V), lambda b, h: (b, h)),
        out_shape=jax.ShapeDtypeStruct((TOKENS, MLA_WIDTH), BF16),
        scratch_shapes=[pltpu.VMEM((ATT_NQ, 1, ATT_BQ), F32),
                        pltpu.VMEM((ATT_NQ, MLA_V + ATT_ONES, ATT_BQ), F32)],
        compiler_params=_cparams("parallel", "parallel"),
        name="mla_attention",
    )(qt, k, vt)


def _dil_attn_kernel(q_ref, k_ref, v_ref, o_ref, m_sc, l_sc, n_sc):
    row2 = lax.broadcasted_iota(jnp.int32, (BLOCK, 2 * BLOCK), 0)
    col2 = lax.broadcasted_iota(jnp.int32, (BLOCK, 2 * BLOCK), 1)
    dist = row2 + BLOCK - col2
    band = (dist >= 0) & (dist <= BLOCK)
    row1 = lax.broadcasted_iota(jnp.int32, (BLOCK, BLOCK), 0)
    col1 = lax.broadcasted_iota(jnp.int32, (BLOCK, BLOCK), 1)
    tri = row1 >= col1
    q_scale = DIL_HEAD_DIM ** -0.5 * math.log2(math.e)
    heads = LANES // DIL_HEAD_DIM
    n_patterns = len(DIL_PATTERNS)

    def rows_at(start, dil):
        return pl.ds(start, BLOCK) if dil == 1 else pl.ds(start, BLOCK, stride=dil)

    blocks = [(pi, dil, r, n) for pi, (_, dil) in enumerate(reversed(DIL_PATTERNS))
              for r in range(dil) for n in range(SEQ // dil // BLOCK)]

    assert heads == 2
    head0 = lax.broadcasted_iota(jnp.int32, (BLOCK, LANES), 1) < DIL_HEAD_DIM

    def load(pi, dil, r, n):
        rows = rows_at(r + dil * BLOCK * n, dil)
        q = q_ref[rows, :] * q_scale
        if n == 0:
            k, v, mask = k_ref[rows, :], v_ref[rows, :], tri
        else:
            prev = rows_at(r + dil * BLOCK * (n - 1), dil)
            k = jnp.concatenate([k_ref[prev, :], k_ref[rows, :]], axis=0)
            v = jnp.concatenate([v_ref[prev, :], v_ref[rows, :]], axis=0)
            mask = band
        v_head0 = lax.broadcasted_iota(jnp.int32, v.shape, 1) < DIL_HEAD_DIM
        q_heads = (jnp.where(head0, q, 0.0).astype(BF16), jnp.where(head0, 0.0, q).astype(BF16))
        v_heads = (jnp.where(v_head0, v, 1.0).astype(BF16),
                   jnp.where(v_head0, 1.0, v).astype(BF16))
        return rows, q_heads, k.astype(BF16), v_heads, mask

    def scores(blk, hd):
        return lax.dot_general(blk[1][hd], blk[2], NT_DIMS, preferred_element_type=F32)

    def softmax(blk, s):
        s = jnp.where(blk[4], s, -jnp.inf)
        m = jnp.max(s, axis=-1, keepdims=True)
        return jnp.exp2(s - m).astype(BF16), m

    def values(blk, hd, p):
        return jnp.dot(p, blk[3][hd], preferred_element_type=F32)

    def merge(pi, rows, parts):
        (m_a, ext_a), (m_b, ext_b) = parts
        m2 = jnp.where(head0, m_a, m_b)
        a2 = jnp.where(head0, ext_a, ext_b)
        l2 = pltpu.roll(jnp.where(head0, ext_b, ext_a), DIL_HEAD_DIM, axis=1)
        if pi > 0:
            m_old = m_sc[rows, :]
            m_new = jnp.maximum(m_old, m2)
            w_old, w_new = jnp.exp2(m_old - m_new), jnp.exp2(m2 - m_new)
            l2 = w_old * l_sc[rows, :] + w_new * l2
            a2 = w_old * n_sc[rows, :] + w_new * a2
            m2 = m_new
        if pi == n_patterns - 1:
            o_ref[rows, :] = a2 / l2
        else:
            m_sc[rows, :] = m2
            l_sc[rows, :] = l2
            n_sc[rows, :] = a2

    groups = [blocks[g:g + DIL_GROUP] for g in range(0, len(blocks), DIL_GROUP)]
    n = len(groups)

    def stage_scores(g):
        out = []
        for spec in groups[g]:
            blk = load(*spec)
            out.append((blk, [scores(blk, hd) for hd in range(heads)]))
        return out

    def stage_softmax(scored):
        return [(blk, [softmax(blk, s) for s in s_list]) for blk, s_list in scored]

    def stage_values(g, probs):
        for spec, (blk, p_list) in zip(groups[g], probs):
            merge(spec[0], blk[0],
                  [(m_rows, values(blk, hd, p)) for hd, (p, m_rows) in enumerate(p_list)])

    s_ready = {g: stage_scores(g) for g in range(min(2, n))}
    p_ready = {0: stage_softmax(s_ready.pop(0))}
    for g in range(n):
        if g + 2 < n:
            s_ready[g + 2] = stage_scores(g + 2)
        if g + 1 < n:
            p_ready[g + 1] = stage_softmax(s_ready.pop(g + 1))
        stage_values(g, p_ready.pop(g))


def _dil_attention(proj):
    def spec(col0):
        return pl.BlockSpec((SEQ, LANES), lambda b, hp: (b, col0 // LANES + hp))

    return pl.pallas_call(
        _dil_attn_kernel,
        grid=(BATCH, DIL_WIDTH // LANES),
        in_specs=[spec(DIL_Q_COL), spec(DIL_K_COL), spec(DIL_V_COL)],
        out_specs=pl.BlockSpec((SEQ, LANES), lambda b, hp: (b, hp)),
        out_shape=jax.ShapeDtypeStruct((TOKENS, DIL_WIDTH), F32),
        scratch_shapes=[pltpu.VMEM((SEQ, LANES), F32)] * 3,
        compiler_params=_cparams("parallel", "parallel"),
        name="dilated_attention",
    )(proj, proj, proj)


SLAB_GROUPS = LANES // SSM_GROUP
SSM_SLABS = SSM_GROUPS // SLAB_GROUPS
SSM_NCHUNK = SEQ // SSM_CHUNK
SSM_ROWS = BATCH * SSM_NCHUNK
SSM_ROW = SSM_CHUNK * LANES
SLAB_STATE = SLAB_GROUPS * SSM_STATE
STATE_TILES = SLAB_STATE // LANES
SUBLANES = 8


def _expand_block_diag(x, width):
    rows, c = x.shape
    src = lax.broadcasted_iota(jnp.int32, (c, width), 0)
    dst = lax.broadcasted_iota(jnp.int32, (c, width), 1)
    repeat = jnp.where(dst % c == src, 1.0, 0.0).astype(BF16)
    wide = jnp.dot(x.astype(BF16), repeat, preferred_element_type=F32)
    row_g = lax.broadcasted_iota(jnp.int32, (rows, width), 0) // (rows // SLAB_GROUPS)
    col_g = lax.broadcasted_iota(jnp.int32, (rows, width), 1) // c
    return jnp.where(row_g == col_g, wide, 0.0).astype(BF16)


def _ssm_kernel(u_ref, dlag_ref, m1re_ref, m1im_ref, m2re_ref, m2im_ref,
                are_ref, aim_ref, d_ref, y_ref, fold_ref, st_ref, t0_sc, op_sc):
    lag_blocks = [_expand_block_diag(dlag_ref[0, tau], LANES) for tau in range(SSM_CHUNK)]
    no_block = jnp.zeros((LANES, LANES), BF16)
    for t in range(SSM_CHUNK):
        for s in range(SSM_CHUNK):
            t0_sc[t * LANES:(t + 1) * LANES, s * LANES:(s + 1) * LANES] = (
                lag_blocks[s - t] if s >= t else no_block)
    for which, ref in enumerate((m1re_ref, m1im_ref, m2re_ref, m2im_ref)):
        for t in range(SSM_CHUNK):
            op_sc[which, t * LANES:(t + 1) * LANES, :] = _expand_block_diag(ref[0, t], SLAB_STATE)

    for b in range(BATCH):
        rows = slice(b * SSM_NCHUNK, (b + 1) * SSM_NCHUNK)
        for t in range(SSM_CHUNK):
            fold_ref[rows, t * LANES:(t + 1) * LANES] = (
                u_ref[pl.ds(b * SEQ + t, SSM_NCHUNK, stride=SSM_CHUNK), :])
    u = fold_ref[...]
    ub = u.astype(BF16)

    loc_re = jnp.dot(ub, op_sc[0], preferred_element_type=F32)
    loc_im = jnp.dot(ub, op_sc[1], preferred_element_type=F32)
    for b in range(BATCH):
        rows = slice(b * SSM_NCHUNK, (b + 1) * SSM_NCHUNK)
        for k in range(STATE_TILES):
            lanes = slice(k * LANES, (k + 1) * LANES)
            st_ref[k, pl.ds(b, SSM_NCHUNK, stride=SUBLANES), :] = loc_re[rows, lanes]
            st_ref[k, pl.ds(BATCH + b, SSM_NCHUNK, stride=SUBLANES), :] = loc_im[rows, lanes]

    a_re, a_im = are_ref[0], aim_ref[0]
    upper = lax.broadcasted_iota(jnp.int32, (SUBLANES, LANES), 0) < BATCH
    mul_same, mul_swap = [], []
    for k in range(STATE_TILES):
        lanes = slice(k * LANES, (k + 1) * LANES)
        mul_same.append(jnp.broadcast_to(a_re[:, lanes], (SUBLANES, LANES)))
        im = jnp.broadcast_to(a_im[:, lanes], (SUBLANES, LANES))
        mul_swap.append(jnp.where(upper, -im, im))

    def chunk_step(c, state):
        r0 = pl.multiple_of(c * SUBLANES, SUBLANES)
        new = []
        for k in range(STATE_TILES):
            loc = st_ref[k, pl.ds(r0, SUBLANES), :]
            st_ref[k, pl.ds(r0, SUBLANES), :] = state[k]
            swapped = pltpu.roll(state[k], BATCH, axis=0)
            new.append(mul_same[k] * state[k] + mul_swap[k] * swapped + loc)
        return tuple(new)

    zero = jnp.zeros((SUBLANES, LANES), F32)
    lax.fori_loop(0, SSM_NCHUNK, chunk_step, (zero,) * STATE_TILES, unroll=4)

    def entering(offset):
        return jnp.concatenate(
            [jnp.concatenate([st_ref[k, pl.ds(offset + b, SSM_NCHUNK, stride=SUBLANES), :]
                              for k in range(STATE_TILES)], axis=1)
             for b in range(BATCH)], axis=0)

    y = jnp.dot(ub, t0_sc[...], preferred_element_type=F32) + u * d_ref[0]
    for offset, which in ((0, 2), (BATCH, 3)):
        m2t = op_sc[which]
        y = y + lax.dot_general(entering(offset).astype(BF16), m2t, NT_DIMS,
                                preferred_element_type=F32)
    fold_ref[...] = jax.nn.gelu(y, approximate=True)

    for b in range(BATCH):
        rows = slice(b * SSM_NCHUNK, (b + 1) * SSM_NCHUNK)
        for t in range(SSM_CHUNK):
            y_ref[pl.ds(b * SEQ + t, SSM_NCHUNK, stride=SSM_CHUNK), :] = (
                fold_ref[rows, t * LANES:(t + 1) * LANES])


def _ssm_scan(proj, mats, layer):
    dlag, m1re, m1im, m2re, m2im, a_re, a_im, dvec = mats

    def spec(r, c):
        return _layer_spec((1, r, c), lambda s: (layer, s, 0, 0))

    def compact(c):
        return _layer_spec((1, SSM_CHUNK, LANES, c), lambda s: (layer, s, 0, 0, 0))

    return pl.pallas_call(
        _ssm_kernel,
        grid=(SSM_SLABS,),
        in_specs=[pl.BlockSpec((TOKENS, LANES), lambda s: (0, SSM_U_COL // LANES + s)),
                  compact(SSM_GROUP),
                  compact(SSM_STATE), compact(SSM_STATE),
                  compact(SSM_STATE), compact(SSM_STATE),
                  spec(1, SLAB_STATE), spec(1, SLAB_STATE), spec(1, SSM_ROW)],
        out_specs=pl.BlockSpec((TOKENS, LANES), lambda s: (0, s)),
        out_shape=jax.ShapeDtypeStruct((TOKENS, SSM_WIDTH), F32),
        scratch_shapes=[pltpu.VMEM((SSM_ROWS, SSM_ROW), F32),
                        pltpu.VMEM((STATE_TILES, SUBLANES * SSM_NCHUNK, LANES), F32),
                        pltpu.VMEM((SSM_ROW, SSM_ROW), BF16),
                        pltpu.VMEM((4, SSM_ROW, SLAB_STATE), BF16)],
        compiler_params=_cparams("arbitrary"),
        name="ssm_chunk_scan",
    )(proj, dlag, m1re, m1im, m2re, m2im, a_re, a_im, dvec)


def _ssm_matrices(a_re, a_im, b_re, b_im, c_re, c_im, d_skip, log_dt):
    lam_re = jnp.minimum(a_re, -1e-4)
    lam_im = a_im
    dt = jnp.exp(log_dt)[:, None]
    mag = jnp.exp(lam_re * dt)
    ab_re, ab_im = mag * jnp.cos(lam_im * dt), mag * jnp.sin(lam_im * dt)
    n_re, n_im = ab_re - 1.0, ab_im
    den = lam_re * lam_re + lam_im * lam_im
    f_re = (n_re * lam_re + n_im * lam_im) / den
    f_im = (n_im * lam_re - n_re * lam_im) / den
    bb_re = f_re[..., None] * b_re - f_im[..., None] * b_im
    bb_im = f_re[..., None] * b_im + f_im[..., None] * b_re
    p_re, p_im = [jnp.ones_like(ab_re)], [jnp.zeros_like(ab_im)]
    for _ in range(SSM_CHUNK):
        p_re.append(p_re[-1] * ab_re - p_im[-1] * ab_im)
        p_im.append(p_re[-2] * ab_im + p_im[-1] * ab_re)
    pw_re, pw_im = jnp.stack(p_re), jnp.stack(p_im)

    cb_re = (c_re[:, None, :, :] * bb_re.transpose(0, 2, 1)[:, :, None, :]
             - c_im[:, None, :, :] * bb_im.transpose(0, 2, 1)[:, :, None, :])
    cb_im = (c_re[:, None, :, :] * bb_im.transpose(0, 2, 1)[:, :, None, :]
             + c_im[:, None, :, :] * bb_re.transpose(0, 2, 1)[:, :, None, :])
    lag_re = pw_re[:SSM_CHUNK].transpose(1, 0, 2)[:, :, None, None, :]
    lag_im = pw_im[:SSM_CHUNK].transpose(1, 0, 2)[:, :, None, None, :]
    kern = jnp.sum(lag_re * cb_re[:, None] - lag_im * cb_im[:, None], axis=-1)
    t_idx = np.arange(SSM_CHUNK)

    back = pw_re[SSM_CHUNK - 1 - t_idx], pw_im[SSM_CHUNK - 1 - t_idx]
    bre_t, bim_t = bb_re.transpose(0, 2, 1), bb_im.transpose(0, 2, 1)
    m1re = (back[0].transpose(1, 0, 2)[:, :, None, :] * bre_t[:, None]
            - back[1].transpose(1, 0, 2)[:, :, None, :] * bim_t[:, None])
    m1im = (back[0].transpose(1, 0, 2)[:, :, None, :] * bim_t[:, None]
            + back[1].transpose(1, 0, 2)[:, :, None, :] * bre_t[:, None])

    fwd_re = pw_re[1:].transpose(1, 0, 2)[:, :, None, :]
    fwd_im = pw_im[1:].transpose(1, 0, 2)[:, :, None, :]
    ca_re = c_re[:, None] * fwd_re - c_im[:, None] * fwd_im
    ca_im = c_re[:, None] * fwd_im + c_im[:, None] * fwd_re

    def rows_by_group(x):
        k, r, c = x.shape[1:]
        x = x.reshape(SSM_SLABS, SLAB_GROUPS, k, r, c).transpose(0, 2, 1, 3, 4)
        return x.reshape(SSM_SLABS, k, SLAB_GROUPS * r, c)

    a_step_re = pw_re[SSM_CHUNK].reshape(SSM_SLABS, 1, SLAB_STATE)
    a_step_im = pw_im[SSM_CHUNK].reshape(SSM_SLABS, 1, SLAB_STATE)
    dvec = jnp.concatenate([d_skip.reshape(SSM_SLABS, 1, LANES)] * SSM_CHUNK, axis=-1)
    return (rows_by_group(kern), rows_by_group(m1re), rows_by_group(m1im),
            rows_by_group(ca_re), rows_by_group(-ca_im), a_step_re, a_step_im, dvec)


def _glu_kernel(y_ref, w_ref, b_ref, o_ref):
    z = jnp.dot(y_ref[...].astype(BF16), _bf16(w_ref[...]),
                preferred_element_type=F32) + b_ref[...]
    o_ref[...] = (z[:, :SSM_WIDTH] * jax.nn.sigmoid(z[:, SSM_WIDTH:])).astype(o_ref.dtype)


def _glu(y, w_glu, b_glu, layer, *, bm):
    return pl.pallas_call(
        _glu_kernel,
        grid=(TOKENS // bm,),
        in_specs=[pl.BlockSpec((bm, SSM_WIDTH), lambda i: (i, 0)),
                  _layer_spec((SSM_WIDTH, 2 * SSM_WIDTH), lambda i: (layer, 0, 0)),
                  _layer_spec((1, 2 * SSM_WIDTH), lambda i: (layer, 0, 0))],
        out_specs=pl.BlockSpec((bm, SSM_WIDTH), lambda i: (i, 0)),
        out_shape=jax.ShapeDtypeStruct((TOKENS, SSM_WIDTH), BF16),
        compiler_params=_cparams("parallel"),
        name="ssm_glu",
    )(y, w_glu, b_glu.reshape(DEPTH, 1, -1))


def _out_proj_kernel(ya_ref, yb_ref, yc_ref, ga_ref, gb_ref, gc_ref, w_ref, x_ref,
                     o_ref, h_ref):
    parts = ((ya_ref, ga_ref[...]), (yb_ref, gb_ref[...]), (yc_ref, gc_ref[...]))

    def normed_chunk(rows):
        return jnp.concatenate([_rms_rows(y_ref[rows, :].astype(F32), g).astype(BF16)
                                for y_ref, g in parts], axis=-1)

    def make_emit():
        w = _bf16(w_ref[...])

        def emit(rows, h):
            o_ref[rows, :] = x_ref[rows, :] + jnp.dot(h, w, preferred_element_type=F32)
        return emit

    _first_step_by_chunks(h_ref, normed_chunk, make_emit)


def _out_proj(y_mla, y_ssm, y_dil, g_mla, g_ssm, g_dil, w_o, x, layer, *, bm, bn):
    def rows(width):
        return pl.BlockSpec((bm, width), lambda i, j: (i, 0))

    def gain(width):
        return _layer_spec((1, width), lambda i, j: (layer, 0, 0))

    return pl.pallas_call(
        _out_proj_kernel,
        grid=(TOKENS // bm, D_MODEL // bn),
        in_specs=[rows(MLA_WIDTH), rows(SSM_WIDTH), rows(DIL_WIDTH),
                  gain(MLA_WIDTH), gain(SSM_WIDTH), gain(DIL_WIDTH),
                  _layer_spec((D_MODEL, bn), lambda i, j: (layer, 0, j)),
                  pl.BlockSpec((bm, bn), lambda i, j: (i, j))],
        out_specs=pl.BlockSpec((bm, bn), lambda i, j: (i, j)),
        out_shape=jax.ShapeDtypeStruct((TOKENS, D_MODEL), F32),
        scratch_shapes=[pltpu.VMEM((bm, D_MODEL), BF16)],
        compiler_params=_cparams("parallel", "arbitrary"),
        name="out_proj",
    )(y_mla, y_ssm, y_dil, g_mla.reshape(DEPTH, 1, -1), g_ssm.reshape(DEPTH, 1, -1),
      g_dil.reshape(DEPTH, 1, -1), w_o, x)


def _ffn_up_kernel(x_ref, g_ref, wg_ref, wu_ref, wd_ref, o_ref, wd_bf16_ref, h_ref):
    g = g_ref[...]
    wd_bf16_ref[...] = wd_ref[...].astype(BF16)

    def make_emit():
        wg, wu = _bf16(wg_ref[...]), _bf16(wu_ref[...])

        def emit(rows, h):
            gate = jnp.dot(h, wg, preferred_element_type=F32)
            up = jnp.dot(h, wu, preferred_element_type=F32)
            o_ref[rows, :] = (jax.nn.silu(gate) * up).astype(o_ref.dtype)
        return emit

    _first_step_by_chunks(h_ref, lambda rows: _rms_rows(x_ref[rows, :], g).astype(BF16),
                          make_emit)


def _ffn_up(x, g, w_gate, w_up, w_down, layer, *, bm, bn):
    col_steps = D_FF // bn
    wd_rows = D_FF // (TOKENS // bm * col_steps)
    return pl.pallas_call(
        _ffn_up_kernel,
        grid=(TOKENS // bm, col_steps),
        in_specs=[pl.BlockSpec((bm, D_MODEL), lambda i, j: (i, 0)),
                  _layer_spec((1, D_MODEL), lambda i, j: (layer, 0, 0)),
                  _layer_spec((D_MODEL, bn), lambda i, j: (layer, 0, j)),
                  _layer_spec((D_MODEL, bn), lambda i, j: (layer, 0, j)),
                  _layer_spec((wd_rows, D_MODEL), lambda i, j: (layer, i * col_steps + j, 0))],
        out_specs=[pl.BlockSpec((bm, bn), lambda i, j: (i, j)),
                   pl.BlockSpec((wd_rows, D_MODEL), lambda i, j: (i * col_steps + j, 0))],
        out_shape=[jax.ShapeDtypeStruct((TOKENS, D_FF), BF16),
                   jax.ShapeDtypeStruct((D_FF, D_MODEL), BF16)],
        scratch_shapes=[pltpu.VMEM((bm, D_MODEL), BF16)],
        compiler_params=_cparams("arbitrary", "arbitrary"),
        name="ffn_gate_up",
    )(x, g.reshape(DEPTH, 1, -1), w_gate, w_up, w_down)


def _ffn_down_kernel(a_ref, w_ref, x_ref, o_ref):
    o_ref[...] = x_ref[...] + jnp.dot(a_ref[...], _bf16(w_ref[...]),
                                      preferred_element_type=F32)


def _ffn_down(a, w_down, x, *, bm, bn):
    return pl.pallas_call(
        _ffn_down_kernel,
        grid=(TOKENS // bm, D_MODEL // bn),
        in_specs=[pl.BlockSpec((bm, D_FF), lambda i, j: (i, 0)),
                  pl.BlockSpec((D_FF, bn), lambda i, j: (0, j)),
                  pl.BlockSpec((bm, bn), lambda i, j: (i, j))],
        out_specs=pl.BlockSpec((bm, bn), lambda i, j: (i, j)),
        out_shape=jax.ShapeDtypeStruct((TOKENS, D_MODEL), F32),
        compiler_params=_cparams("parallel", "arbitrary"),
        name="ffn_down",
    )(a, w_down, x)


def _final_norm_kernel(x_ref, g_ref, o_ref):
    o_ref[...] = _rms_rows(x_ref[...], g_ref[...])


def _final_norm(x, g, *, bm):
    return pl.pallas_call(
        _final_norm_kernel,
        grid=(TOKENS // bm,),
        in_specs=[pl.BlockSpec((bm, D_MODEL), lambda i: (i, 0)),
                  pl.BlockSpec((1, D_MODEL), lambda i: (0, 0))],
        out_specs=pl.BlockSpec((bm, D_MODEL), lambda i: (i, 0)),
        out_shape=jax.ShapeDtypeStruct((TOKENS, D_MODEL), F32),
        compiler_params=_cparams("parallel"),
        name="final_norm",
    )(x, g.reshape(1, -1))


W_IN_SPLIT = 768
W_IN_ROPE_AT = 896
W_IN_REST_AT = 1024


def _pad_w_in_kernel(w_ref, o_ref):
    x = w_ref[...]
    rows = x.shape[0]
    o_ref[:, :W_IN_SPLIT] = x[:, :W_IN_SPLIT].astype(BF16)
    o_ref[:, W_IN_SPLIT:W_IN_ROPE_AT] = jnp.zeros((rows, W_IN_ROPE_AT - W_IN_SPLIT), BF16)
    o_ref[:, W_IN_ROPE_AT:W_IN_ROPE_AT + MLA_ROPE] = (
        x[:, W_IN_SPLIT:W_IN_SPLIT + MLA_ROPE].astype(BF16))
    o_ref[:, W_IN_ROPE_AT + MLA_ROPE:W_IN_REST_AT] = (
        jnp.zeros((rows, W_IN_REST_AT - W_IN_ROPE_AT - MLA_ROPE), BF16))
    o_ref[:, W_IN_REST_AT:] = x[:, W_IN_SPLIT + MLA_ROPE:].astype(BF16)


def _pad_w_in(w_in, *, rows=256):
    lyr, d, n = w_in.shape
    return pl.pallas_call(
        _pad_w_in_kernel,
        grid=(lyr, d // rows),
        in_specs=[pl.BlockSpec((None, rows, n), lambda l, i: (l, i, 0))],
        out_specs=pl.BlockSpec((None, rows, IN_PAD), lambda l, i: (l, i, 0)),
        out_shape=jax.ShapeDtypeStruct((lyr, d, IN_PAD), BF16),
        compiler_params=_cparams("parallel", "parallel"),
        name="pad_w_in",
    )(w_in)


def _pad_wt_uq(w_uq):
    lyr = w_uq.shape[0]
    w = w_uq.reshape(lyr, MLA_Q_LORA, MLA_HEADS, MLA_NOPE + MLA_ROPE)
    w = jnp.pad(w, ((0, 0), (0, 0), (0, 0), (0, MLA_QK_PAD - MLA_NOPE - MLA_ROPE)))
    w = w.reshape(lyr, MLA_Q_LORA, MLA_HEADS * MLA_QK_PAD)
    return w.transpose(0, 2, 1).astype(BF16)


def _split_w_ukv(w_ukv):
    lyr = w_ukv.shape[0]
    w = w_ukv.reshape(lyr, MLA_KV_LORA, MLA_HEADS, MLA_NOPE + MLA_V)
    wk = jnp.pad(w[..., :MLA_NOPE], ((0, 0), (0, 0), (0, 0), (0, MLA_QK_PAD - MLA_NOPE)))
    wk = wk.reshape(lyr, MLA_KV_LORA, -1).astype(BF16)
    wv = w[..., MLA_NOPE:].reshape(lyr, MLA_KV_LORA, -1)
    return wk, wv.transpose(0, 2, 1).astype(BF16)


def _rope_angles():
    inv_freq = ROPE_THETA ** (-jnp.arange(ROPE_HALF, dtype=F32) / ROPE_HALF)
    ang = jnp.arange(SEQ, dtype=F32)[:, None] * inv_freq[None, :]
    return jnp.cos(ang), jnp.sin(ang)


def _rope_tables_k():
    cos, sin = _rope_angles()
    one = jnp.ones((SEQ, MLA_NOPE), F32)
    z = lambda n: jnp.zeros((SEQ, n), F32)
    tail = MLA_QK_PAD - MLA_NOPE - MLA_ROPE
    cos_t = jnp.concatenate([one, cos, cos, z(tail)], axis=1)
    sin_a = jnp.concatenate([z(MLA_NOPE), -sin, z(ROPE_HALF), z(tail)], axis=1)
    sin_b = jnp.concatenate([z(MLA_NOPE), z(ROPE_HALF), sin, z(tail)], axis=1)
    return cos_t, sin_a, sin_b


def kernel(x, g_mix, w_in, g_q, w_uq, g_kv, w_ukv, a_re, a_im, b_re, b_im, c_re, c_im,
           d_skip, log_dt, w_glu, b_glu, g_out_mla, g_out_ssm, g_out_dil, w_o,
           g_ffn, w_gate, w_up, w_down, g_final):
    x = x.reshape(TOKENS, D_MODEL)
    w_in_p = _pad_w_in(w_in)
    wt_q = _pad_wt_uq(w_uq)
    w_k, wt_v = _split_w_ukv(w_ukv)
    w_o_b = w_o.astype(BF16)
    q_scale = (MLA_NOPE + MLA_ROPE) ** -0.5 * math.log2(math.e)
    cos, sin = _rope_angles()
    q_cos_t, q_sin_t = (cos * q_scale).T, (sin * q_scale).T
    k_tabs = _rope_tables_k()
    ssm_mats = jax.vmap(_ssm_matrices)(a_re, a_im, b_re, b_im, c_re, c_im, d_skip, log_dt)

    for l in range(DEPTH):
        proj = _norm_matmul(x, g_mix, w_in_p, l, bm=1024, bn=1024, out_dtype=F32,
                            name="in_proj")
        qt = _q_up(proj, g_q, wt_q, q_cos_t, q_sin_t, l, q_scale, bm=1024)
        k, vt = _kv_up(proj, g_kv, w_k, wt_v, k_tabs, l, bm=1024)
        y_mla = _mla_attention(qt, k, vt)
        y_ssm = _glu(_ssm_scan(proj, ssm_mats, l), w_glu, b_glu, l, bm=1024)
        y_dil = _dil_attention(proj)
        x = _out_proj(y_mla, y_ssm, y_dil, g_out_mla, g_out_ssm, g_out_dil,
                      w_o_b, x, l, bm=1024, bn=1024)
        act, w_down_b = _ffn_up(x, g_ffn, w_gate, w_up, w_down, l, bm=1024, bn=512)
        x = _ffn_down(act, w_down_b, x, bm=1024, bn=512)
    out = _final_norm(x, g_final, bm=512)
    return out.reshape(BATCH, SEQ, D_MODEL)
```

```python
import functools
import math

import jax
import jax.numpy as jnp
import numpy as np
from jax import lax
from jax.experimental import pallas as pl
from jax.experimental.pallas import tpu as pltpu

F32 = jnp.float32
BF16 = jnp.bfloat16

D_MODEL = 2048
BATCH = 4
SEQ = 2048
DEPTH = 4
TOKENS = BATCH * SEQ

MLA_HEADS = 8
MLA_NOPE = 128
MLA_ROPE = 64
MLA_V = 128
MLA_Q_LORA = 512
MLA_KV_LORA = 256
MLA_WIDTH = MLA_HEADS * MLA_V
MLA_QK_PAD = 256
ROPE_THETA = 10000.0
ROPE_HALF = MLA_ROPE // 2

SSM_WIDTH = 512
SSM_GROUP = 16
SSM_GROUPS = 32
SSM_STATE = 64
SSM_CHUNK = 8

DIL_WIDTH = 512
DIL_HEAD_DIM = 64
DIL_HEADS = 8
DIL_PATTERNS = ((128, 1), (512, 4), (2048, 16))
BLOCK = 128
DIL_GROUP = 1

IN_PAD = 3072
DIL_Q_COL = 1536
DIL_K_COL = 2048
DIL_V_COL = 2560
SSM_U_COL = 1024
D_FF = 5632
NORM_EPS = 1e-6

LANES = 128
VMEM_LIMIT_BYTES = 56 * 1024 * 1024

NT_DIMS = (((1,), (1,)), ((), ()))


def _cparams(*semantics):
    return pltpu.CompilerParams(dimension_semantics=semantics,
                                vmem_limit_bytes=VMEM_LIMIT_BYTES)


def _rms_rows(x, g):
    ms = jnp.mean(x * x, axis=-1, keepdims=True)
    return x * lax.rsqrt(ms + NORM_EPS) * g


NORM_CHUNK = 256


def _first_step_by_chunks(h_ref, normed_chunk, make_emit):
    @pl.when(pl.program_id(1) == 0)
    def _():
        emit = make_emit()
        for r in range(0, h_ref.shape[0], NORM_CHUNK):
            rows = slice(r, r + NORM_CHUNK)
            h = normed_chunk(rows)
            h_ref[rows, :] = h
            emit(rows, h)

    @pl.when(pl.program_id(1) != 0)
    def _():
        make_emit()(slice(None), h_ref[...])


def _bf16(w):
    return w if w.dtype == BF16 else w.astype(BF16)


def _layer_spec(shape, index_map):
    return pl.BlockSpec((None,) + tuple(shape), index_map)


def _norm_matmul_kernel(x_ref, g_ref, w_ref, o_ref, h_ref):
    g = g_ref[...]

    def make_emit():
        w = _bf16(w_ref[...])

        def emit(rows, h):
            o_ref[rows, :] = jnp.dot(h, w, preferred_element_type=F32).astype(o_ref.dtype)
        return emit

    _first_step_by_chunks(h_ref, lambda rows: _rms_rows(x_ref[rows, :], g).astype(BF16),
                          make_emit)


def _norm_matmul(x, g, w, layer, *, bm, bn, out_dtype, name):
    m, k = x.shape
    n = w.shape[2]
    return pl.pallas_call(
        _norm_matmul_kernel,
        grid=(m // bm, n // bn),
        in_specs=[pl.BlockSpec((bm, k), lambda i, j: (i, 0)),
                  _layer_spec((1, k), lambda i, j: (layer, 0, 0)),
                  _layer_spec((k, bn), lambda i, j: (layer, 0, j))],
        out_specs=pl.BlockSpec((bm, bn), lambda i, j: (i, j)),
        out_shape=jax.ShapeDtypeStruct((m, n), out_dtype),
        scratch_shapes=[pltpu.VMEM((bm, k), BF16)],
        compiler_params=_cparams("parallel", "arbitrary"),
        name=name,
    )(x, g.reshape(DEPTH, 1, k), w)


def _q_up_kernel(c_ref, g_ref, wt_ref, cos_ref, sin_ref, o_ref, *, scale):
    h = _rms_rows(c_ref[...], g_ref[...]).astype(BF16)
    cos, sin = cos_ref[...], sin_ref[...]

    def project(hd):
        rows = slice(hd * MLA_QK_PAD, (hd + 1) * MLA_QK_PAD)
        return lax.dot_general(wt_ref[rows, :], h, NT_DIMS, preferred_element_type=F32)

    def finish(hd, qt):
        r0 = hd * MLA_QK_PAD
        n1, n2, n3 = MLA_NOPE, MLA_NOPE + ROPE_HALF, MLA_NOPE + MLA_ROPE
        x1, x2 = qt[n1:n2], qt[n2:n3]
        o_ref[r0:r0 + n1, :] = (qt[:n1] * scale).astype(o_ref.dtype)
        o_ref[r0 + n1:r0 + n2, :] = (x1 * cos - x2 * sin).astype(o_ref.dtype)
        o_ref[r0 + n2:r0 + n3, :] = (x2 * cos + x1 * sin).astype(o_ref.dtype)
        o_ref[r0 + n3:r0 + MLA_QK_PAD, :] = qt[n3:].astype(o_ref.dtype)

    qt_next = project(0)
    for hd in range(MLA_HEADS):
        qt = qt_next
        if hd + 1 < MLA_HEADS:
            qt_next = project(hd + 1)
        finish(hd, qt)


def _q_up(proj, g_q, wt_q, cos_t, sin_t, layer, scale, *, bm):
    pos_blocks = SEQ // bm
    tab_spec = pl.BlockSpec((ROPE_HALF, bm), lambda i: (0, i % pos_blocks))
    n = MLA_HEADS * MLA_QK_PAD
    return pl.pallas_call(
        functools.partial(_q_up_kernel, scale=scale),
        grid=(TOKENS // bm,),
        in_specs=[pl.BlockSpec((bm, MLA_Q_LORA), lambda i: (i, 0)),
                  _layer_spec((1, MLA_Q_LORA), lambda i: (layer, 0, 0)),
                  _layer_spec((n, MLA_Q_LORA), lambda i: (layer, 0, 0)),
                  tab_spec, tab_spec],
        out_specs=pl.BlockSpec((n, bm), lambda i: (0, i)),
        out_shape=jax.ShapeDtypeStruct((n, TOKENS), BF16),
        compiler_params=_cparams("parallel"),
        name="mla_q_up",
    )(proj, g_q.reshape(DEPTH, 1, -1), wt_q, cos_t, sin_t)


def _rope_pad(x, cos_t, sin_a, sin_b):
    up = pltpu.roll(x, MLA_QK_PAD - ROPE_HALF, axis=1)
    dn = pltpu.roll(x, ROPE_HALF, axis=1)
    return x * cos_t + up * sin_a + dn * sin_b


def _kv_up_kernel(c_ref, kr_ref, g_ref, wk_ref, wvt_ref, cos_ref, sina_ref, sinb_ref,
                  k_ref, vt_ref):
    h = _rms_rows(c_ref[...], g_ref[...]).astype(BF16)
    kn = jnp.dot(h, wk_ref[...], preferred_element_type=F32)
    vt = lax.dot_general(wvt_ref[...], h, NT_DIMS, preferred_element_type=F32)
    k_pe = _rope_pad(kr_ref[...], cos_ref[...], sina_ref[...], sinb_ref[...])
    for hd in range(MLA_HEADS):
        sl = slice(hd * MLA_QK_PAD, (hd + 1) * MLA_QK_PAD)
        k_ref[:, sl] = (kn[:, sl] + k_pe).astype(k_ref.dtype)
    vt_ref[...] = vt.astype(vt_ref.dtype)


def _kv_up(proj, g_kv, w_k, wt_v, tabs, layer, *, bm):
    pos_blocks = SEQ // bm
    tab_spec = pl.BlockSpec((bm, MLA_QK_PAD), lambda i: (i % pos_blocks, 0))
    nk = MLA_HEADS * MLA_QK_PAD
    return pl.pallas_call(
        _kv_up_kernel,
        grid=(TOKENS // bm,),
        in_specs=[pl.BlockSpec((bm, MLA_KV_LORA), lambda i: (i, 2)),
                  pl.BlockSpec((bm, MLA_QK_PAD), lambda i: (i, 3)),
                  _layer_spec((1, MLA_KV_LORA), lambda i: (layer, 0, 0)),
                  _layer_spec((MLA_KV_LORA, nk), lambda i: (layer, 0, 0)),
                  _layer_spec((MLA_WIDTH, MLA_KV_LORA), lambda i: (layer, 0, 0)),
                  tab_spec, tab_spec, tab_spec],
        out_specs=[pl.BlockSpec((bm, nk), lambda i: (i, 0)),
                   pl.BlockSpec((MLA_WIDTH, bm), lambda i: (0, i))],
        out_shape=[jax.ShapeDtypeStruct((TOKENS, nk), BF16),
                   jax.ShapeDtypeStruct((MLA_WIDTH, TOKENS), BF16)],
        compiler_params=_cparams("parallel"),
        name="mla_kv_up",
    )(proj, proj, g_kv.reshape(DEPTH, 1, -1), w_k, wt_v, *tabs)


ATT_BQ = 256
ATT_BK = 256
ATT_NQ = SEQ // ATT_BQ
ATT_ONES = 16
ATT_GROUP = 3


def _mla_attn_kernel(qt_ref, k_ref, vt_ref, o_ref, m_sc, acc_sc):
    key = lax.broadcasted_iota(jnp.int32, (ATT_BK, ATT_BQ), 0)
    qry = lax.broadcasted_iota(jnp.int32, (ATT_BK, ATT_BQ), 1)
    causal = key <= qry
    tiles = [(i, j) for j in range(ATT_NQ) for i in range(j, ATT_NQ)]

    def scores(i, j):
        kj = k_ref[j * ATT_BK:(j + 1) * ATT_BK, :]
        qi = qt_ref[:, i * ATT_BQ:(i + 1) * ATT_BQ]
        return jnp.dot(kj, qi, preferred_element_type=F32)

    def softmax(i, j, s):
        if i == j:
            s = jnp.where(causal, s, -jnp.inf)
        m_blk = jnp.max(s, axis=0, keepdims=True)
        if j == 0:
            m_new, alpha = m_blk, None
        else:
            m_old = m_sc[i]
            m_new = jnp.maximum(m_old, m_blk)
            alpha = jnp.exp2(m_old - m_new)
        if i != j:
            m_sc[i] = m_new
        return jnp.exp2(s - m_new).astype(BF16), alpha

    ones_rows = jnp.ones((ATT_ONES, ATT_BK), BF16)

    def values(i, j, p, alpha):
        vj = jnp.concatenate([vt_ref[:, j * ATT_BK:(j + 1) * ATT_BK], ones_rows], axis=0)
        acc = jnp.dot(vj, p, preferred_element_type=F32)
        if alpha is not None:
            acc = alpha * acc_sc[i] + acc
        if i == j:
            out = acc[:MLA_V] / acc[MLA_V:MLA_V + 1]
            o_ref[i * ATT_BQ:(i + 1) * ATT_BQ, :] = out.T.astype(o_ref.dtype)
        else:
            acc_sc[i] = acc

    groups = [tiles[g:g + ATT_GROUP] for g in range(0, len(tiles), ATT_GROUP)]
    n = len(groups)

    def stage_scores(g):
        return [scores(*tile) for tile in groups[g]]

    def stage_softmax(g, s_list):
        return [softmax(*tile, s) for tile, s in zip(groups[g], s_list)]

    def stage_values(g, p_list):
        for tile, p_alpha in zip(groups[g], p_list):
            values(*tile, *p_alpha)

    s_ready = {g: stage_scores(g) for g in range(min(2, n))}
    p_ready = {0: stage_softmax(0, s_ready.pop(0))}
    for g in range(n):
        if g + 2 < n:
            s_ready[g + 2] = stage_scores(g + 2)
        if g + 1 < n:
            p_ready[g + 1] = stage_softmax(g + 1, s_ready.pop(g + 1))
        stage_values(g, p_ready.pop(g))


def _mla_attention(qt, k, vt):
    return pl.pallas_call(
        _mla_attn_kernel,
        grid=(BATCH, MLA_HEADS),
        in_specs=[pl.BlockSpec((MLA_QK_PAD, SEQ), lambda b, h: (h, b)),
                  pl.BlockSpec((SEQ, MLA_QK_PAD), lambda b, h: (b, h)),
                  pl.BlockSpec((MLA_V, SEQ), lambda b, h: (h, b))],
        out_specs=pl.BlockSpec((SEQ, MLA_V), lambda b, h: (b, h)),
        out_shape=jax.ShapeDtypeStruct((TOKENS, MLA_WIDTH), BF16),
        scratch_shapes=[pltpu.VMEM((ATT_NQ, 1, ATT_BQ), F32),
                        pltpu.VMEM((ATT_NQ, MLA_V + ATT_ONES, ATT_BQ), F32)],
        compiler_params=_cparams("parallel", "parallel"),
        name="mla_attention",
    )(qt, k, vt)


def _dil_attn_kernel(q_ref, k_ref, v_ref, o_ref, m_sc, l_sc, n_sc):
    row2 = lax.broadcasted_iota(jnp.int32, (BLOCK, 2 * BLOCK), 0)
    col2 = lax.broadcasted_iota(jnp.int32, (BLOCK, 2 * BLOCK), 1)
    dist = row2 + BLOCK - col2
    band = (dist >= 0) & (dist <= BLOCK)
    row1 = lax.broadcasted_iota(jnp.int32, (BLOCK, BLOCK), 0)
    col1 = lax.broadcasted_iota(jnp.int32, (BLOCK, BLOCK), 1)
    tri = row1 >= col1
    q_scale = DIL_HEAD_DIM ** -0.5 * math.log2(math.e)
    heads = LANES // DIL_HEAD_DIM
    n_patterns = len(DIL_PATTERNS)

    def rows_at(start, dil):
        return pl.ds(start, BLOCK) if dil == 1 else pl.ds(start, BLOCK, stride=dil)

    blocks = [(pi, dil, r, n) for pi, (_, dil) in enumerate(reversed(DIL_PATTERNS))
              for r in range(dil) for n in range(SEQ // dil // BLOCK)]

    assert heads == 2
    head0 = lax.broadcasted_iota(jnp.int32, (BLOCK, LANES), 1) < DIL_HEAD_DIM

    def load(pi, dil, r, n):
        rows = rows_at(r + dil * BLOCK * n, dil)
        q = q_ref[rows, :] * q_scale
        if n == 0:
            k, v, mask = k_ref[rows, :], v_ref[rows, :], tri
        else:
            prev = rows_at(r + dil * BLOCK * (n - 1), dil)
            k = jnp.concatenate([k_ref[prev, :], k_ref[rows, :]], axis=0)
            v = jnp.concatenate([v_ref[prev, :], v_ref[rows, :]], axis=0)
            mask = band
        v_head0 = lax.broadcasted_iota(jnp.int32, v.shape, 1) < DIL_HEAD_DIM
        q_heads = (jnp.where(head0, q, 0.0).astype(BF16), jnp.where(head0, 0.0, q).astype(BF16))
        v_heads = (jnp.where(v_head0, v, 1.0).astype(BF16),
                   jnp.where(v_head0, 1.0, v).astype(BF16))
        return rows, q_heads, k.astype(BF16), v_heads, mask

    def scores(blk, hd):
        return lax.dot_general(blk[1][hd], blk[2], NT_DIMS, preferred_element_type=F32)

    def softmax(blk, s):
        s = jnp.where(blk[4], s, -jnp.inf)
        m = jnp.max(s, axis=-1, keepdims=True)
        return jnp.exp2(s - m).astype(BF16), m

    def values(blk, hd, p):
        return jnp.dot(p, blk[3][hd], preferred_element_type=F32)

    def merge(pi, rows, parts):
        (m_a, ext_a), (m_b, ext_b) = parts
        m2 = jnp.where(head0, m_a, m_b)
        a2 = jnp.where(head0, ext_a, ext_b)
        l2 = pltpu.roll(jnp.where(head0, ext_b, ext_a), DIL_HEAD_DIM, axis=1)
        if pi > 0:
            m_old = m_sc[rows, :]
            m_new = jnp.maximum(m_old, m2)
            w_old, w_new = jnp.exp2(m_old - m_new), jnp.exp2(m2 - m_new)
            l2 = w_old * l_sc[rows, :] + w_new * l2
            a2 = w_old * n_sc[rows, :] + w_new * a2
            m2 = m_new
        if pi == n_patterns - 1:
            o_ref[rows, :] = a2 / l2
        else:
            m_sc[rows, :] = m2
            l_sc[rows, :] = l2
            n_sc[rows, :] = a2

    groups = [blocks[g:g + DIL_GROUP] for g in range(0, len(blocks), DIL_GROUP)]
    n = len(groups)

    def stage_scores(g):
        out = []
        for spec in groups[g]:
            blk = load(*spec)
            out.append((blk, [scores(blk, hd) for hd in range(heads)]))
        return out

    def stage_softmax(scored):
        return [(blk, [softmax(blk, s) for s in s_list]) for blk, s_list in scored]

    def stage_values(g, probs):
        for spec, (blk, p_list) in zip(groups[g], probs):
            merge(spec[0], blk[0],
                  [(m_rows, values(blk, hd, p)) for hd, (p, m_rows) in enumerate(p_list)])

    s_ready = {g: stage_scores(g) for g in range(min(2, n))}
    p_ready = {0: stage_softmax(s_ready.pop(0))}
    for g in range(n):
        if g + 2 < n:
            s_ready[g + 2] = stage_scores(g + 2)
        if g + 1 < n:
            p_ready[g + 1] = stage_softmax(s_ready.pop(g + 1))
        stage_values(g, p_ready.pop(g))


def _dil_attention(proj):
    def spec(col0):
        return pl.BlockSpec((SEQ, LANES), lambda b, hp: (b, col0 // LANES + hp))

    return pl.pallas_call(
        _dil_attn_kernel,
        grid=(BATCH, DIL_WIDTH // LANES),
        in_specs=[spec(DIL_Q_COL), spec(DIL_K_COL), spec(DIL_V_COL)],
        out_specs=pl.BlockSpec((SEQ, LANES), lambda b, hp: (b, hp)),
        out_shape=jax.ShapeDtypeStruct((TOKENS, DIL_WIDTH), F32),
        scratch_shapes=[pltpu.VMEM((SEQ, LANES), F32)] * 3,
        compiler_params=_cparams("parallel", "parallel"),
        name="dilated_attention",
    )(proj, proj, proj)


SLAB_GROUPS = LANES // SSM_GROUP
SSM_SLABS = SSM_GROUPS // SLAB_GROUPS
SSM_NCHUNK = SEQ // SSM_CHUNK
SSM_ROWS = BATCH * SSM_NCHUNK
SSM_ROW = SSM_CHUNK * LANES
SLAB_STATE = SLAB_GROUPS * SSM_STATE
STATE_TILES = SLAB_STATE // LANES
SUBLANES = 8


def _expand_block_diag(x, width):
    rows, c = x.shape
    src = lax.broadcasted_iota(jnp.int32, (c, width), 0)
    dst = lax.broadcasted_iota(jnp.int32, (c, width), 1)
    repeat = jnp.where(dst % c == src, 1.0, 0.0).astype(BF16)
    wide = jnp.dot(x.astype(BF16), repeat, preferred_element_type=F32)
    row_g = lax.broadcasted_iota(jnp.int32, (rows, width), 0) // (rows // SLAB_GROUPS)
    col_g = lax.broadcasted_iota(jnp.int32, (rows, width), 1) // c
    return jnp.where(row_g == col_g, wide, 0.0).astype(BF16)


def _ssm_kernel(u_ref, dlag_ref, m1re_ref, m1im_ref, m2re_ref, m2im_ref,
                are_ref, aim_ref, d_ref, y_ref, fold_ref, st_ref, t0_sc, op_sc):
    lag_blocks = [_expand_block_diag(dlag_ref[0, tau], LANES) for tau in range(SSM_CHUNK)]
    no_block = jnp.zeros((LANES, LANES), BF16)
    for t in range(SSM_CHUNK):
        for s in range(SSM_CHUNK):
            t0_sc[t * LANES:(t + 1) * LANES, s * LANES:(s + 1) * LANES] = (
                lag_blocks[s - t] if s >= t else no_block)
    for which, ref in enumerate((m1re_ref, m1im_ref, m2re_ref, m2im_ref)):
        for t in range(SSM_CHUNK):
            op_sc[which, t * LANES:(t + 1) * LANES, :] = _expand_block_diag(ref[0, t], SLAB_STATE)

    for b in range(BATCH):
        rows = slice(b * SSM_NCHUNK, (b + 1) * SSM_NCHUNK)
        for t in range(SSM_CHUNK):
            fold_ref[rows, t * LANES:(t + 1) * LANES] = (
                u_ref[pl.ds(b * SEQ + t, SSM_NCHUNK, stride=SSM_CHUNK), :])
    u = fold_ref[...]
    ub = u.astype(BF16)

    loc_re = jnp.dot(ub, op_sc[0], preferred_element_type=F32)
    loc_im = jnp.dot(ub, op_sc[1], preferred_element_type=F32)
    fold_ref[...] = jnp.dot(ub, t0_sc[...], preferred_element_type=F32) + u * d_ref[0]
    for b in range(BATCH):
        rows = slice(b * SSM_NCHUNK, (b + 1) * SSM_NCHUNK)
        for k in range(STATE_TILES):
            lanes = slice(k * LANES, (k + 1) * LANES)
            st_ref[k, pl.ds(b, SSM_NCHUNK, stride=SUBLANES), :] = loc_re[rows, lanes]
            st_ref[k, pl.ds(BATCH + b, SSM_NCHUNK, stride=SUBLANES), :] = loc_im[rows, lanes]

    a_re, a_im = are_ref[0], aim_ref[0]
    upper = lax.broadcasted_iota(jnp.int32, (SUBLANES, LANES), 0) < BATCH
    mul_same, mul_swap = [], []
    for k in range(STATE_TILES):
        lanes = slice(k * LANES, (k + 1) * LANES)
        mul_same.append(jnp.broadcast_to(a_re[:, lanes], (SUBLANES, LANES)))
        im = jnp.broadcast_to(a_im[:, lanes], (SUBLANES, LANES))
        mul_swap.append(jnp.where(upper, -im, im))

    def chunk_step(c, state):
        r0 = pl.multiple_of(c * SUBLANES, SUBLANES)
        new = []
        for k in range(STATE_TILES):
            loc = st_ref[k, pl.ds(r0, SUBLANES), :]
            st_ref[k, pl.ds(r0, SUBLANES), :] = state[k]
            swapped = pltpu.roll(state[k], BATCH, axis=0)
            new.append(mul_same[k] * state[k] + mul_swap[k] * swapped + loc)
        return tuple(new)

    zero = jnp.zeros((SUBLANES, LANES), F32)
    lax.fori_loop(0, SSM_NCHUNK, chunk_step, (zero,) * STATE_TILES, unroll=4)

    def entering(offset):
        return jnp.concatenate(
            [jnp.concatenate([st_ref[k, pl.ds(offset + b, SSM_NCHUNK, stride=SUBLANES), :]
                              for k in range(STATE_TILES)], axis=1)
             for b in range(BATCH)], axis=0)

    y = fold_ref[...]
    for offset, which in ((0, 2), (BATCH, 3)):
        m2t = op_sc[which]
        y = y + lax.dot_general(entering(offset).astype(BF16), m2t, NT_DIMS,
                                preferred_element_type=F32)
    fold_ref[...] = jax.nn.gelu(y, approximate=True)

    for b in range(BATCH):
        rows = slice(b * SSM_NCHUNK, (b + 1) * SSM_NCHUNK)
        for t in range(SSM_CHUNK):
            y_ref[pl.ds(b * SEQ + t, SSM_NCHUNK, stride=SSM_CHUNK), :] = (
                fold_ref[rows, t * LANES:(t + 1) * LANES])


def _ssm_scan(proj, mats, layer):
    dlag, m1re, m1im, m2re, m2im, a_re, a_im, dvec = mats

    def spec(r, c):
        return _layer_spec((1, r, c), lambda s: (layer, s, 0, 0))

    def compact(c):
        return _layer_spec((1, SSM_CHUNK, LANES, c), lambda s: (layer, s, 0, 0, 0))

    return pl.pallas_call(
        _ssm_kernel,
        grid=(SSM_SLABS,),
        in_specs=[pl.BlockSpec((TOKENS, LANES), lambda s: (0, SSM_U_COL // LANES + s)),
                  compact(SSM_GROUP),
                  compact(SSM_STATE), compact(SSM_STATE),
                  compact(SSM_STATE), compact(SSM_STATE),
                  spec(1, SLAB_STATE), spec(1, SLAB_STATE), spec(1, SSM_ROW)],
        out_specs=pl.BlockSpec((TOKENS, LANES), lambda s: (0, s)),
        out_shape=jax.ShapeDtypeStruct((TOKENS, SSM_WIDTH), F32),
        scratch_shapes=[pltpu.VMEM((SSM_ROWS, SSM_ROW), F32),
                        pltpu.VMEM((STATE_TILES, SUBLANES * SSM_NCHUNK, LANES), F32),
                        pltpu.VMEM((SSM_ROW, SSM_ROW), BF16),
                        pltpu.VMEM((4, SSM_ROW, SLAB_STATE), BF16)],
        compiler_params=_cparams("arbitrary"),
        name="ssm_chunk_scan",
    )(proj, dlag, m1re, m1im, m2re, m2im, a_re, a_im, dvec)


def _ssm_matrices(a_re, a_im, b_re, b_im, c_re, c_im, d_skip, log_dt):
    lam_re = jnp.minimum(a_re, -1e-4)
    lam_im = a_im
    dt = jnp.exp(log_dt)[:, None]
    mag = jnp.exp(lam_re * dt)
    ab_re, ab_im = mag * jnp.cos(lam_im * dt), mag * jnp.sin(lam_im * dt)
    n_re, n_im = ab_re - 1.0, ab_im
    den = lam_re * lam_re + lam_im * lam_im
    f_re = (n_re * lam_re + n_im * lam_im) / den
    f_im = (n_im * lam_re - n_re * lam_im) / den
    bb_re = f_re[..., None] * b_re - f_im[..., None] * b_im
    bb_im = f_re[..., None] * b_im + f_im[..., None] * b_re
    p_re, p_im = [jnp.ones_like(ab_re)], [jnp.zeros_like(ab_im)]
    for _ in range(SSM_CHUNK):
        p_re.append(p_re[-1] * ab_re - p_im[-1] * ab_im)
        p_im.append(p_re[-2] * ab_im + p_im[-1] * ab_re)
    pw_re, pw_im = jnp.stack(p_re), jnp.stack(p_im)

    cb_re = (c_re[:, None, :, :] * bb_re.transpose(0, 2, 1)[:, :, None, :]
             - c_im[:, None, :, :] * bb_im.transpose(0, 2, 1)[:, :, None, :])
    cb_im = (c_re[:, None, :, :] * bb_im.transpose(0, 2, 1)[:, :, None, :]
             + c_im[:, None, :, :] * bb_re.transpose(0, 2, 1)[:, :, None, :])
    lag_re = pw_re[:SSM_CHUNK].transpose(1, 0, 2)[:, :, None, None, :]
    lag_im = pw_im[:SSM_CHUNK].transpose(1, 0, 2)[:, :, None, None, :]
    kern = jnp.sum(lag_re * cb_re[:, None] - lag_im * cb_im[:, None], axis=-1)
    t_idx = np.arange(SSM_CHUNK)

    back = pw_re[SSM_CHUNK - 1 - t_idx], pw_im[SSM_CHUNK - 1 - t_idx]
    bre_t, bim_t = bb_re.transpose(0, 2, 1), bb_im.transpose(0, 2, 1)
    m1re = (back[0].transpose(1, 0, 2)[:, :, None, :] * bre_t[:, None]
            - back[1].transpose(1, 0, 2)[:, :, None, :] * bim_t[:, None])
    m1im = (back[0].transpose(1, 0, 2)[:, :, None, :] * bim_t[:, None]
            + back[1].transpose(1, 0, 2)[:, :, None, :] * bre_t[:, None])

    fwd_re = pw_re[1:].transpose(1, 0, 2)[:, :, None, :]
    fwd_im = pw_im[1:].transpose(1, 0, 2)[:, :, None, :]
    ca_re = c_re[:, None] * fwd_re - c_im[:, None] * fwd_im
    ca_im = c_re[:, None] * fwd_im + c_im[:, None] * fwd_re

    def rows_by_group(x):
        k, r, c = x.shape[1:]
        x = x.reshape(SSM_SLABS, SLAB_GROUPS, k, r, c).transpose(0, 2, 1, 3, 4)
        return x.reshape(SSM_SLABS, k, SLAB_GROUPS * r, c)

    a_step_re = pw_re[SSM_CHUNK].reshape(SSM_SLABS, 1, SLAB_STATE)
    a_step_im = pw_im[SSM_CHUNK].reshape(SSM_SLABS, 1, SLAB_STATE)
    dvec = jnp.concatenate([d_skip.reshape(SSM_SLABS, 1, LANES)] * SSM_CHUNK, axis=-1)
    return (rows_by_group(kern), rows_by_group(m1re), rows_by_group(m1im),
            rows_by_group(ca_re), rows_by_group(-ca_im), a_step_re, a_step_im, dvec)


def _glu_kernel(y_ref, w_ref, b_ref, o_ref):
    w, bias = _bf16(w_ref[...]), b_ref[...]
    chunks = [slice(r, r + NORM_CHUNK) for r in range(0, y_ref.shape[0], NORM_CHUNK)]

    def project(rows):
        return jnp.dot(y_ref[rows, :].astype(BF16), w, preferred_element_type=F32)

    z_next = project(chunks[0])
    for c, rows in enumerate(chunks):
        z = z_next + bias
        if c + 1 < len(chunks):
            z_next = project(chunks[c + 1])
        o_ref[rows, :] = (z[:, :SSM_WIDTH] * jax.nn.sigmoid(z[:, SSM_WIDTH:])).astype(o_ref.dtype)


def _glu(y, w_glu, b_glu, layer, *, bm):
    return pl.pallas_call(
        _glu_kernel,
        grid=(TOKENS // bm,),
        in_specs=[pl.BlockSpec((bm, SSM_WIDTH), lambda i: (i, 0)),
                  _layer_spec((SSM_WIDTH, 2 * SSM_WIDTH), lambda i: (layer, 0, 0)),
                  _layer_spec((1, 2 * SSM_WIDTH), lambda i: (layer, 0, 0))],
        out_specs=pl.BlockSpec((bm, SSM_WIDTH), lambda i: (i, 0)),
        out_shape=jax.ShapeDtypeStruct((TOKENS, SSM_WIDTH), BF16),
        compiler_params=_cparams("parallel"),
        name="ssm_glu",
    )(y, w_glu, b_glu.reshape(DEPTH, 1, -1))


def _out_proj_kernel(ya_ref, yb_ref, yc_ref, ga_ref, gb_ref, gc_ref, w_ref, x_ref,
                     o_ref, h_ref):
    parts = ((ya_ref, ga_ref[...]), (yb_ref, gb_ref[...]), (yc_ref, gc_ref[...]))

    def normed_chunk(rows):
        return jnp.concatenate([_rms_rows(y_ref[rows, :].astype(F32), g).astype(BF16)
                                for y_ref, g in parts], axis=-1)

    def make_emit():
        w = _bf16(w_ref[...])

        def emit(rows, h):
            o_ref[rows, :] = x_ref[rows, :] + jnp.dot(h, w, preferred_element_type=F32)
        return emit

    _first_step_by_chunks(h_ref, normed_chunk, make_emit)


def _out_proj(y_mla, y_ssm, y_dil, g_mla, g_ssm, g_dil, w_o, x, layer, *, bm, bn):
    def rows(width):
        return pl.BlockSpec((bm, width), lambda i, j: (i, 0))

    def gain(width):
        return _layer_spec((1, width), lambda i, j: (layer, 0, 0))

    return pl.pallas_call(
        _out_proj_kernel,
        grid=(TOKENS // bm, D_MODEL // bn),
        in_specs=[rows(MLA_WIDTH), rows(SSM_WIDTH), rows(DIL_WIDTH),
                  gain(MLA_WIDTH), gain(SSM_WIDTH), gain(DIL_WIDTH),
                  _layer_spec((D_MODEL, bn), lambda i, j: (layer, 0, j)),
                  pl.BlockSpec((bm, bn), lambda i, j: (i, j))],
        out_specs=pl.BlockSpec((bm, bn), lambda i, j: (i, j)),
        out_shape=jax.ShapeDtypeStruct((TOKENS, D_MODEL), F32),
        scratch_shapes=[pltpu.VMEM((bm, D_MODEL), BF16)],
        compiler_params=_cparams("parallel", "arbitrary"),
        name="out_proj",
    )(y_mla, y_ssm, y_dil, g_mla.reshape(DEPTH, 1, -1), g_ssm.reshape(DEPTH, 1, -1),
      g_dil.reshape(DEPTH, 1, -1), w_o, x)


def _ffn_up_kernel(x_ref, g_ref, wg_ref, wu_ref, wd_ref, o_ref, wd_bf16_ref, h_ref):
    g = g_ref[...]
    wd_bf16_ref[...] = wd_ref[...].astype(BF16)

    def make_emit():
        wg, wu = _bf16(wg_ref[...]), _bf16(wu_ref[...])

        def emit(rows, h):
            gate = jnp.dot(h, wg, preferred_element_type=F32)
            up = jnp.dot(h, wu, preferred_element_type=F32)
            o_ref[rows, :] = (jax.nn.silu(gate) * up).astype(o_ref.dtype)
        return emit

    _first_step_by_chunks(h_ref, lambda rows: _rms_rows(x_ref[rows, :], g).astype(BF16),
                          make_emit)


def _ffn_up(x, g, w_gate, w_up, w_down, layer, *, bm, bn):
    col_steps = D_FF // bn
    wd_rows = D_FF // (TOKENS // bm * col_steps)
    return pl.pallas_call(
        _ffn_up_kernel,
        grid=(TOKENS // bm, col_steps),
        in_specs=[pl.BlockSpec((bm, D_MODEL), lambda i, j: (i, 0)),
                  _layer_spec((1, D_MODEL), lambda i, j: (layer, 0, 0)),
                  _layer_spec((D_MODEL, bn), lambda i, j: (layer, 0, j)),
                  _layer_spec((D_MODEL, bn), lambda i, j: (layer, 0, j)),
                  _layer_spec((wd_rows, D_MODEL), lambda i, j: (layer, i * col_steps + j, 0))],
        out_specs=[pl.BlockSpec((bm, bn), lambda i, j: (i, j)),
                   pl.BlockSpec((wd_rows, D_MODEL), lambda i, j: (i * col_steps + j, 0))],
        out_shape=[jax.ShapeDtypeStruct((TOKENS, D_FF), BF16),
                   jax.ShapeDtypeStruct((D_FF, D_MODEL), BF16)],
        scratch_shapes=[pltpu.VMEM((bm, D_MODEL), BF16)],
        compiler_params=_cparams("arbitrary", "arbitrary"),
        name="ffn_gate_up",
    )(x, g.reshape(DEPTH, 1, -1), w_gate, w_up, w_down)


def _ffn_down_kernel(a_ref, w_ref, x_ref, o_ref):
    o_ref[...] = x_ref[...] + jnp.dot(a_ref[...], _bf16(w_ref[...]),
                                      preferred_element_type=F32)


def _ffn_down(a, w_down, x, *, bm, bn):
    return pl.pallas_call(
        _ffn_down_kernel,
        grid=(TOKENS // bm, D_MODEL // bn),
        in_specs=[pl.BlockSpec((bm, D_FF), lambda i, j: (i, 0)),
                  pl.BlockSpec((D_FF, bn), lambda i, j: (0, j)),
                  pl.BlockSpec((bm, bn), lambda i, j: (i, j))],
        out_specs=pl.BlockSpec((bm, bn), lambda i, j: (i, j)),
        out_shape=jax.ShapeDtypeStruct((TOKENS, D_MODEL), F32),
        compiler_params=_cparams("parallel", "arbitrary"),
        name="ffn_down",
    )(a, w_down, x)


def _final_norm_kernel(x_ref, g_ref, o_ref):
    o_ref[...] = _rms_rows(x_ref[...], g_ref[...])


def _final_norm(x, g, *, bm):
    return pl.pallas_call(
        _final_norm_kernel,
        grid=(TOKENS // bm,),
        in_specs=[pl.BlockSpec((bm, D_MODEL), lambda i: (i, 0)),
                  pl.BlockSpec((1, D_MODEL), lambda i: (0, 0))],
        out_specs=pl.BlockSpec((bm, D_MODEL), lambda i: (i, 0)),
        out_shape=jax.ShapeDtypeStruct((TOKENS, D_MODEL), F32),
        compiler_params=_cparams("parallel"),
        name="final_norm",
    )(x, g.reshape(1, -1))


W_IN_SPLIT = 768
W_IN_ROPE_AT = 896
W_IN_REST_AT = 1024


def _pad_w_in_kernel(w_ref, o_ref):
    x = w_ref[...]
    rows = x.shape[0]
    o_ref[:, :W_IN_SPLIT] = x[:, :W_IN_SPLIT].astype(BF16)
    o_ref[:, W_IN_SPLIT:W_IN_ROPE_AT] = jnp.zeros((rows, W_IN_ROPE_AT - W_IN_SPLIT), BF16)
    o_ref[:, W_IN_ROPE_AT:W_IN_ROPE_AT + MLA_ROPE] = (
        x[:, W_IN_SPLIT:W_IN_SPLIT + MLA_ROPE].astype(BF16))
    o_ref[:, W_IN_ROPE_AT + MLA_ROPE:W_IN_REST_AT] = (
        jnp.zeros((rows, W_IN_REST_AT - W_IN_ROPE_AT - MLA_ROPE), BF16))
    o_ref[:, W_IN_REST_AT:] = x[:, W_IN_SPLIT + MLA_ROPE:].astype(BF16)


def _pad_w_in(w_in, *, rows=256):
    lyr, d, n = w_in.shape
    return pl.pallas_call(
        _pad_w_in_kernel,
        grid=(lyr, d // rows),
        in_specs=[pl.BlockSpec((None, rows, n), lambda l, i: (l, i, 0))],
        out_specs=pl.BlockSpec((None, rows, IN_PAD), lambda l, i: (l, i, 0)),
        out_shape=jax.ShapeDtypeStruct((lyr, d, IN_PAD), BF16),
        compiler_params=_cparams("parallel", "parallel"),
        name="pad_w_in",
    )(w_in)


def _pad_wt_uq(w_uq):
    lyr = w_uq.shape[0]
    w = w_uq.reshape(lyr, MLA_Q_LORA, MLA_HEADS, MLA_NOPE + MLA_ROPE)
    w = jnp.pad(w, ((0, 0), (0, 0), (0, 0), (0, MLA_QK_PAD - MLA_NOPE - MLA_ROPE)))
    w = w.reshape(lyr, MLA_Q_LORA, MLA_HEADS * MLA_QK_PAD)
    return w.transpose(0, 2, 1).astype(BF16)


def _split_w_ukv(w_ukv):
    lyr = w_ukv.shape[0]
    w = w_ukv.reshape(lyr, MLA_KV_LORA, MLA_HEADS, MLA_NOPE + MLA_V)
    wk = jnp.pad(w[..., :MLA_NOPE], ((0, 0), (0, 0), (0, 0), (0, MLA_QK_PAD - MLA_NOPE)))
    wk = wk.reshape(lyr, MLA_KV_LORA, -1).astype(BF16)
    wv = w[..., MLA_NOPE:].reshape(lyr, MLA_KV_LORA, -1)
    return wk, wv.transpose(0, 2, 1).astype(BF16)


def _rope_angles():
    inv_freq = ROPE_THETA ** (-jnp.arange(ROPE_HALF, dtype=F32) / ROPE_HALF)
    ang = jnp.arange(SEQ, dtype=F32)[:, None] * inv_freq[None, :]
    return jnp.cos(ang), jnp.sin(ang)


def _rope_tables_k():
    cos, sin = _rope_angles()
    one = jnp.ones((SEQ, MLA_NOPE), F32)
    z = lambda n: jnp.zeros((SEQ, n), F32)
    tail = MLA_QK_PAD - MLA_NOPE - MLA_ROPE
    cos_t = jnp.concatenate([one, cos, cos, z(tail)], axis=1)
    sin_a = jnp.concatenate([z(MLA_NOPE), -sin, z(ROPE_HALF), z(tail)], axis=1)
    sin_b = jnp.concatenate([z(MLA_NOPE), z(ROPE_HALF), sin, z(tail)], axis=1)
    return cos_t, sin_a, sin_b


def kernel(x, g_mix, w_in, g_q, w_uq, g_kv, w_ukv, a_re, a_im, b_re, b_im, c_re, c_im,
           d_skip, log_dt, w_glu, b_glu, g_out_mla, g_out_ssm, g_out_dil, w_o,
           g_ffn, w_gate, w_up, w_down, g_final):
    x = x.reshape(TOKENS, D_MODEL)
    w_in_p = _pad_w_in(w_in)
    wt_q = _pad_wt_uq(w_uq)
    w_k, wt_v = _split_w_ukv(w_ukv)
    w_o_b = w_o.astype(BF16)
    q_scale = (MLA_NOPE + MLA_ROPE) ** -0.5 * math.log2(math.e)
    cos, sin = _rope_angles()
    q_cos_t, q_sin_t = (cos * q_scale).T, (sin * q_scale).T
    k_tabs = _rope_tables_k()
    ssm_mats = jax.vmap(_ssm_matrices)(a_re, a_im, b_re, b_im, c_re, c_im, d_skip, log_dt)

    for l in range(DEPTH):
        proj = _norm_matmul(x, g_mix, w_in_p, l, bm=1024, bn=1024, out_dtype=F32,
                            name="in_proj")
        qt = _q_up(proj, g_q, wt_q, q_cos_t, q_sin_t, l, q_scale, bm=1024)
        k, vt = _kv_up(proj, g_kv, w_k, wt_v, k_tabs, l, bm=1024)
        y_mla = _mla_attention(qt, k, vt)
        y_ssm = _glu(_ssm_scan(proj, ssm_mats, l), w_glu, b_glu, l, bm=1024)
        y_dil = _dil_attention(proj)
        x = _out_proj(y_mla, y_ssm, y_dil, g_out_mla, g_out_ssm, g_out_dil,
                      w_o_b, x, l, bm=1024, bn=1024)
        act, w_down_b = _ffn_up(x, g_ffn, w_gate, w_up, w_down, l, bm=1024, bn=512)
        x = _ffn_down(act, w_down_b, x, bm=1024, bn=512)
    out = _final_norm(x, g_final, bm=512)
    return out.reshape(BATCH, SEQ, D_MODEL)
```

```python
import functools
import math

import jax
import jax.numpy as jnp
import numpy as np
from jax import lax
from jax.experimental import pallas as pl
from jax.experimental.pallas import tpu as pltpu

F32 = jnp.float32
BF16 = jnp.bfloat16

D_MODEL = 2048
BATCH = 4
SEQ = 2048
DEPTH = 4
TOKENS = BATCH * SEQ

MLA_HEADS = 8
MLA_NOPE = 128
MLA_ROPE = 64
MLA_V = 128
MLA_Q_LORA = 512
MLA_KV_LORA = 256
MLA_WIDTH = MLA_HEADS * MLA_V
MLA_QK_PAD = 256
ROPE_THETA = 10000.0
ROPE_HALF = MLA_ROPE // 2

SSM_WIDTH = 512
SSM_GROUP = 16
SSM_GROUPS = 32
SSM_STATE = 64
SSM_CHUNK = 8

DIL_WIDTH = 512
DIL_HEAD_DIM = 64
DIL_HEADS = 8
DIL_PATTERNS = ((128, 1), (512, 4), (2048, 16))
BLOCK = 128
DIL_GROUP = 1

IN_PAD = 3072
DIL_Q_COL = 1536
DIL_K_COL = 2048
DIL_V_COL = 2560
SSM_U_COL = 1024
D_FF = 5632
NORM_EPS = 1e-6

LANES = 128
VMEM_LIMIT_BYTES = 56 * 1024 * 1024

NT_DIMS = (((1,), (1,)), ((), ()))


def _cparams(*semantics):
    return pltpu.CompilerParams(dimension_semantics=semantics,
                                vmem_limit_bytes=VMEM_LIMIT_BYTES)


def _rms_rows(x, g):
    ms = jnp.mean(x * x, axis=-1, keepdims=True)
    return x * lax.rsqrt(ms + NORM_EPS) * g


NORM_CHUNK = 256


def _first_step_by_chunks(h_ref, normed_chunk, make_emit):
    @pl.when(pl.program_id(1) == 0)
    def _():
        emit = make_emit()
        for r in range(0, h_ref.shape[0], NORM_CHUNK):
            rows = slice(r, r + NORM_CHUNK)
            h = normed_chunk(rows)
            h_ref[rows, :] = h
            emit(rows, h)

    @pl.when(pl.program_id(1) != 0)
    def _():
        make_emit()(slice(None), h_ref[...])


def _bf16(w):
    return w if w.dtype == BF16 else w.astype(BF16)


def _layer_spec(shape, index_map):
    return pl.BlockSpec((None,) + tuple(shape), index_map)


def _norm_matmul_kernel(x_ref, g_ref, w_ref, o_ref, h_ref):
    g = g_ref[...]

    def make_emit():
        w = _bf16(w_ref[...])

        def emit(rows, h):
            o_ref[rows, :] = jnp.dot(h, w, preferred_element_type=F32).astype(o_ref.dtype)
        return emit

    _first_step_by_chunks(h_ref, lambda rows: _rms_rows(x_ref[rows, :], g).astype(BF16),
                          make_emit)


def _norm_matmul(x, g, w, layer, *, bm, bn, out_dtype, name):
    m, k = x.shape
    n = w.shape[2]
    return pl.pallas_call(
        _norm_matmul_kernel,
        grid=(m // bm, n // bn),
        in_specs=[pl.BlockSpec((bm, k), lambda i, j: (i, 0)),
                  _layer_spec((1, k), lambda i, j: (layer, 0, 0)),
                  _layer_spec((k, bn), lambda i, j: (layer, 0, j))],
        out_specs=pl.BlockSpec((bm, bn), lambda i, j: (i, j)),
        out_shape=jax.ShapeDtypeStruct((m, n), out_dtype),
        scratch_shapes=[pltpu.VMEM((bm, k), BF16)],
        compiler_params=_cparams("parallel", "arbitrary"),
        name=name,
    )(x, g.reshape(DEPTH, 1, k), w)


def _q_up_kernel(c_ref, g_ref, wt_ref, cos_ref, sin_ref, o_ref, *, scale):
    h = _rms_rows(c_ref[...], g_ref[...]).astype(BF16)
    cos, sin = cos_ref[...], sin_ref[...]

    def project(hd):
        rows = slice(hd * MLA_QK_PAD, (hd + 1) * MLA_QK_PAD)
        return lax.dot_general(wt_ref[rows, :], h, NT_DIMS, preferred_element_type=F32)

    def finish(hd, qt):
        r0 = hd * MLA_QK_PAD
        n1, n2, n3 = MLA_NOPE, MLA_NOPE + ROPE_HALF, MLA_NOPE + MLA_ROPE
        x1, x2 = qt[n1:n2], qt[n2:n3]
        o_ref[r0:r0 + n1, :] = (qt[:n1] * scale).astype(o_ref.dtype)
        o_ref[r0 + n1:r0 + n2, :] = (x1 * cos - x2 * sin).astype(o_ref.dtype)
        o_ref[r0 + n2:r0 + n3, :] = (x2 * cos + x1 * sin).astype(o_ref.dtype)
        o_ref[r0 + n3:r0 + MLA_QK_PAD, :] = qt[n3:].astype(o_ref.dtype)

    qt_next = project(0)
    for hd in range(MLA_HEADS):
        qt = qt_next
        if hd + 1 < MLA_HEADS:
            qt_next = project(hd + 1)
        finish(hd, qt)


def _q_up(proj, g_q, wt_q, cos_t, sin_t, layer, scale, *, bm):
    pos_blocks = SEQ // bm
    tab_spec = pl.BlockSpec((ROPE_HALF, bm), lambda i: (0, i % pos_blocks))
    n = MLA_HEADS * MLA_QK_PAD
    return pl.pallas_call(
        functools.partial(_q_up_kernel, scale=scale),
        grid=(TOKENS // bm,),
        in_specs=[pl.BlockSpec((bm, MLA_Q_LORA), lambda i: (i, 0)),
                  _layer_spec((1, MLA_Q_LORA), lambda i: (layer, 0, 0)),
                  _layer_spec((n, MLA_Q_LORA), lambda i: (layer, 0, 0)),
                  tab_spec, tab_spec],
        out_specs=pl.BlockSpec((n, bm), lambda i: (0, i)),
        out_shape=jax.ShapeDtypeStruct((n, TOKENS), BF16),
        compiler_params=_cparams("parallel"),
        name="mla_q_up",
    )(proj, g_q.reshape(DEPTH, 1, -1), wt_q, cos_t, sin_t)


def _rope_pad(x, cos_t, sin_a, sin_b):
    up = pltpu.roll(x, MLA_QK_PAD - ROPE_HALF, axis=1)
    dn = pltpu.roll(x, ROPE_HALF, axis=1)
    return x * cos_t + up * sin_a + dn * sin_b


def _kv_up_kernel(c_ref, kr_ref, g_ref, wk_ref, wvt_ref, cos_ref, sina_ref, sinb_ref,
                  k_ref, vt_ref):
    h = _rms_rows(c_ref[...], g_ref[...]).astype(BF16)
    kn = jnp.dot(h, wk_ref[...], preferred_element_type=F32)
    vt = lax.dot_general(wvt_ref[...], h, NT_DIMS, preferred_element_type=F32)
    k_pe = _rope_pad(kr_ref[...], cos_ref[...], sina_ref[...], sinb_ref[...])
    for hd in range(MLA_HEADS):
        sl = slice(hd * MLA_QK_PAD, (hd + 1) * MLA_QK_PAD)
        k_ref[:, sl] = (kn[:, sl] + k_pe).astype(k_ref.dtype)
    vt_ref[...] = vt.astype(vt_ref.dtype)


def _kv_up(proj, g_kv, w_k, wt_v, tabs, layer, *, bm):
    pos_blocks = SEQ // bm
    tab_spec = pl.BlockSpec((bm, MLA_QK_PAD), lambda i: (i % pos_blocks, 0))
    nk = MLA_HEADS * MLA_QK_PAD
    return pl.pallas_call(
        _kv_up_kernel,
        grid=(TOKENS // bm,),
        in_specs=[pl.BlockSpec((bm, MLA_KV_LORA), lambda i: (i, 2)),
                  pl.BlockSpec((bm, MLA_QK_PAD), lambda i: (i, 3)),
                  _layer_spec((1, MLA_KV_LORA), lambda i: (layer, 0, 0)),
                  _layer_spec((MLA_KV_LORA, nk), lambda i: (layer, 0, 0)),
                  _layer_spec((MLA_WIDTH, MLA_KV_LORA), lambda i: (layer, 0, 0)),
                  tab_spec, tab_spec, tab_spec],
        out_specs=[pl.BlockSpec((bm, nk), lambda i: (i, 0)),
                   pl.BlockSpec((MLA_WIDTH, bm), lambda i: (0, i))],
        out_shape=[jax.ShapeDtypeStruct((TOKENS, nk), BF16),
                   jax.ShapeDtypeStruct((MLA_WIDTH, TOKENS), BF16)],
        compiler_params=_cparams("parallel"),
        name="mla_kv_up",
    )(proj, proj, g_kv.reshape(DEPTH, 1, -1), w_k, wt_v, *tabs)


ATT_BQ = 256
ATT_BK = 256
ATT_NQ = SEQ // ATT_BQ
ATT_ONES = 16
ATT_GROUP = 3


def _mla_attn_kernel(qt_ref, k_ref, vt_ref, o_ref, m_sc, acc_sc):
    key = lax.broadcasted_iota(jnp.int32, (ATT_BK, ATT_BQ), 0)
    qry = lax.broadcasted_iota(jnp.int32, (ATT_BK, ATT_BQ), 1)
    causal = key <= qry
    tiles = [(i, j) for j in range(ATT_NQ) for i in range(j, ATT_NQ)]

    def scores(i, j):
        kj = k_ref[j * ATT_BK:(j + 1) * ATT_BK, :]
        qi = qt_ref[:, i * ATT_BQ:(i + 1) * ATT_BQ]
        return jnp.dot(kj, qi, preferred_element_type=F32)

    def softmax(i, j, s):
        if i == j:
            s = jnp.where(causal, s, -jnp.inf)
        m_blk = jnp.max(s, axis=0, keepdims=True)
        if j == 0:
            m_new, alpha = m_blk, None
        else:
            m_old = m_sc[i]
            m_new = jnp.maximum(m_old, m_blk)
            alpha = jnp.exp2(m_old - m_new)
        if i != j:
            m_sc[i] = m_new
        return jnp.exp2(s - m_new).astype(BF16), alpha

    ones_rows = jnp.ones((ATT_ONES, ATT_BK), BF16)

    def values(i, j, p, alpha):
        vj = jnp.concatenate([vt_ref[:, j * ATT_BK:(j + 1) * ATT_BK], ones_rows], axis=0)
        acc = jnp.dot(vj, p, preferred_element_type=F32)
        if alpha is not None:
            acc = alpha * acc_sc[i] + acc
        if i == j:
            out = acc[:MLA_V] / acc[MLA_V:MLA_V + 1]
            o_ref[i * ATT_BQ:(i + 1) * ATT_BQ, :] = out.T.astype(o_ref.dtype)
        else:
            acc_sc[i] = acc

    groups = [tiles[g:g + ATT_GROUP] for g in range(0, len(tiles), ATT_GROUP)]
    n = len(groups)

    def stage_scores(g):
        return [scores(*tile) for tile in groups[g]]

    def stage_softmax(g, s_list):
        return [softmax(*tile, s) for tile, s in zip(groups[g], s_list)]

    def stage_values(g, p_list):
        for tile, p_alpha in zip(groups[g], p_list):
            values(*tile, *p_alpha)

    s_ready = {g: stage_scores(g) for g in range(min(2, n))}
    p_ready = {0: stage_softmax(0, s_ready.pop(0))}
    for g in range(n):
        if g + 2 < n:
            s_ready[g + 2] = stage_scores(g + 2)
        if g + 1 < n:
            p_ready[g + 1] = stage_softmax(g + 1, s_ready.pop(g + 1))
        stage_values(g, p_ready.pop(g))


def _mla_attention(qt, k, vt):
    return pl.pallas_call(
        _mla_attn_kernel,
        grid=(BATCH, MLA_HEADS),
        in_specs=[pl.BlockSpec((MLA_QK_PAD, SEQ), lambda b, h: (h, b)),
                  pl.BlockSpec((SEQ, MLA_QK_PAD), lambda b, h: (b, h)),
                  pl.BlockSpec((MLA_V, SEQ), lambda b, h: (h, b))],
        out_specs=pl.BlockSpec((SEQ, MLA_V), lambda b, h: (b, h)),
        out_shape=jax.ShapeDtypeStruct((TOKENS, MLA_WIDTH), BF16),
        scratch_shapes=[pltpu.VMEM((ATT_NQ, 1, ATT_BQ), F32),
                        pltpu.VMEM((ATT_NQ, MLA_V + ATT_ONES, ATT_BQ), F32)],
        compiler_params=_cparams("parallel", "parallel"),
        name="mla_attention",
    )(qt, k, vt)


def _dil_attn_kernel(q_ref, k_ref, v_ref, o_ref, m_sc, l_sc, n_sc):
    row2 = lax.broadcasted_iota(jnp.int32, (BLOCK, 2 * BLOCK), 0)
    col2 = lax.broadcasted_iota(jnp.int32, (BLOCK, 2 * BLOCK), 1)
    dist = row2 + BLOCK - col2
    band = (dist >= 0) & (dist <= BLOCK)
    row1 = lax.broadcasted_iota(jnp.int32, (BLOCK, BLOCK), 0)
    col1 = lax.broadcasted_iota(jnp.int32, (BLOCK, BLOCK), 1)
    tri = row1 >= col1
    q_scale = DIL_HEAD_DIM ** -0.5 * math.log2(math.e)
    heads = LANES // DIL_HEAD_DIM
    n_patterns = len(DIL_PATTERNS)

    def rows_at(start, dil):
        return pl.ds(start, BLOCK) if dil == 1 else pl.ds(start, BLOCK, stride=dil)

    blocks = [(pi, dil, r, n) for pi, (_, dil) in enumerate(reversed(DIL_PATTERNS))
              for r in range(dil) for n in range(SEQ // dil // BLOCK)]

    assert heads == 2
    head0 = lax.broadcasted_iota(jnp.int32, (BLOCK, LANES), 1) < DIL_HEAD_DIM

    def load(pi, dil, r, n):
        rows = rows_at(r + dil * BLOCK * n, dil)
        q = q_ref[rows, :] * q_scale
        if n == 0:
            k, v, mask = k_ref[rows, :], v_ref[rows, :], tri
        else:
            prev = rows_at(r + dil * BLOCK * (n - 1), dil)
            k = jnp.concatenate([k_ref[prev, :], k_ref[rows, :]], axis=0)
            v = jnp.concatenate([v_ref[prev, :], v_ref[rows, :]], axis=0)
            mask = band
        v_head0 = lax.broadcasted_iota(jnp.int32, v.shape, 1) < DIL_HEAD_DIM
        q_heads = (jnp.where(head0, q, 0.0).astype(BF16), jnp.where(head0, 0.0, q).astype(BF16))
        v_heads = (jnp.where(v_head0, v, 1.0).astype(BF16),
                   jnp.where(v_head0, 1.0, v).astype(BF16))
        return rows, q_heads, k.astype(BF16), v_heads, mask

    def scores(blk, hd):
        return lax.dot_general(blk[1][hd], blk[2], NT_DIMS, preferred_element_type=F32)

    def softmax(blk, s):
        s = jnp.where(blk[4], s, -jnp.inf)
        m = jnp.max(s, axis=-1, keepdims=True)
        return jnp.exp2(s - m).astype(BF16), m

    def values(blk, hd, p):
        return jnp.dot(p, blk[3][hd], preferred_element_type=F32)

    def merge(pi, rows, parts):
        (m_a, ext_a), (m_b, ext_b) = parts
        m2 = jnp.where(head0, m_a, m_b)
        a2 = jnp.where(head0, ext_a, ext_b)
        l2 = pltpu.roll(jnp.where(head0, ext_b, ext_a), DIL_HEAD_DIM, axis=1)
        if pi > 0:
            m_old = m_sc[rows, :]
            m_new = jnp.maximum(m_old, m2)
            w_old, w_new = jnp.exp2(m_old - m_new), jnp.exp2(m2 - m_new)
            l2 = w_old * l_sc[rows, :] + w_new * l2
            a2 = w_old * n_sc[rows, :] + w_new * a2
            m2 = m_new
        if pi == n_patterns - 1:
            o_ref[rows, :] = a2 / l2
        else:
            m_sc[rows, :] = m2
            l_sc[rows, :] = l2
            n_sc[rows, :] = a2

    groups = [blocks[g:g + DIL_GROUP] for g in range(0, len(blocks), DIL_GROUP)]
    n = len(groups)

    def stage_scores(g):
        out = []
        for spec in groups[g]:
            blk = load(*spec)
            out.append((blk, [scores(blk, hd) for hd in range(heads)]))
        return out

    def stage_softmax(scored):
        return [(blk, [softmax(blk, s) for s in s_list]) for blk, s_list in scored]

    def stage_values(g, probs):
        for spec, (blk, p_list) in zip(groups[g], probs):
            merge(spec[0], blk[0],
                  [(m_rows, values(blk, hd, p)) for hd, (p, m_rows) in enumerate(p_list)])

    s_ready = {g: stage_scores(g) for g in range(min(2, n))}
    p_ready = {0: stage_softmax(s_ready.pop(0))}
    for g in range(n):
        if g + 2 < n:
            s_ready[g + 2] = stage_scores(g + 2)
        if g + 1 < n:
            p_ready[g + 1] = stage_softmax(s_ready.pop(g + 1))
        stage_values(g, p_ready.pop(g))


def _dil_attention(proj):
    def spec(col0):
        return pl.BlockSpec((SEQ, LANES), lambda b, hp: (b, col0 // LANES + hp))

    return pl.pallas_call(
        _dil_attn_kernel,
        grid=(BATCH, DIL_WIDTH // LANES),
        in_specs=[spec(DIL_Q_COL), spec(DIL_K_COL), spec(DIL_V_COL)],
        out_specs=pl.BlockSpec((SEQ, LANES), lambda b, hp: (b, hp)),
        out_shape=jax.ShapeDtypeStruct((TOKENS, DIL_WIDTH), F32),
        scratch_shapes=[pltpu.VMEM((SEQ, LANES), F32)] * 3,
        compiler_params=_cparams("parallel", "parallel"),
        name="dilated_attention",
    )(proj, proj, proj)


SLAB_GROUPS = LANES // SSM_GROUP
SSM_SLABS = SSM_GROUPS // SLAB_GROUPS
SSM_NCHUNK = SEQ // SSM_CHUNK
SSM_ROWS = BATCH * SSM_NCHUNK
SSM_ROW = SSM_CHUNK * LANES
SLAB_STATE = SLAB_GROUPS * SSM_STATE
STATE_TILES = SLAB_STATE // LANES
SUBLANES = 8


def _expand_block_diag(x, width):
    rows, c = x.shape
    src = lax.broadcasted_iota(jnp.int32, (c, width), 0)
    dst = lax.broadcasted_iota(jnp.int32, (c, width), 1)
    repeat = jnp.where(dst % c == src, 1.0, 0.0).astype(BF16)
    wide = jnp.dot(x.astype(BF16), repeat, preferred_element_type=F32)
    row_g = lax.broadcasted_iota(jnp.int32, (rows, width), 0) // (rows // SLAB_GROUPS)
    col_g = lax.broadcasted_iota(jnp.int32, (rows, width), 1) // c
    return jnp.where(row_g == col_g, wide, 0.0).astype(BF16)


def _ssm_kernel(u_ref, dlag_ref, m1re_ref, m1im_ref, m2re_ref, m2im_ref,
                are_ref, aim_ref, d_ref, y_ref, fold_ref, st_ref, t0_sc, op_sc):
    lag_blocks = [_expand_block_diag(dlag_ref[0, tau], LANES) for tau in range(SSM_CHUNK)]
    no_block = jnp.zeros((LANES, LANES), BF16)
    for t in range(SSM_CHUNK):
        for s in range(SSM_CHUNK):
            t0_sc[t * LANES:(t + 1) * LANES, s * LANES:(s + 1) * LANES] = (
                lag_blocks[s - t] if s >= t else no_block)
    for which, ref in enumerate((m1re_ref, m1im_ref, m2re_ref, m2im_ref)):
        for t in range(SSM_CHUNK):
            op_sc[which, t * LANES:(t + 1) * LANES, :] = _expand_block_diag(ref[0, t], SLAB_STATE)

    for b in range(BATCH):
        rows = slice(b * SSM_NCHUNK, (b + 1) * SSM_NCHUNK)
        for t in range(SSM_CHUNK):
            fold_ref[rows, t * LANES:(t + 1) * LANES] = (
                u_ref[pl.ds(b * SEQ + t, SSM_NCHUNK, stride=SSM_CHUNK), :])
    u = fold_ref[...]
    ub = u.astype(BF16)

    loc_re = jnp.dot(ub, op_sc[0], preferred_element_type=F32)
    loc_im = jnp.dot(ub, op_sc[1], preferred_element_type=F32)
    fold_ref[...] = jnp.dot(ub, t0_sc[...], preferred_element_type=F32) + u * d_ref[0]
    for b in range(BATCH):
        rows = slice(b * SSM_NCHUNK, (b + 1) * SSM_NCHUNK)
        for k in range(STATE_TILES):
            lanes = slice(k * LANES, (k + 1) * LANES)
            st_ref[k, pl.ds(b, SSM_NCHUNK, stride=SUBLANES), :] = loc_re[rows, lanes]
            st_ref[k, pl.ds(BATCH + b, SSM_NCHUNK, stride=SUBLANES), :] = loc_im[rows, lanes]

    a_re, a_im = are_ref[0], aim_ref[0]
    upper = lax.broadcasted_iota(jnp.int32, (SUBLANES, LANES), 0) < BATCH
    mul_same, mul_swap = [], []
    for k in range(STATE_TILES):
        lanes = slice(k * LANES, (k + 1) * LANES)
        mul_same.append(jnp.broadcast_to(a_re[:, lanes], (SUBLANES, LANES)))
        im = jnp.broadcast_to(a_im[:, lanes], (SUBLANES, LANES))
        mul_swap.append(jnp.where(upper, -im, im))

    def chunk_step(c, state):
        r0 = pl.multiple_of(c * SUBLANES, SUBLANES)
        new = []
        for k in range(STATE_TILES):
            loc = st_ref[k, pl.ds(r0, SUBLANES), :]
            st_ref[k, pl.ds(r0, SUBLANES), :] = state[k]
            swapped = pltpu.roll(state[k], BATCH, axis=0)
            new.append(mul_same[k] * state[k] + mul_swap[k] * swapped + loc)
        return tuple(new)

    zero = jnp.zeros((SUBLANES, LANES), F32)
    lax.fori_loop(0, SSM_NCHUNK, chunk_step, (zero,) * STATE_TILES, unroll=4)

    def entering(offset):
        return jnp.concatenate(
            [jnp.concatenate([st_ref[k, pl.ds(offset + b, SSM_NCHUNK, stride=SUBLANES), :]
                              for k in range(STATE_TILES)], axis=1)
             for b in range(BATCH)], axis=0)

    y = fold_ref[...]
    for offset, which in ((0, 2), (BATCH, 3)):
        m2t = op_sc[which]
        y = y + lax.dot_general(entering(offset).astype(BF16), m2t, NT_DIMS,
                                preferred_element_type=F32)
    fold_ref[...] = jax.nn.gelu(y, approximate=True)

    for b in range(BATCH):
        rows = slice(b * SSM_NCHUNK, (b + 1) * SSM_NCHUNK)
        for t in range(SSM_CHUNK):
            y_ref[pl.ds(b * SEQ + t, SSM_NCHUNK, stride=SSM_CHUNK), :] = (
                fold_ref[rows, t * LANES:(t + 1) * LANES])


def _ssm_scan(proj, mats, layer):
    dlag, m1re, m1im, m2re, m2im, a_re, a_im, dvec = mats

    def spec(r, c):
        return _layer_spec((1, r, c), lambda s: (layer, s, 0, 0))

    def compact(c):
        return _layer_spec((1, SSM_CHUNK, LANES, c), lambda s: (layer, s, 0, 0, 0))

    return pl.pallas_call(
        _ssm_kernel,
        grid=(SSM_SLABS,),
        in_specs=[pl.BlockSpec((TOKENS, LANES), lambda s: (0, SSM_U_COL // LANES + s)),
                  compact(SSM_GROUP),
                  compact(SSM_STATE), compact(SSM_STATE),
                  compact(SSM_STATE), compact(SSM_STATE),
                  spec(1, SLAB_STATE), spec(1, SLAB_STATE), spec(1, SSM_ROW)],
        out_specs=pl.BlockSpec((TOKENS, LANES), lambda s: (0, s)),
        out_shape=jax.ShapeDtypeStruct((TOKENS, SSM_WIDTH), F32),
        scratch_shapes=[pltpu.VMEM((SSM_ROWS, SSM_ROW), F32),
                        pltpu.VMEM((STATE_TILES, SUBLANES * SSM_NCHUNK, LANES), F32),
                        pltpu.VMEM((SSM_ROW, SSM_ROW), BF16),
                        pltpu.VMEM((4, SSM_ROW, SLAB_STATE), BF16)],
        compiler_params=_cparams("arbitrary"),
        name="ssm_chunk_scan",
    )(proj, dlag, m1re, m1im, m2re, m2im, a_re, a_im, dvec)


def _ssm_matrices(a_re, a_im, b_re, b_im, c_re, c_im, d_skip, log_dt):
    lam_re = jnp.minimum(a_re, -1e-4)
    lam_im = a_im
    dt = jnp.exp(log_dt)[:, None]
    mag = jnp.exp(lam_re * dt)
    ab_re, ab_im = mag * jnp.cos(lam_im * dt), mag * jnp.sin(lam_im * dt)
    n_re, n_im = ab_re - 1.0, ab_im
    den = lam_re * lam_re + lam_im * lam_im
    f_re = (n_re * lam_re + n_im * lam_im) / den
    f_im = (n_im * lam_re - n_re * lam_im) / den
    bb_re = f_re[..., None] * b_re - f_im[..., None] * b_im
    bb_im = f_re[..., None] * b_im + f_im[..., None] * b_re
    p_re, p_im = [jnp.ones_like(ab_re)], [jnp.zeros_like(ab_im)]
    for _ in range(SSM_CHUNK):
        p_re.append(p_re[-1] * ab_re - p_im[-1] * ab_im)
        p_im.append(p_re[-2] * ab_im + p_im[-1] * ab_re)
    pw_re, pw_im = jnp.stack(p_re), jnp.stack(p_im)

    cb_re = (c_re[:, None, :, :] * bb_re.transpose(0, 2, 1)[:, :, None, :]
             - c_im[:, None, :, :] * bb_im.transpose(0, 2, 1)[:, :, None, :])
    cb_im = (c_re[:, None, :, :] * bb_im.transpose(0, 2, 1)[:, :, None, :]
             + c_im[:, None, :, :] * bb_re.transpose(0, 2, 1)[:, :, None, :])
    lag_re = pw_re[:SSM_CHUNK].transpose(1, 0, 2)[:, :, None, None, :]
    lag_im = pw_im[:SSM_CHUNK].transpose(1, 0, 2)[:, :, None, None, :]
    kern = jnp.sum(lag_re * cb_re[:, None] - lag_im * cb_im[:, None], axis=-1)
    t_idx = np.arange(SSM_CHUNK)

    back = pw_re[SSM_CHUNK - 1 - t_idx], pw_im[SSM_CHUNK - 1 - t_idx]
    bre_t, bim_t = bb_re.transpose(0, 2, 1), bb_im.transpose(0, 2, 1)
    m1re = (back[0].transpose(1, 0, 2)[:, :, None, :] * bre_t[:, None]
            - back[1].transpose(1, 0, 2)[:, :, None, :] * bim_t[:, None])
    m1im = (back[0].transpose(1, 0, 2)[:, :, None, :] * bim_t[:, None]
            + back[1].transpose(1, 0, 2)[:, :, None, :] * bre_t[:, None])

    fwd_re = pw_re[1:].transpose(1, 0, 2)[:, :, None, :]
    fwd_im = pw_im[1:].transpose(1, 0, 2)[:, :, None, :]
    ca_re = c_re[:, None] * fwd_re - c_im[:, None] * fwd_im
    ca_im = c_re[:, None] * fwd_im + c_im[:, None] * fwd_re

    def rows_by_group(x):
        k, r, c = x.shape[1:]
        x = x.reshape(SSM_SLABS, SLAB_GROUPS, k, r, c).transpose(0, 2, 1, 3, 4)
        return x.reshape(SSM_SLABS, k, SLAB_GROUPS * r, c)

    a_step_re = pw_re[SSM_CHUNK].reshape(SSM_SLABS, 1, SLAB_STATE)
    a_step_im = pw_im[SSM_CHUNK].reshape(SSM_SLABS, 1, SLAB_STATE)
    dvec = jnp.concatenate([d_skip.reshape(SSM_SLABS, 1, LANES)] * SSM_CHUNK, axis=-1)
    return (rows_by_group(kern), rows_by_group(m1re), rows_by_group(m1im),
            rows_by_group(ca_re), rows_by_group(-ca_im), a_step_re, a_step_im, dvec)


def _glu_kernel(y_ref, w_ref, b_ref, o_ref):
    w, bias = _bf16(w_ref[...]), b_ref[...]
    chunks = [slice(r, r + NORM_CHUNK) for r in range(0, y_ref.shape[0], NORM_CHUNK)]

    def project(rows):
        return jnp.dot(y_ref[rows, :].astype(BF16), w, preferred_element_type=F32)

    z_next = project(chunks[0])
    for c, rows in enumerate(chunks):
        z = z_next + bias
        if c + 1 < len(chunks):
            z_next = project(chunks[c + 1])
        o_ref[rows, :] = (z[:, :SSM_WIDTH] * jax.nn.sigmoid(z[:, SSM_WIDTH:])).astype(o_ref.dtype)


def _glu(y, w_glu, b_glu, layer, *, bm):
    return pl.pallas_call(
        _glu_kernel,
        grid=(TOKENS // bm,),
        in_specs=[pl.BlockSpec((bm, SSM_WIDTH), lambda i: (i, 0)),
                  _layer_spec((SSM_WIDTH, 2 * SSM_WIDTH), lambda i: (layer, 0, 0)),
                  _layer_spec((1, 2 * SSM_WIDTH), lambda i: (layer, 0, 0))],
        out_specs=pl.BlockSpec((bm, SSM_WIDTH), lambda i: (i, 0)),
        out_shape=jax.ShapeDtypeStruct((TOKENS, SSM_WIDTH), BF16),
        compiler_params=_cparams("parallel"),
        name="ssm_glu",
    )(y, w_glu, b_glu.reshape(DEPTH, 1, -1))


def _out_proj_kernel(ya_ref, yb_ref, yc_ref, ga_ref, gb_ref, gc_ref, w_ref, x_ref,
                     o_ref, h_ref):
    parts = ((ya_ref, ga_ref[...]), (yb_ref, gb_ref[...]), (yc_ref, gc_ref[...]))

    def normed_chunk(rows):
        return jnp.concatenate([_rms_rows(y_ref[rows, :].astype(F32), g).astype(BF16)
                                for y_ref, g in parts], axis=-1)

    def make_emit():
        w = _bf16(w_ref[...])

        def emit(rows, h):
            o_ref[rows, :] = x_ref[rows, :] + jnp.dot(h, w, preferred_element_type=F32)
        return emit

    _first_step_by_chunks(h_ref, normed_chunk, make_emit)


def _out_proj(y_mla, y_ssm, y_dil, g_mla, g_ssm, g_dil, w_o, x, layer, *, bm, bn):
    def rows(width):
        return pl.BlockSpec((bm, width), lambda i, j: (i, 0))

    def gain(width):
        return _layer_spec((1, width), lambda i, j: (layer, 0, 0))

    return pl.pallas_call(
        _out_proj_kernel,
        grid=(TOKENS // bm, D_MODEL // bn),
        in_specs=[rows(MLA_WIDTH), rows(SSM_WIDTH), rows(DIL_WIDTH),
                  gain(MLA_WIDTH), gain(SSM_WIDTH), gain(DIL_WIDTH),
                  _layer_spec((D_MODEL, bn), lambda i, j: (layer, 0, j)),
                  pl.BlockSpec((bm, bn), lambda i, j: (i, j))],
        out_specs=pl.BlockSpec((bm, bn), lambda i, j: (i, j)),
        out_shape=jax.ShapeDtypeStruct((TOKENS, D_MODEL), F32),
        scratch_shapes=[pltpu.VMEM((bm, D_MODEL), BF16)],
        compiler_params=_cparams("parallel", "arbitrary"),
        name="out_proj",
    )(y_mla, y_ssm, y_dil, g_mla.reshape(DEPTH, 1, -1), g_ssm.reshape(DEPTH, 1, -1),
      g_dil.reshape(DEPTH, 1, -1), w_o, x)


def _ffn_up_kernel(x_ref, g_ref, wg_ref, wu_ref, wd_ref, o_ref, wd_bf16_ref, h_ref):
    g = g_ref[...]
    wd_bf16_ref[...] = wd_ref[...].astype(BF16)

    def make_emit():
        wg, wu = _bf16(wg_ref[...]), _bf16(wu_ref[...])

        def emit(rows, h):
            gate = jnp.dot(h, wg, preferred_element_type=F32)
            up = jnp.dot(h, wu, preferred_element_type=F32)
            o_ref[rows, :] = (jax.nn.silu(gate) * up).astype(o_ref.dtype)
        return emit

    _first_step_by_chunks(h_ref, lambda rows: _rms_rows(x_ref[rows, :], g).astype(BF16),
                          make_emit)


def _ffn_up(x, g, w_gate, w_up, w_down, layer, *, bm, bn):
    col_steps = D_FF // bn
    wd_rows = D_FF // (TOKENS // bm * col_steps)
    return pl.pallas_call(
        _ffn_up_kernel,
        grid=(TOKENS // bm, col_steps),
        in_specs=[pl.BlockSpec((bm, D_MODEL), lambda i, j: (i, 0)),
                  _layer_spec((1, D_MODEL), lambda i, j: (layer, 0, 0)),
                  _layer_spec((D_MODEL, bn), lambda i, j: (layer, 0, j)),
                  _layer_spec((D_MODEL, bn), lambda i, j: (layer, 0, j)),
                  _layer_spec((wd_rows, D_MODEL), lambda i, j: (layer, i * col_steps + j, 0))],
        out_specs=[pl.BlockSpec((bm, bn), lambda i, j: (i, j)),
                   pl.BlockSpec((wd_rows, D_MODEL), lambda i, j: (i * col_steps + j, 0))],
        out_shape=[jax.ShapeDtypeStruct((TOKENS, D_FF), BF16),
                   jax.ShapeDtypeStruct((D_FF, D_MODEL), BF16)],
        scratch_shapes=[pltpu.VMEM((bm, D_MODEL), BF16)],
        compiler_params=_cparams("arbitrary", "arbitrary"),
        name="ffn_gate_up",
    )(x, g.reshape(DEPTH, 1, -1), w_gate, w_up, w_down)


def _ffn_down_kernel(a_ref, w_ref, x_ref, o_ref):
    o_ref[...] = x_ref[...] + jnp.dot(a_ref[...], _bf16(w_ref[...]),
                                      preferred_element_type=F32)


def _ffn_down(a, w_down, x, *, bm, bn):
    return pl.pallas_call(
        _ffn_down_kernel,
        grid=(TOKENS // bm, D_MODEL // bn),
        in_specs=[pl.BlockSpec((bm, D_FF), lambda i, j: (i, 0)),
                  pl.BlockSpec((D_FF, bn), lambda i, j: (0, j)),
                  pl.BlockSpec((bm, bn), lambda i, j: (i, j))],
        out_specs=pl.BlockSpec((bm, bn), lambda i, j: (i, j)),
        out_shape=jax.ShapeDtypeStruct((TOKENS, D_MODEL), F32),
        compiler_params=_cparams("parallel", "arbitrary"),
        name="ffn_down",
    )(a, w_down, x)


def _final_norm_kernel(x_ref, g_ref, o_ref):
    o_ref[...] = _rms_rows(x_ref[...], g_ref[...])


def _final_norm(x, g, *, bm):
    return pl.pallas_call(
        _final_norm_kernel,
        grid=(TOKENS // bm,),
        in_specs=[pl.BlockSpec((bm, D_MODEL), lambda i: (i, 0)),
                  pl.BlockSpec((1, D_MODEL), lambda i: (0, 0))],
        out_specs=pl.BlockSpec((bm, D_MODEL), lambda i: (i, 0)),
        out_shape=jax.ShapeDtypeStruct((TOKENS, D_MODEL), F32),
        compiler_params=_cparams("parallel"),
        name="final_norm",
    )(x, g.reshape(1, -1))


W_IN_SPLIT = 768
W_IN_ROPE_AT = 896
W_IN_REST_AT = 1024


def _pad_w_in_kernel(w_ref, o_ref):
    x = w_ref[...]
    rows = x.shape[0]
    o_ref[:, :W_IN_SPLIT] = x[:, :W_IN_SPLIT].astype(BF16)
    o_ref[:, W_IN_SPLIT:W_IN_ROPE_AT] = jnp.zeros((rows, W_IN_ROPE_AT - W_IN_SPLIT), BF16)
    o_ref[:, W_IN_ROPE_AT:W_IN_ROPE_AT + MLA_ROPE] = (
        x[:, W_IN_SPLIT:W_IN_SPLIT + MLA_ROPE].astype(BF16))
    o_ref[:, W_IN_ROPE_AT + MLA_ROPE:W_IN_REST_AT] = (
        jnp.zeros((rows, W_IN_REST_AT - W_IN_ROPE_AT - MLA_ROPE), BF16))
    o_ref[:, W_IN_REST_AT:] = x[:, W_IN_SPLIT + MLA_ROPE:].astype(BF16)


def _pad_w_in(w_in, *, rows=256):
    lyr, d, n = w_in.shape
    return pl.pallas_call(
        _pad_w_in_kernel,
        grid=(lyr, d // rows),
        in_specs=[pl.BlockSpec((None, rows, n), lambda l, i: (l, i, 0))],
        out_specs=pl.BlockSpec((None, rows, IN_PAD), lambda l, i: (l, i, 0)),
        out_shape=jax.ShapeDtypeStruct((lyr, d, IN_PAD), BF16),
        compiler_params=_cparams("parallel", "parallel"),
        name="pad_w_in",
    )(w_in)


def _pad_wt_uq(w_uq):
    lyr = w_uq.shape[0]
    w = w_uq.reshape(lyr, MLA_Q_LORA, MLA_HEADS, MLA_NOPE + MLA_ROPE)
    w = jnp.pad(w, ((0, 0), (0, 0), (0, 0), (0, MLA_QK_PAD - MLA_NOPE - MLA_ROPE)))
    w = w.reshape(lyr, MLA_Q_LORA, MLA_HEADS * MLA_QK_PAD)
    return w.transpose(0, 2, 1).astype(BF16)


def _split_w_ukv(w_ukv):
    lyr = w_ukv.shape[0]
    w = w_ukv.reshape(lyr, MLA_KV_LORA, MLA_HEADS, MLA_NOPE + MLA_V)
    wk = jnp.pad(w[..., :MLA_NOPE], ((0, 0), (0, 0), (0, 0), (0, MLA_QK_PAD - MLA_NOPE)))
    wk = wk.reshape(lyr, MLA_KV_LORA, -1).astype(BF16)
    wv = w[..., MLA_NOPE:].reshape(lyr, MLA_KV_LORA, -1)
    return wk, wv.transpose(0, 2, 1).astype(BF16)


def _rope_angles():
    inv_freq = ROPE_THETA ** (-jnp.arange(ROPE_HALF, dtype=F32) / ROPE_HALF)
    ang = jnp.arange(SEQ, dtype=F32)[:, None] * inv_freq[None, :]
    return jnp.cos(ang), jnp.sin(ang)


def _rope_tables_k():
    cos, sin = _rope_angles()
    one = jnp.ones((SEQ, MLA_NOPE), F32)
    z = lambda n: jnp.zeros((SEQ, n), F32)
    tail = MLA_QK_PAD - MLA_NOPE - MLA_ROPE
    cos_t = jnp.concatenate([one, cos, cos, z(tail)], axis=1)
    sin_a = jnp.concatenate([z(MLA_NOPE), -sin, z(ROPE_HALF), z(tail)], axis=1)
    sin_b = jnp.concatenate([z(MLA_NOPE), z(ROPE_HALF), sin, z(tail)], axis=1)
    return cos_t, sin_a, sin_b


def kernel(x, g_mix, w_in, g_q, w_uq, g_kv, w_ukv, a_re, a_im, b_re, b_im, c_re, c_im,
           d_skip, log_dt, w_glu, b_glu, g_out_mla, g_out_ssm, g_out_dil, w_o,
           g_ffn, w_gate, w_up, w_down, g_final):
    x = x.reshape(TOKENS, D_MODEL)
    w_in_p = _pad_w_in(w_in)
    wt_q = _pad_wt_uq(w_uq)
    w_k, wt_v = _split_w_ukv(w_ukv)
    w_o_b = w_o.astype(BF16)
    q_scale = (MLA_NOPE + MLA_ROPE) ** -0.5 * math.log2(math.e)
    cos, sin = _rope_angles()
    q_cos_t, q_sin_t = (cos * q_scale).T, (sin * q_scale).T
    k_tabs = _rope_tables_k()
    ssm_mats = jax.vmap(_ssm_matrices)(a_re, a_im, b_re, b_im, c_re, c_im, d_skip, log_dt)

    for l in range(DEPTH):
        proj = _norm_matmul(x, g_mix, w_in_p, l, bm=512, bn=IN_PAD, out_dtype=F32,
                            name="in_proj")
        qt = _q_up(proj, g_q, wt_q, q_cos_t, q_sin_t, l, q_scale, bm=1024)
        k, vt = _kv_up(proj, g_kv, w_k, wt_v, k_tabs, l, bm=1024)
        y_mla = _mla_attention(qt, k, vt)
        y_ssm = _glu(_ssm_scan(proj, ssm_mats, l), w_glu, b_glu, l, bm=1024)
        y_dil = _dil_attention(proj)
        x = _out_proj(y_mla, y_ssm, y_dil, g_out_mla, g_out_ssm, g_out_dil,
                      w_o_b, x, l, bm=512, bn=D_MODEL)
        act, w_down_b = _ffn_up(x, g_ffn, w_gate, w_up, w_down, l, bm=1024, bn=512)
        x = _ffn_down(act, w_down_b, x, bm=1024, bn=512)
    out = _final_norm(x, g_final, bm=512)
    return out.reshape(BATCH, SEQ, D_MODEL)
```

```python
import functools
import math

import jax
import jax.numpy as jnp
import numpy as np
from jax import lax
from jax.experimental import pallas as pl
from jax.experimental.pallas import tpu as pltpu

F32 = jnp.float32
BF16 = jnp.bfloat16

D_MODEL = 2048
BATCH = 4
SEQ = 2048
DEPTH = 4
TOKENS = BATCH * SEQ

MLA_HEADS = 8
MLA_NOPE = 128
MLA_ROPE = 64
MLA_V = 128
MLA_Q_LORA = 512
MLA_KV_LORA = 256
MLA_WIDTH = MLA_HEADS * MLA_V
MLA_QK_PAD = 256
ROPE_THETA = 10000.0
ROPE_HALF = MLA_ROPE // 2

SSM_WIDTH = 512
SSM_GROUP = 16
SSM_GROUPS = 32
SSM_STATE = 64
SSM_CHUNK = 8

DIL_WIDTH = 512
DIL_HEAD_DIM = 64
DIL_HEADS = 8
DIL_PATTERNS = ((128, 1), (512, 4), (2048, 16))
BLOCK = 128
DIL_GROUP = 1

IN_PAD = 3072
LATENT_PAD = 1024
MIX_COLS = IN_PAD - LATENT_PAD
SSM_U_COL = 0
DIL_Q_COL = 512
DIL_K_COL = 1024
DIL_V_COL = 1536
D_FF = 5632
NORM_EPS = 1e-6

LANES = 128
VMEM_LIMIT_BYTES = 56 * 1024 * 1024

NT_DIMS = (((1,), (1,)), ((), ()))


def _cparams(*semantics):
    return pltpu.CompilerParams(dimension_semantics=semantics,
                                vmem_limit_bytes=VMEM_LIMIT_BYTES)


def _rms_rows(x, g):
    ms = jnp.mean(x * x, axis=-1, keepdims=True)
    return x * lax.rsqrt(ms + NORM_EPS) * g


NORM_CHUNK = 256


def _first_step_by_chunks(h_ref, normed_chunk, make_emit):
    @pl.when(pl.program_id(1) == 0)
    def _():
        emit = make_emit()
        for r in range(0, h_ref.shape[0], NORM_CHUNK):
            rows = slice(r, r + NORM_CHUNK)
            h = normed_chunk(rows)
            h_ref[rows, :] = h
            emit(rows, h)

    @pl.when(pl.program_id(1) != 0)
    def _():
        make_emit()(slice(None), h_ref[...])


def _bf16(w):
    return w if w.dtype == BF16 else w.astype(BF16)


def _layer_spec(shape, index_map):
    return pl.BlockSpec((None,) + tuple(shape), index_map)


def _in_proj_kernel(x_ref, g_ref, w_ref, lat_ref, mix_ref, h_ref):
    g = g_ref[...]

    def make_emit():
        w = w_ref[...]

        def emit(rows, h):
            res = jnp.dot(h, w, preferred_element_type=F32)
            lat_ref[rows, :] = res[:, :LATENT_PAD].astype(lat_ref.dtype)
            mix_ref[rows, :] = res[:, LATENT_PAD:]
        return emit

    _first_step_by_chunks(h_ref, lambda rows: _rms_rows(x_ref[rows, :], g).astype(BF16),
                          make_emit)


def _in_proj(x, g, w, layer, *, bm):
    return pl.pallas_call(
        _in_proj_kernel,
        grid=(TOKENS // bm, 1),
        in_specs=[pl.BlockSpec((bm, D_MODEL), lambda i, j: (i, 0)),
                  _layer_spec((1, D_MODEL), lambda i, j: (layer, 0, 0)),
                  _layer_spec((D_MODEL, IN_PAD), lambda i, j: (layer, 0, 0))],
        out_specs=[pl.BlockSpec((bm, LATENT_PAD), lambda i, j: (i, 0)),
                   pl.BlockSpec((bm, MIX_COLS), lambda i, j: (i, 0))],
        out_shape=[jax.ShapeDtypeStruct((TOKENS, LATENT_PAD), BF16),
                   jax.ShapeDtypeStruct((TOKENS, MIX_COLS), F32)],
        scratch_shapes=[pltpu.VMEM((bm, D_MODEL), BF16)],
        compiler_params=_cparams("parallel", "arbitrary"),
        name="in_proj",
    )(x, g.reshape(DEPTH, 1, D_MODEL), w)


def _q_up_kernel(c_ref, g_ref, wt_ref, cos_ref, sin_ref, o_ref, *, scale):
    h = _rms_rows(c_ref[...].astype(F32), g_ref[...]).astype(BF16)
    cos, sin = cos_ref[...], sin_ref[...]

    def project(hd):
        rows = slice(hd * MLA_QK_PAD, (hd + 1) * MLA_QK_PAD)
        return lax.dot_general(wt_ref[rows, :], h, NT_DIMS, preferred_element_type=F32)

    def finish(hd, qt):
        r0 = hd * MLA_QK_PAD
        n1, n2, n3 = MLA_NOPE, MLA_NOPE + ROPE_HALF, MLA_NOPE + MLA_ROPE
        x1, x2 = qt[n1:n2], qt[n2:n3]
        o_ref[r0:r0 + n1, :] = (qt[:n1] * scale).astype(o_ref.dtype)
        o_ref[r0 + n1:r0 + n2, :] = (x1 * cos - x2 * sin).astype(o_ref.dtype)
        o_ref[r0 + n2:r0 + n3, :] = (x2 * cos + x1 * sin).astype(o_ref.dtype)
        o_ref[r0 + n3:r0 + MLA_QK_PAD, :] = qt[n3:].astype(o_ref.dtype)

    qt_next = project(0)
    for hd in range(MLA_HEADS):
        qt = qt_next
        if hd + 1 < MLA_HEADS:
            qt_next = project(hd + 1)
        finish(hd, qt)


def _q_up(proj, g_q, wt_q, cos_t, sin_t, layer, scale, *, bm):
    pos_blocks = SEQ // bm
    tab_spec = pl.BlockSpec((ROPE_HALF, bm), lambda i: (0, i % pos_blocks))
    n = MLA_HEADS * MLA_QK_PAD
    return pl.pallas_call(
        functools.partial(_q_up_kernel, scale=scale),
        grid=(TOKENS // bm,),
        in_specs=[pl.BlockSpec((bm, MLA_Q_LORA), lambda i: (i, 0)),
                  _layer_spec((1, MLA_Q_LORA), lambda i: (layer, 0, 0)),
                  _layer_spec((n, MLA_Q_LORA), lambda i: (layer, 0, 0)),
                  tab_spec, tab_spec],
        out_specs=pl.BlockSpec((n, bm), lambda i: (0, i)),
        out_shape=jax.ShapeDtypeStruct((n, TOKENS), BF16),
        compiler_params=_cparams("parallel"),
        name="mla_q_up",
    )(proj, g_q.reshape(DEPTH, 1, -1), wt_q, cos_t, sin_t)


def _rope_pad(x, cos_t, sin_a, sin_b):
    up = pltpu.roll(x, MLA_QK_PAD - ROPE_HALF, axis=1)
    dn = pltpu.roll(x, ROPE_HALF, axis=1)
    return x * cos_t + up * sin_a + dn * sin_b


def _kv_up_kernel(c_ref, kr_ref, g_ref, wk_ref, wvt_ref, cos_ref, sina_ref, sinb_ref,
                  k_ref, vt_ref):
    h = _rms_rows(c_ref[...].astype(F32), g_ref[...]).astype(BF16)
    kn = jnp.dot(h, wk_ref[...], preferred_element_type=F32)
    vt = lax.dot_general(wvt_ref[...], h, NT_DIMS, preferred_element_type=F32)
    k_pe = _rope_pad(kr_ref[...].astype(F32), cos_ref[...], sina_ref[...], sinb_ref[...])
    for hd in range(MLA_HEADS):
        sl = slice(hd * MLA_QK_PAD, (hd + 1) * MLA_QK_PAD)
        k_ref[:, sl] = (kn[:, sl] + k_pe).astype(k_ref.dtype)
    vt_ref[...] = vt.astype(vt_ref.dtype)


def _kv_up(proj, g_kv, w_k, wt_v, tabs, layer, *, bm):
    pos_blocks = SEQ // bm
    tab_spec = pl.BlockSpec((bm, MLA_QK_PAD), lambda i: (i % pos_blocks, 0))
    nk = MLA_HEADS * MLA_QK_PAD
    return pl.pallas_call(
        _kv_up_kernel,
        grid=(TOKENS // bm,),
        in_specs=[pl.BlockSpec((bm, MLA_KV_LORA), lambda i: (i, 2)),
                  pl.BlockSpec((bm, MLA_QK_PAD), lambda i: (i, 3)),
                  _layer_spec((1, MLA_KV_LORA), lambda i: (layer, 0, 0)),
                  _layer_spec((MLA_KV_LORA, nk), lambda i: (layer, 0, 0)),
                  _layer_spec((MLA_WIDTH, MLA_KV_LORA), lambda i: (layer, 0, 0)),
                  tab_spec, tab_spec, tab_spec],
        out_specs=[pl.BlockSpec((bm, nk), lambda i: (i, 0)),
                   pl.BlockSpec((MLA_WIDTH, bm), lambda i: (0, i))],
        out_shape=[jax.ShapeDtypeStruct((TOKENS, nk), BF16),
                   jax.ShapeDtypeStruct((MLA_WIDTH, TOKENS), BF16)],
        compiler_params=_cparams("parallel"),
        name="mla_kv_up",
    )(proj, proj, g_kv.reshape(DEPTH, 1, -1), w_k, wt_v, *tabs)


ATT_BQ = 256
ATT_BK = 256
ATT_NQ = SEQ // ATT_BQ
ATT_ONES = 16
ATT_GROUP = 3


def _mla_attn_kernel(qt_ref, k_ref, vt_ref, o_ref, m_sc, acc_sc):
    key = lax.broadcasted_iota(jnp.int32, (ATT_BK, ATT_BQ), 0)
    qry = lax.broadcasted_iota(jnp.int32, (ATT_BK, ATT_BQ), 1)
    causal = key <= qry
    tiles = [(i, j) for j in range(ATT_NQ) for i in range(j, ATT_NQ)]

    def scores(i, j):
        kj = k_ref[j * ATT_BK:(j + 1) * ATT_BK, :]
        qi = qt_ref[:, i * ATT_BQ:(i + 1) * ATT_BQ]
        return jnp.dot(kj, qi, preferred_element_type=F32)

    def softmax(i, j, s):
        if i == j:
            s = jnp.where(causal, s, -jnp.inf)
        m_blk = jnp.max(s, axis=0, keepdims=True)
        if j == 0:
            m_new, alpha = m_blk, None
        else:
            m_old = m_sc[i]
            m_new = jnp.maximum(m_old, m_blk)
            alpha = jnp.exp2(m_old - m_new)
        if i != j:
            m_sc[i] = m_new
        return jnp.exp2(s - m_new).astype(BF16), alpha

    ones_rows = jnp.ones((ATT_ONES, ATT_BK), BF16)

    def values(i, j, p, alpha):
        vj = jnp.concatenate([vt_ref[:, j * ATT_BK:(j + 1) * ATT_BK], ones_rows], axis=0)
        acc = jnp.dot(vj, p, preferred_element_type=F32)
        if alpha is not None:
            acc = alpha * acc_sc[i] + acc
        if i == j:
            out = acc[:MLA_V] / acc[MLA_V:MLA_V + 1]
            o_ref[i * ATT_BQ:(i + 1) * ATT_BQ, :] = out.T.astype(o_ref.dtype)
        else:
            acc_sc[i] = acc

    groups = [tiles[g:g + ATT_GROUP] for g in range(0, len(tiles), ATT_GROUP)]
    n = len(groups)

    def stage_scores(g):
        return [scores(*tile) for tile in groups[g]]

    def stage_softmax(g, s_list):
        return [softmax(*tile, s) for tile, s in zip(groups[g], s_list)]

    def stage_values(g, p_list):
        for tile, p_alpha in zip(groups[g], p_list):
            values(*tile, *p_alpha)

    s_ready = {g: stage_scores(g) for g in range(min(2, n))}
    p_ready = {0: stage_softmax(0, s_ready.pop(0))}
    for g in range(n):
        if g + 2 < n:
            s_ready[g + 2] = stage_scores(g + 2)
        if g + 1 < n:
            p_ready[g + 1] = stage_softmax(g + 1, s_ready.pop(g + 1))
        stage_values(g, p_ready.pop(g))


def _mla_attention(qt, k, vt):
    return pl.pallas_call(
        _mla_attn_kernel,
        grid=(BATCH, MLA_HEADS),
        in_specs=[pl.BlockSpec((MLA_QK_PAD, SEQ), lambda b, h: (h, b)),
                  pl.BlockSpec((SEQ, MLA_QK_PAD), lambda b, h: (b, h)),
                  pl.BlockSpec((MLA_V, SEQ), lambda b, h: (h, b))],
        out_specs=pl.BlockSpec((SEQ, MLA_V), lambda b, h: (b, h)),
        out_shape=jax.ShapeDtypeStruct((TOKENS, MLA_WIDTH), BF16),
        scratch_shapes=[pltpu.VMEM((ATT_NQ, 1, ATT_BQ), F32),
                        pltpu.VMEM((ATT_NQ, MLA_V + ATT_ONES, ATT_BQ), F32)],
        compiler_params=_cparams("parallel", "parallel"),
        name="mla_attention",
    )(qt, k, vt)


def _dil_attn_kernel(q_ref, k_ref, v_ref, o_ref, m_sc, l_sc, n_sc):
    row2 = lax.broadcasted_iota(jnp.int32, (BLOCK, 2 * BLOCK), 0)
    col2 = lax.broadcasted_iota(jnp.int32, (BLOCK, 2 * BLOCK), 1)
    dist = row2 + BLOCK - col2
    band = (dist >= 0) & (dist <= BLOCK)
    row1 = lax.broadcasted_iota(jnp.int32, (BLOCK, BLOCK), 0)
    col1 = lax.broadcasted_iota(jnp.int32, (BLOCK, BLOCK), 1)
    tri = row1 >= col1
    q_scale = DIL_HEAD_DIM ** -0.5 * math.log2(math.e)
    heads = LANES // DIL_HEAD_DIM
    n_patterns = len(DIL_PATTERNS)

    def rows_at(start, dil):
        return pl.ds(start, BLOCK) if dil == 1 else pl.ds(start, BLOCK, stride=dil)

    blocks = [(pi, dil, r, n) for pi, (_, dil) in enumerate(reversed(DIL_PATTERNS))
              for r in range(dil) for n in range(SEQ // dil // BLOCK)]

    assert heads == 2
    head0 = lax.broadcasted_iota(jnp.int32, (BLOCK, LANES), 1) < DIL_HEAD_DIM

    def load(pi, dil, r, n):
        rows = rows_at(r + dil * BLOCK * n, dil)
        q = q_ref[rows, :] * q_scale
        if n == 0:
            k, v, mask = k_ref[rows, :], v_ref[rows, :], tri
        else:
            prev = rows_at(r + dil * BLOCK * (n - 1), dil)
            k = jnp.concatenate([k_ref[prev, :], k_ref[rows, :]], axis=0)
            v = jnp.concatenate([v_ref[prev, :], v_ref[rows, :]], axis=0)
            mask = band
        v_head0 = lax.broadcasted_iota(jnp.int32, v.shape, 1) < DIL_HEAD_DIM
        q_heads = (jnp.where(head0, q, 0.0).astype(BF16), jnp.where(head0, 0.0, q).astype(BF16))
        v_heads = (jnp.where(v_head0, v, 1.0).astype(BF16),
                   jnp.where(v_head0, 1.0, v).astype(BF16))
        return rows, q_heads, k.astype(BF16), v_heads, mask

    def scores(blk, hd):
        return lax.dot_general(blk[1][hd], blk[2], NT_DIMS, preferred_element_type=F32)

    def softmax(blk, s):
        s = jnp.where(blk[4], s, -jnp.inf)
        m = jnp.max(s, axis=-1, keepdims=True)
        return jnp.exp2(s - m).astype(BF16), m

    def values(blk, hd, p):
        return jnp.dot(p, blk[3][hd], preferred_element_type=F32)

    def merge(pi, rows, parts):
        (m_a, ext_a), (m_b, ext_b) = parts
        m2 = jnp.where(head0, m_a, m_b)
        a2 = jnp.where(head0, ext_a, ext_b)
        l2 = pltpu.roll(jnp.where(head0, ext_b, ext_a), DIL_HEAD_DIM, axis=1)
        if pi > 0:
            m_old = m_sc[rows, :]
            m_new = jnp.maximum(m_old, m2)
            w_old, w_new = jnp.exp2(m_old - m_new), jnp.exp2(m2 - m_new)
            l2 = w_old * l_sc[rows, :] + w_new * l2
            a2 = w_old * n_sc[rows, :] + w_new * a2
            m2 = m_new
        if pi == n_patterns - 1:
            o_ref[rows, :] = a2 / l2
        else:
            m_sc[rows, :] = m2
            l_sc[rows, :] = l2
            n_sc[rows, :] = a2

    groups = [blocks[g:g + DIL_GROUP] for g in range(0, len(blocks), DIL_GROUP)]
    n = len(groups)

    def stage_scores(g):
        out = []
        for spec in groups[g]:
            blk = load(*spec)
            out.append((blk, [scores(blk, hd) for hd in range(heads)]))
        return out

    def stage_softmax(scored):
        return [(blk, [softmax(blk, s) for s in s_list]) for blk, s_list in scored]

    def stage_values(g, probs):
        for spec, (blk, p_list) in zip(groups[g], probs):
            merge(spec[0], blk[0],
                  [(m_rows, values(blk, hd, p)) for hd, (p, m_rows) in enumerate(p_list)])

    s_ready = {g: stage_scores(g) for g in range(min(2, n))}
    p_ready = {0: stage_softmax(s_ready.pop(0))}
    for g in range(n):
        if g + 2 < n:
            s_ready[g + 2] = stage_scores(g + 2)
        if g + 1 < n:
            p_ready[g + 1] = stage_softmax(s_ready.pop(g + 1))
        stage_values(g, p_ready.pop(g))


def _dil_attention(proj):
    def spec(col0):
        return pl.BlockSpec((SEQ, LANES), lambda b, hp: (b, col0 // LANES + hp))

    return pl.pallas_call(
        _dil_attn_kernel,
        grid=(BATCH, DIL_WIDTH // LANES),
        in_specs=[spec(DIL_Q_COL), spec(DIL_K_COL), spec(DIL_V_COL)],
        out_specs=pl.BlockSpec((SEQ, LANES), lambda b, hp: (b, hp)),
        out_shape=jax.ShapeDtypeStruct((TOKENS, DIL_WIDTH), F32),
        scratch_shapes=[pltpu.VMEM((SEQ, LANES), F32)] * 3,
        compiler_params=_cparams("parallel", "parallel"),
        name="dilated_attention",
    )(proj, proj, proj)


SLAB_GROUPS = LANES // SSM_GROUP
SSM_SLABS = SSM_GROUPS // SLAB_GROUPS
SSM_NCHUNK = SEQ // SSM_CHUNK
SSM_ROWS = BATCH * SSM_NCHUNK
SSM_ROW = SSM_CHUNK * LANES
SLAB_STATE = SLAB_GROUPS * SSM_STATE
STATE_TILES = SLAB_STATE // LANES
STATE_OPS = 4
SUBLANES = 8


def _expand_block_diag(x, width):
    rows, c = x.shape
    src = lax.broadcasted_iota(jnp.int32, (c, width), 0)
    dst = lax.broadcasted_iota(jnp.int32, (c, width), 1)
    repeat = jnp.where(dst % c == src, 1.0, 0.0).astype(BF16)
    wide = jnp.dot(x.astype(BF16), repeat, preferred_element_type=F32)
    row_g = lax.broadcasted_iota(jnp.int32, (rows, width), 0) // (rows // SLAB_GROUPS)
    col_g = lax.broadcasted_iota(jnp.int32, (rows, width), 1) // c
    return jnp.where(row_g == col_g, wide, 0.0).astype(BF16)


def _ssm_kernel(u_ref, dlag_ref, m1re_ref, m1im_ref, m2re_ref, m2im_ref,
                are_ref, aim_ref, d_ref, y_ref, fold_ref, st_ref, t0_sc, op_sc):
    lag_blocks = [_expand_block_diag(dlag_ref[0, tau], LANES) for tau in range(SSM_CHUNK)]
    no_block = jnp.zeros((LANES, LANES), BF16)
    for t in range(SSM_CHUNK):
        for s in range(SSM_CHUNK):
            t0_sc[t * LANES:(t + 1) * LANES, s * LANES:(s + 1) * LANES] = (
                lag_blocks[s - t] if s >= t else no_block)
    for which, ref in enumerate((m1re_ref, m1im_ref, m2re_ref, m2im_ref)):
        for t in range(SSM_CHUNK):
            op_sc[which, t * LANES:(t + 1) * LANES, :] = _expand_block_diag(ref[0, t], SLAB_STATE)

    for b in range(BATCH):
        rows = slice(b * SSM_NCHUNK, (b + 1) * SSM_NCHUNK)
        for t in range(SSM_CHUNK):
            fold_ref[rows, t * LANES:(t + 1) * LANES] = (
                u_ref[pl.ds(b * SEQ + t, SSM_NCHUNK, stride=SSM_CHUNK), :])
    u = fold_ref[...]
    ub = u.astype(BF16)

    loc_re = jnp.dot(ub, op_sc[0], preferred_element_type=F32)
    loc_im = jnp.dot(ub, op_sc[1], preferred_element_type=F32)
    fold_ref[...] = jnp.dot(ub, t0_sc[...], preferred_element_type=F32) + u * d_ref[0]
    for b in range(BATCH):
        rows = slice(b * SSM_NCHUNK, (b + 1) * SSM_NCHUNK)
        for k in range(STATE_TILES):
            lanes = slice(k * LANES, (k + 1) * LANES)
            st_ref[k, pl.ds(b, SSM_NCHUNK, stride=SUBLANES), :] = loc_re[rows, lanes]
            st_ref[k, pl.ds(BATCH + b, SSM_NCHUNK, stride=SUBLANES), :] = loc_im[rows, lanes]

    a_re, a_im = are_ref[0], aim_ref[0]
    upper = lax.broadcasted_iota(jnp.int32, (SUBLANES, LANES), 0) < BATCH
    mul_same, mul_swap = [], []
    for k in range(STATE_TILES):
        lanes = slice(k * LANES, (k + 1) * LANES)
        mul_same.append(jnp.broadcast_to(a_re[:, lanes], (SUBLANES, LANES)))
        im = jnp.broadcast_to(a_im[:, lanes], (SUBLANES, LANES))
        mul_swap.append(jnp.where(upper, -im, im))

    def chunk_step(c, state):
        r0 = pl.multiple_of(c * SUBLANES, SUBLANES)
        new = []
        for k in range(STATE_TILES):
            loc = st_ref[k, pl.ds(r0, SUBLANES), :]
            st_ref[k, pl.ds(r0, SUBLANES), :] = state[k]
            swapped = pltpu.roll(state[k], BATCH, axis=0)
            new.append(mul_same[k] * state[k] + mul_swap[k] * swapped + loc)
        return tuple(new)

    zero = jnp.zeros((SUBLANES, LANES), F32)
    lax.fori_loop(0, SSM_NCHUNK, chunk_step, (zero,) * STATE_TILES, unroll=4)

    def entering(offset):
        return jnp.concatenate(
            [jnp.concatenate([st_ref[k, pl.ds(offset + b, SSM_NCHUNK, stride=SUBLANES), :]
                              for k in range(STATE_TILES)], axis=1)
             for b in range(BATCH)], axis=0)

    y = fold_ref[...]
    for offset, which in ((0, 2), (BATCH, 3)):
        m2t = op_sc[which]
        y = y + lax.dot_general(entering(offset).astype(BF16), m2t, NT_DIMS,
                                preferred_element_type=F32)
    fold_ref[...] = jax.nn.gelu(y, approximate=True)

    for b in range(BATCH):
        rows = slice(b * SSM_NCHUNK, (b + 1) * SSM_NCHUNK)
        for t in range(SSM_CHUNK):
            y_ref[pl.ds(b * SEQ + t, SSM_NCHUNK, stride=SSM_CHUNK), :] = (
                fold_ref[rows, t * LANES:(t + 1) * LANES])


def _ssm_scan(proj, mats, layer):
    dlag, m1re, m1im, m2re, m2im, a_re, a_im, dvec = mats

    def spec(r, c):
        return _layer_spec((1, r, c), lambda s: (layer, s, 0, 0))

    def compact(c):
        return _layer_spec((1, SSM_CHUNK, LANES, c), lambda s: (layer, s, 0, 0, 0))

    return pl.pallas_call(
        _ssm_kernel,
        grid=(SSM_SLABS,),
        in_specs=[pl.BlockSpec((TOKENS, LANES), lambda s: (0, SSM_U_COL // LANES + s)),
                  compact(SSM_GROUP),
                  compact(SSM_STATE), compact(SSM_STATE),
                  compact(SSM_STATE), compact(SSM_STATE),
                  spec(1, SLAB_STATE), spec(1, SLAB_STATE), spec(1, SSM_ROW)],
        out_specs=pl.BlockSpec((TOKENS, LANES), lambda s: (0, s)),
        out_shape=jax.ShapeDtypeStruct((TOKENS, SSM_WIDTH), F32),
        scratch_shapes=[pltpu.VMEM((SSM_ROWS, SSM_ROW), F32),
                        pltpu.VMEM((STATE_TILES, SUBLANES * SSM_NCHUNK, LANES), F32),
                        pltpu.VMEM((SSM_ROW, SSM_ROW), BF16),
                        pltpu.VMEM((STATE_OPS, SSM_ROW, SLAB_STATE), BF16)],
        compiler_params=_cparams("arbitrary"),
        name="ssm_chunk_scan",
    )(proj, dlag, m1re, m1im, m2re, m2im, a_re, a_im, dvec)


def _ssm_matrices(a_re, a_im, b_re, b_im, c_re, c_im, d_skip, log_dt):
    lam_re = jnp.minimum(a_re, -1e-4)
    lam_im = a_im
    dt = jnp.exp(log_dt)[:, None]
    mag = jnp.exp(lam_re * dt)
    ab_re, ab_im = mag * jnp.cos(lam_im * dt), mag * jnp.sin(lam_im * dt)
    n_re, n_im = ab_re - 1.0, ab_im
    den = lam_re * lam_re + lam_im * lam_im
    f_re = (n_re * lam_re + n_im * lam_im) / den
    f_im = (n_im * lam_re - n_re * lam_im) / den
    bb_re = f_re[..., None] * b_re - f_im[..., None] * b_im
    bb_im = f_re[..., None] * b_im + f_im[..., None] * b_re
    p_re, p_im = [jnp.ones_like(ab_re)], [jnp.zeros_like(ab_im)]
    for _ in range(SSM_CHUNK):
        p_re.append(p_re[-1] * ab_re - p_im[-1] * ab_im)
        p_im.append(p_re[-2] * ab_im + p_im[-1] * ab_re)
    pw_re, pw_im = jnp.stack(p_re), jnp.stack(p_im)

    cb_re = (c_re[:, None, :, :] * bb_re.transpose(0, 2, 1)[:, :, None, :]
             - c_im[:, None, :, :] * bb_im.transpose(0, 2, 1)[:, :, None, :])
    cb_im = (c_re[:, None, :, :] * bb_im.transpose(0, 2, 1)[:, :, None, :]
             + c_im[:, None, :, :] * bb_re.transpose(0, 2, 1)[:, :, None, :])
    lag_re = pw_re[:SSM_CHUNK].transpose(1, 0, 2)[:, :, None, None, :]
    lag_im = pw_im[:SSM_CHUNK].transpose(1, 0, 2)[:, :, None, None, :]
    kern = jnp.sum(lag_re * cb_re[:, None] - lag_im * cb_im[:, None], axis=-1)
    t_idx = np.arange(SSM_CHUNK)

    back = pw_re[SSM_CHUNK - 1 - t_idx], pw_im[SSM_CHUNK - 1 - t_idx]
    bre_t, bim_t = bb_re.transpose(0, 2, 1), bb_im.transpose(0, 2, 1)
    m1re = (back[0].transpose(1, 0, 2)[:, :, None, :] * bre_t[:, None]
            - back[1].transpose(1, 0, 2)[:, :, None, :] * bim_t[:, None])
    m1im = (back[0].transpose(1, 0, 2)[:, :, None, :] * bim_t[:, None]
            + back[1].transpose(1, 0, 2)[:, :, None, :] * bre_t[:, None])

    fwd_re = pw_re[1:].transpose(1, 0, 2)[:, :, None, :]
    fwd_im = pw_im[1:].transpose(1, 0, 2)[:, :, None, :]
    ca_re = c_re[:, None] * fwd_re - c_im[:, None] * fwd_im
    ca_im = c_re[:, None] * fwd_im + c_im[:, None] * fwd_re

    def rows_by_group(x):
        k, r, c = x.shape[1:]
        x = x.reshape(SSM_SLABS, SLAB_GROUPS, k, r, c).transpose(0, 2, 1, 3, 4)
        return x.reshape(SSM_SLABS, k, SLAB_GROUPS * r, c)

    a_step_re = pw_re[SSM_CHUNK].reshape(SSM_SLABS, 1, SLAB_STATE)
    a_step_im = pw_im[SSM_CHUNK].reshape(SSM_SLABS, 1, SLAB_STATE)
    dvec = jnp.concatenate([d_skip.reshape(SSM_SLABS, 1, LANES)] * SSM_CHUNK, axis=-1)
    return (rows_by_group(kern), rows_by_group(m1re), rows_by_group(m1im),
            rows_by_group(ca_re), rows_by_group(-ca_im), a_step_re, a_step_im, dvec)


def _glu_kernel(y_ref, w_ref, b_ref, o_ref):
    w, bias = _bf16(w_ref[...]), b_ref[...]
    chunks = [slice(r, r + NORM_CHUNK) for r in range(0, y_ref.shape[0], NORM_CHUNK)]

    def project(rows):
        return jnp.dot(y_ref[rows, :].astype(BF16), w, preferred_element_type=F32)

    z_next = project(chunks[0])
    for c, rows in enumerate(chunks):
        z = z_next + bias
        if c + 1 < len(chunks):
            z_next = project(chunks[c + 1])
        o_ref[rows, :] = (z[:, :SSM_WIDTH] * jax.nn.sigmoid(z[:, SSM_WIDTH:])).astype(o_ref.dtype)


def _glu(y, w_glu, b_glu, layer, *, bm):
    return pl.pallas_call(
        _glu_kernel,
        grid=(TOKENS // bm,),
        in_specs=[pl.BlockSpec((bm, SSM_WIDTH), lambda i: (i, 0)),
                  _layer_spec((SSM_WIDTH, 2 * SSM_WIDTH), lambda i: (layer, 0, 0)),
                  _layer_spec((1, 2 * SSM_WIDTH), lambda i: (layer, 0, 0))],
        out_specs=pl.BlockSpec((bm, SSM_WIDTH), lambda i: (i, 0)),
        out_shape=jax.ShapeDtypeStruct((TOKENS, SSM_WIDTH), BF16),
        compiler_params=_cparams("parallel"),
        name="ssm_glu",
    )(y, w_glu, b_glu.reshape(DEPTH, 1, -1))


def _out_proj_kernel(ya_ref, yb_ref, yc_ref, ga_ref, gb_ref, gc_ref, w_ref, x_ref,
                     o_ref, h_ref):
    parts = ((ya_ref, ga_ref[...]), (yb_ref, gb_ref[...]), (yc_ref, gc_ref[...]))

    def normed_chunk(rows):
        return jnp.concatenate([_rms_rows(y_ref[rows, :].astype(F32), g).astype(BF16)
                                for y_ref, g in parts], axis=-1)

    def make_emit():
        w = _bf16(w_ref[...])

        def emit(rows, h):
            o_ref[rows, :] = x_ref[rows, :] + jnp.dot(h, w, preferred_element_type=F32)
        return emit

    _first_step_by_chunks(h_ref, normed_chunk, make_emit)


def _out_proj(y_mla, y_ssm, y_dil, g_mla, g_ssm, g_dil, w_o, x, layer, *, bm, bn):
    def rows(width):
        return pl.BlockSpec((bm, width), lambda i, j: (i, 0))

    def gain(width):
        return _layer_spec((1, width), lambda i, j: (layer, 0, 0))

    return pl.pallas_call(
        _out_proj_kernel,
        grid=(TOKENS // bm, D_MODEL // bn),
        in_specs=[rows(MLA_WIDTH), rows(SSM_WIDTH), rows(DIL_WIDTH),
                  gain(MLA_WIDTH), gain(SSM_WIDTH), gain(DIL_WIDTH),
                  _layer_spec((D_MODEL, bn), lambda i, j: (layer, 0, j)),
                  pl.BlockSpec((bm, bn), lambda i, j: (i, j))],
        out_specs=pl.BlockSpec((bm, bn), lambda i, j: (i, j)),
        out_shape=jax.ShapeDtypeStruct((TOKENS, D_MODEL), F32),
        scratch_shapes=[pltpu.VMEM((bm, D_MODEL), BF16)],
        compiler_params=_cparams("parallel", "arbitrary"),
        name="out_proj",
    )(y_mla, y_ssm, y_dil, g_mla.reshape(DEPTH, 1, -1), g_ssm.reshape(DEPTH, 1, -1),
      g_dil.reshape(DEPTH, 1, -1), w_o, x)


def _ffn_up_kernel(x_ref, g_ref, wg_ref, wu_ref, wd_ref, o_ref, wd_bf16_ref, h_ref):
    g = g_ref[...]
    wd_bf16_ref[...] = wd_ref[...].astype(BF16)

    def make_emit():
        wg, wu = _bf16(wg_ref[...]), _bf16(wu_ref[...])

        def emit(rows, h):
            gate = jnp.dot(h, wg, preferred_element_type=F32)
            up = jnp.dot(h, wu, preferred_element_type=F32)
            o_ref[rows, :] = (jax.nn.silu(gate) * up).astype(o_ref.dtype)
        return emit

    _first_step_by_chunks(h_ref, lambda rows: _rms_rows(x_ref[rows, :], g).astype(BF16),
                          make_emit)


def _ffn_up(x, g, w_gate, w_up, w_down, layer, *, bm, bn):
    col_steps = D_FF // bn
    wd_rows = D_FF // (TOKENS // bm * col_steps)
    return pl.pallas_call(
        _ffn_up_kernel,
        grid=(TOKENS // bm, col_steps),
        in_specs=[pl.BlockSpec((bm, D_MODEL), lambda i, j: (i, 0)),
                  _layer_spec((1, D_MODEL), lambda i, j: (layer, 0, 0)),
                  _layer_spec((D_MODEL, bn), lambda i, j: (layer, 0, j)),
                  _layer_spec((D_MODEL, bn), lambda i, j: (layer, 0, j)),
                  _layer_spec((wd_rows, D_MODEL), lambda i, j: (layer, i * col_steps + j, 0))],
        out_specs=[pl.BlockSpec((bm, bn), lambda i, j: (i, j)),
                   pl.BlockSpec((wd_rows, D_MODEL), lambda i, j: (i * col_steps + j, 0))],
        out_shape=[jax.ShapeDtypeStruct((TOKENS, D_FF), BF16),
                   jax.ShapeDtypeStruct((D_FF, D_MODEL), BF16)],
        scratch_shapes=[pltpu.VMEM((bm, D_MODEL), BF16)],
        compiler_params=_cparams("arbitrary", "arbitrary"),
        name="ffn_gate_up",
    )(x, g.reshape(DEPTH, 1, -1), w_gate, w_up, w_down)


def _ffn_down_kernel(a_ref, w_ref, x_ref, o_ref):
    o_ref[...] = x_ref[...] + jnp.dot(a_ref[...], _bf16(w_ref[...]),
                                      preferred_element_type=F32)


def _ffn_down(a, w_down, x, *, bm, bn):
    return pl.pallas_call(
        _ffn_down_kernel,
        grid=(TOKENS // bm, D_MODEL // bn),
        in_specs=[pl.BlockSpec((bm, D_FF), lambda i, j: (i, 0)),
                  pl.BlockSpec((D_FF, bn), lambda i, j: (0, j)),
                  pl.BlockSpec((bm, bn), lambda i, j: (i, j))],
        out_specs=pl.BlockSpec((bm, bn), lambda i, j: (i, j)),
        out_shape=jax.ShapeDtypeStruct((TOKENS, D_MODEL), F32),
        compiler_params=_cparams("parallel", "arbitrary"),
        name="ffn_down",
    )(a, w_down, x)


def _final_norm_kernel(x_ref, g_ref, o_ref):
    o_ref[...] = _rms_rows(x_ref[...], g_ref[...])


def _final_norm(x, g, *, bm):
    return pl.pallas_call(
        _final_norm_kernel,
        grid=(TOKENS // bm,),
        in_specs=[pl.BlockSpec((bm, D_MODEL), lambda i: (i, 0)),
                  pl.BlockSpec((1, D_MODEL), lambda i: (0, 0))],
        out_specs=pl.BlockSpec((bm, D_MODEL), lambda i: (i, 0)),
        out_shape=jax.ShapeDtypeStruct((TOKENS, D_MODEL), F32),
        compiler_params=_cparams("parallel"),
        name="final_norm",
    )(x, g.reshape(1, -1))


W_IN_SPLIT = 768
W_IN_ROPE_AT = 896
W_IN_REST_AT = 1024


def _pad_w_in_kernel(w_ref, o_ref):
    x = w_ref[...]
    rows = x.shape[0]
    o_ref[:, :W_IN_SPLIT] = x[:, :W_IN_SPLIT].astype(BF16)
    o_ref[:, W_IN_SPLIT:W_IN_ROPE_AT] = jnp.zeros((rows, W_IN_ROPE_AT - W_IN_SPLIT), BF16)
    o_ref[:, W_IN_ROPE_AT:W_IN_ROPE_AT + MLA_ROPE] = (
        x[:, W_IN_SPLIT:W_IN_SPLIT + MLA_ROPE].astype(BF16))
    o_ref[:, W_IN_ROPE_AT + MLA_ROPE:W_IN_REST_AT] = (
        jnp.zeros((rows, W_IN_REST_AT - W_IN_ROPE_AT - MLA_ROPE), BF16))
    o_ref[:, W_IN_REST_AT:] = x[:, W_IN_SPLIT + MLA_ROPE:].astype(BF16)


def _pad_w_in(w_in, *, rows=256):
    lyr, d, n = w_in.shape
    return pl.pallas_call(
        _pad_w_in_kernel,
        grid=(lyr, d // rows),
        in_specs=[pl.BlockSpec((None, rows, n), lambda l, i: (l, i, 0))],
        out_specs=pl.BlockSpec((None, rows, IN_PAD), lambda l, i: (l, i, 0)),
        out_shape=jax.ShapeDtypeStruct((lyr, d, IN_PAD), BF16),
        compiler_params=_cparams("parallel", "parallel"),
        name="pad_w_in",
    )(w_in)


def _pad_wt_uq(w_uq):
    lyr = w_uq.shape[0]
    w = w_uq.reshape(lyr, MLA_Q_LORA, MLA_HEADS, MLA_NOPE + MLA_ROPE)
    w = jnp.pad(w, ((0, 0), (0, 0), (0, 0), (0, MLA_QK_PAD - MLA_NOPE - MLA_ROPE)))
    w = w.reshape(lyr, MLA_Q_LORA, MLA_HEADS * MLA_QK_PAD)
    return w.transpose(0, 2, 1).astype(BF16)


def _split_w_ukv(w_ukv):
    lyr = w_ukv.shape[0]
    w = w_ukv.reshape(lyr, MLA_KV_LORA, MLA_HEADS, MLA_NOPE + MLA_V)
    wk = jnp.pad(w[..., :MLA_NOPE], ((0, 0), (0, 0), (0, 0), (0, MLA_QK_PAD - MLA_NOPE)))
    wk = wk.reshape(lyr, MLA_KV_LORA, -1).astype(BF16)
    wv = w[..., MLA_NOPE:].reshape(lyr, MLA_KV_LORA, -1)
    return wk, wv.transpose(0, 2, 1).astype(BF16)


def _rope_angles():
    inv_freq = ROPE_THETA ** (-jnp.arange(ROPE_HALF, dtype=F32) / ROPE_HALF)
    ang = jnp.arange(SEQ, dtype=F32)[:, None] * inv_freq[None, :]
    return jnp.cos(ang), jnp.sin(ang)


def _rope_tables_k():
    cos, sin = _rope_angles()
    one = jnp.ones((SEQ, MLA_NOPE), F32)
    z = lambda n: jnp.zeros((SEQ, n), F32)
    tail = MLA_QK_PAD - MLA_NOPE - MLA_ROPE
    cos_t = jnp.concatenate([one, cos, cos, z(tail)], axis=1)
    sin_a = jnp.concatenate([z(MLA_NOPE), -sin, z(ROPE_HALF), z(tail)], axis=1)
    sin_b = jnp.concatenate([z(MLA_NOPE), z(ROPE_HALF), sin, z(tail)], axis=1)
    return cos_t, sin_a, sin_b


def kernel(x, g_mix, w_in, g_q, w_uq, g_kv, w_ukv, a_re, a_im, b_re, b_im, c_re, c_im,
           d_skip, log_dt, w_glu, b_glu, g_out_mla, g_out_ssm, g_out_dil, w_o,
           g_ffn, w_gate, w_up, w_down, g_final):
    x = x.reshape(TOKENS, D_MODEL)
    w_in_p = _pad_w_in(w_in)
    wt_q = _pad_wt_uq(w_uq)
    w_k, wt_v = _split_w_ukv(w_ukv)
    w_o_b = w_o.astype(BF16)
    q_scale = (MLA_NOPE + MLA_ROPE) ** -0.5 * math.log2(math.e)
    cos, sin = _rope_angles()
    q_cos_t, q_sin_t = (cos * q_scale).T, (sin * q_scale).T
    k_tabs = _rope_tables_k()
    ssm_mats = jax.vmap(_ssm_matrices)(a_re, a_im, b_re, b_im, c_re, c_im, d_skip, log_dt)

    for l in range(DEPTH):
        latent, proj = _in_proj(x, g_mix, w_in_p, l, bm=512)
        qt = _q_up(latent, g_q, wt_q, q_cos_t, q_sin_t, l, q_scale, bm=1024)
        k, vt = _kv_up(latent, g_kv, w_k, wt_v, k_tabs, l, bm=1024)
        y_mla = _mla_attention(qt, k, vt)
        y_ssm = _glu(_ssm_scan(proj, ssm_mats, l), w_glu, b_glu, l, bm=1024)
        y_dil = _dil_attention(proj)
        x = _out_proj(y_mla, y_ssm, y_dil, g_out_mla, g_out_ssm, g_out_dil,
                      w_o_b, x, l, bm=512, bn=D_MODEL)
        act, w_down_b = _ffn_up(x, g_ffn, w_gate, w_up, w_down, l, bm=1024, bn=512)
        x = _ffn_down(act, w_down_b, x, bm=1024, bn=512)
    out = _final_norm(x, g_final, bm=512)
    return out.reshape(BATCH, SEQ, D_MODEL)
```
